```python
import jax, jax.numpy as jnp
from jax import lax
import numpy as np

D_MODEL = 2048
BATCH = 4
SEQ = 4096
DEPTH = 1

N_META = 16
HEAD_DIM = 64
RWKV_HEADS = 16
RWKV_WIDTH = RWKV_HEADS * HEAD_DIM
FOX_HEADS = 16
FOX_WIDTH = FOX_HEADS * HEAD_DIM
DECAY_LORA = 96
AAA_LORA = 96
GATE_LORA = 256
D_FF = -(-8 * D_MODEL // (3 * 256)) * 256
Q_BLOCK = 128
RMS_EPS = 1e-6
GN_EPS = 64e-5
ATTN_SCALE = HEAD_DIM ** -0.5

RWKV_SPLITS = (RWKV_WIDTH, 2 * RWKV_WIDTH, 3 * RWKV_WIDTH,
               3 * RWKV_WIDTH + DECAY_LORA, 3 * RWKV_WIDTH + DECAY_LORA + AAA_LORA)
RWKV_COLS = 3 * RWKV_WIDTH + DECAY_LORA + AAA_LORA + GATE_LORA
FOX_SPLITS = (FOX_WIDTH, 2 * FOX_WIDTH, 3 * FOX_WIDTH)
FOX_COLS = 3 * FOX_WIDTH + FOX_HEADS
N_IN = RWKV_COLS + FOX_COLS + 2 * D_MODEL

kernel_name = 'hybrid_rwkv7_fox_meta_gated_block'

F32 = jnp.float32


def _rms(x, g):
    xf = x.astype(F32)
    xf = xf * lax.rsqrt(jnp.mean(xf * xf, axis=-1, keepdims=True) + RMS_EPS)
    return xf.astype(x.dtype) * g


def _wkv7_scan(r, w, k, v, a, b):
    B, L, H, N = r.shape

    def step(S, inp):
        r_t, w_t, k_t, v_t, a_t, b_t = inp
        sa = jnp.einsum('bhvk,bhk->bhv', S, a_t)
        S = (S * w_t[:, :, None, :] + sa[..., None] * b_t[:, :, None, :]
             + v_t[..., None] * k_t[:, :, None, :])
        return S, jnp.einsum('bhvk,bhk->bhv', S, r_t)

    xs = tuple(jnp.swapaxes(t, 0, 1) for t in (r, w, k, v, a, b))
    _, y = lax.scan(step, jnp.zeros((B, H, N, N), F32), xs)
    return jnp.swapaxes(y, 0, 1)


def _rwkv7(z, w0, w2, a0, a2, g2, k_k, k_a, r_k, gn_w, gn_b):
    B, L, _ = z.shape
    r, k, v, wd, ad, gd = jnp.split(z, RWKV_SPLITS, axis=-1)
    w_log = -jax.nn.softplus(-(w0 + jnp.tanh(wd) @ w2).astype(F32)) - 0.5
    decay = jnp.exp(-jnp.exp(w_log))
    a = jax.nn.sigmoid((a0 + ad @ a2).astype(F32))
    g = jax.nn.sigmoid(gd) @ g2
    heads = lambda t: t.reshape(B, L, RWKV_HEADS, HEAD_DIM)
    kk = heads((k * k_k).astype(F32))
    kk = kk / jnp.maximum(jnp.sqrt(jnp.sum(kk * kk, axis=-1, keepdims=True)), 1e-12)
    kf = k.astype(F32) * (1.0 + (a - 1.0) * k_a.astype(F32))
    rh, kh, vh, ah, dh = heads(r.astype(F32)), heads(kf), heads(v.astype(F32)), heads(a), heads(decay)
    y = _wkv7_scan(rh, dh, kh, vh, -kk, kk * ah)
    mu = jnp.mean(y, axis=-1, keepdims=True)
    var = jnp.mean(jnp.square(y - mu), axis=-1, keepdims=True)
    y = (y - mu) * lax.rsqrt(var + GN_EPS)
    y = y * gn_w.astype(F32).reshape(RWKV_HEADS, HEAD_DIM) + gn_b.astype(F32).reshape(RWKV_HEADS, HEAD_DIM)
    y = y + jnp.sum(rh * kh * r_k.astype(F32), axis=-1, keepdims=True) * vh
    return y.reshape(B, L, RWKV_WIDTH).astype(z.dtype) * g


def _fox(z, q_g, k_g, f_bias):
    B, L, _ = z.shape
    q, k, v, fl = jnp.split(z, FOX_SPLITS, axis=-1)
    q = _rms(q.reshape(B, L, FOX_HEADS, HEAD_DIM), q_g)
    k = _rms(k.reshape(B, L, FOX_HEADS, HEAD_DIM), k_g)
    v = v.reshape(B, L, FOX_HEADS, HEAD_DIM)
    log_f = jax.nn.log_sigmoid(fl.astype(F32) + f_bias.astype(F32))
    c = jnp.swapaxes(jnp.cumsum(log_f, axis=1), 1, 2)
    bounds = [(0, N_META)] + [(s, min(s + Q_BLOCK, L)) for s in range(N_META, L, Q_BLOCK)]
    outs = []
    for s, e in bounds:
        sc = jnp.einsum('bqhd,bkhd->bhqk', q[:, s:e], k[:, :e]).astype(F32) * ATTN_SCALE
        sc = sc + c[:, :, s:e, None] - c[:, :, None, :e]
        causal = jnp.arange(s, e)[:, None] >= jnp.arange(e)[None, :]
        p = jax.nn.softmax(jnp.where(causal, sc, -jnp.inf), axis=-1)
        outs.append(jnp.einsum('bhqk,bkhd->bqhd', p.astype(v.dtype), v[:, :e]))
    return jnp.concatenate(outs, axis=1).reshape(B, L, FOX_WIDTH)


def _layer(h, n1, w_in, mu, w0, w2, a0, a2, g2, k_k, k_a, r_k, gn_w, gn_b,
           q_g, k_g, f_bias, w_a, w_b, w_o, n2, w_gu, w_dn):
    xn = _rms(h, n1)
    proj = xn @ w_in
    z_rwkv, z_fox, z_gate = jnp.split(proj, (RWKV_COLS, RWKV_COLS + FOX_COLS), axis=-1)
    z_prev = jnp.pad(z_rwkv, ((0, 0), (1, 0), (0, 0)))[:, :-1]
    z_rwkv = z_rwkv + (z_prev - z_rwkv) * mu
    y_a = _rwkv7(z_rwkv, w0, w2, a0, a2, g2, k_k, k_a, r_k, gn_w, gn_b)
    y_b = _fox(z_fox, q_g, k_g, f_bias)
    gates = jax.nn.sigmoid(z_gate.astype(F32)).astype(h.dtype)
    g_a, g_b = jnp.split(gates, 2, axis=-1)
    merged = g_a * (y_a @ w_a) + g_b * (y_b @ w_b)
    h = h + merged @ w_o
    gate, up = jnp.split(_rms(h, n2) @ w_gu, 2, axis=-1)
    return h + (jax.nn.silu(gate) * up) @ w_dn


def setup_inputs(seed: int = 0) -> dict:
    key = jax.random.key(seed)
    ks = jax.random.split(key, 24)
    nrm = lambda k, shape, scale: jax.random.normal(k, shape, F32) * scale
    Dp = DEPTH
    return {
        'x': nrm(ks[0], (BATCH, SEQ, D_MODEL), 1.0),
        'meta_tokens': nrm(ks[1], (N_META, D_MODEL), 1.0),
        'norm1_g': 1.0 + nrm(ks[2], (Dp, D_MODEL), 0.05),
        'w_in': nrm(ks[3], (Dp, D_MODEL, N_IN), D_MODEL ** -0.5),
        'rwkv_mu': jax.random.uniform(ks[4], (Dp, RWKV_COLS), F32, 0.0, 1.0),
        'rwkv_w0': jax.random.uniform(ks[5], (Dp, RWKV_WIDTH), F32, -5.0, -1.0),
        'rwkv_w2': nrm(ks[6], (Dp, DECAY_LORA, RWKV_WIDTH), 0.5 * DECAY_LORA ** -0.5),
        'rwkv_a0': nrm(ks[7], (Dp, RWKV_WIDTH), 0.1),
        'rwkv_a2': nrm(ks[8], (Dp, AAA_LORA, RWKV_WIDTH), 0.5 * AAA_LORA ** -0.5),
        'rwkv_g2': nrm(ks[9], (Dp, GATE_LORA, RWKV_WIDTH), GATE_LORA ** -0.5),
        'rwkv_k_k': 0.85 + nrm(ks[10], (Dp, RWKV_WIDTH), 0.05),
        'rwkv_k_a': 1.0 + nrm(ks[11], (Dp, RWKV_WIDTH), 0.05),
        'rwkv_r_k': nrm(ks[12], (Dp, RWKV_HEADS, HEAD_DIM), 0.1),
        'rwkv_gn_w': 1.0 + nrm(ks[13], (Dp, RWKV_WIDTH), 0.05),
        'rwkv_gn_b': nrm(ks[14], (Dp, RWKV_WIDTH), 0.01),
        'fox_q_norm_g': 1.0 + nrm(ks[15], (Dp, HEAD_DIM), 0.05),
        'fox_k_norm_g': 1.0 + nrm(ks[16], (Dp, HEAD_DIM), 0.05),
        'fox_f_bias': jax.random.uniform(ks[17], (Dp, FOX_HEADS), F32, 1.0, 4.0),
        'w_branch_a': nrm(ks[18], (Dp, RWKV_WIDTH, D_MODEL), RWKV_WIDTH ** -0.5),
        'w_branch_b': nrm(ks[19], (Dp, FOX_WIDTH, D_MODEL), FOX_WIDTH ** -0.5),
        'w_o': nrm(ks[20], (Dp, D_MODEL, D_MODEL), D_MODEL ** -0.5),
        'norm2_g': 1.0 + nrm(ks[21], (Dp, D_MODEL), 0.05),
        'w_gate_up': nrm(ks[22], (Dp, D_MODEL, 2 * D_FF), D_MODEL ** -0.5),
        'w_down': nrm(ks[23], (Dp, D_FF, D_MODEL), D_FF ** -0.5),
    }


def reference(x, meta_tokens, norm1_g, w_in, rwkv_mu, rwkv_w0, rwkv_w2, rwkv_a0, rwkv_a2,
              rwkv_g2, rwkv_k_k, rwkv_k_a, rwkv_r_k, rwkv_gn_w, rwkv_gn_b, fox_q_norm_g,
              fox_k_norm_g, fox_f_bias, w_branch_a, w_branch_b, w_o, norm2_g, w_gate_up, w_down):
    B = x.shape[0]
    meta = jnp.broadcast_to(meta_tokens.astype(x.dtype)[None], (B, N_META, D_MODEL))
    h = jnp.concatenate([meta, x], axis=1)
    for l in range(DEPTH):
        h = _layer(h, norm1_g[l], w_in[l], rwkv_mu[l], rwkv_w0[l], rwkv_w2[l], rwkv_a0[l],
                   rwkv_a2[l], rwkv_g2[l], rwkv_k_k[l], rwkv_k_a[l], rwkv_r_k[l], rwkv_gn_w[l],
                   rwkv_gn_b[l], fox_q_norm_g[l], fox_k_norm_g[l], fox_f_bias[l], w_branch_a[l],
                   w_branch_b[l], w_o[l], norm2_g[l], w_gate_up[l], w_down[l])
    return h[:, N_META:]
```

```python
import functools

import jax
import jax.numpy as jnp
from jax import lax
from jax.experimental import pallas as pl
from jax.experimental.pallas import tpu as pltpu

F32 = jnp.float32
BF16 = jnp.bfloat16

D_MODEL = 2048
N_META = 16
HEAD_DIM = 64
HEADS = 16
WIDTH = HEADS * HEAD_DIM
PAIRS = HEADS // 2
DECAY_LORA = 96
AAA_LORA = 96
GATE_LORA = 256
D_FF = 5632
RMS_EPS = 1e-6
GN_EPS = 64e-5
ATTN_SCALE = HEAD_DIM ** -0.5

LANES = 128
META_ROWS = 128
CHUNK = 64
MASKED_KEY = -1e30

C_R, C_K, C_V = 0, 1024, 2048
C_FQ, C_FK, C_FV = 3072, 4096, 5120
C_GA, C_GB = 6144, 8192
C_S = 10240
S_COLS = 512
FL_LANE = 96
N_PACK = C_S + S_COLS

VMEM_LIMIT = 56 * 1024 * 1024


def _sigmoid(x):
    return 1.0 / (1.0 + jnp.exp(-x))


def _softplus(x):
    return jnp.maximum(x, 0.0) + jnp.log(1.0 + jnp.exp(-jnp.abs(x)))


def _dot(a, b):
    return jnp.dot(a, b, preferred_element_type=F32)


def _dot_nt(a, b):
    return lax.dot_general(a, b, (((1,), (1,)), ((), ())), preferred_element_type=F32)


def _dot_tn(a, b):
    return lax.dot_general(a, b, (((0,), (0,)), ((), ())), preferred_element_type=F32)


def _split3(x):
    hi = x.astype(BF16)
    r1 = x - hi.astype(F32)
    mid = r1.astype(BF16)
    lo = (r1 - mid.astype(F32)).astype(BF16)
    return hi, mid, lo


def _split2(x):
    hi = x.astype(BF16)
    lo = (x - hi.astype(F32)).astype(BF16)
    return hi, lo


def _mm(a, b, dn, passes):
    dg = functools.partial(lax.dot_general, dimension_numbers=dn, preferred_element_type=F32)
    if passes == 1:
        return dg(a.astype(BF16), b.astype(BF16))
    ah, al = _split2(a)
    bh, bl = _split2(b)
    return dg(ah, bh) + (dg(ah, bl) + dg(al, bh))


_NN = (((1,), (0,)), ((), ()))
_NT = (((1,), (1,)), ((), ()))
_TN = (((0,), (0,)), ((), ()))


def _head_ones():
    r = lax.broadcasted_iota(jnp.int32, (LANES, LANES), 0) // HEAD_DIM
    c = lax.broadcasted_iota(jnp.int32, (LANES, LANES), 1) // HEAD_DIM
    return jnp.where(r == c, 1.0, 0.0).astype(BF16)


def _head_sum(x, ones_bd):
    hi, mid, lo = _split3(x)
    return _dot(jnp.concatenate([hi, mid, lo], axis=1),
                jnp.concatenate([ones_bd, ones_bd, ones_bd], axis=0))


def _rms_mm_kernel(x_ref, g_ref, w_ref, o_ref, xn_ref):
    @pl.when(pl.program_id(1) == 0)
    def _():
        x = x_ref[...]
        ms = jnp.mean(x * x, axis=-1, keepdims=True)
        xn_ref[...] = ((x * lax.rsqrt(ms + RMS_EPS)) * g_ref[...]).astype(BF16)

    o_ref[...] = _dot(xn_ref[...], w_ref[...]).astype(o_ref.dtype)


def _rms_glu_kernel(x_ref, g_ref, wg_ref, wu_ref, o_ref, xn_ref):
    @pl.when(pl.program_id(1) == 0)
    def _():
        x = x_ref[...]
        ms = jnp.mean(x * x, axis=-1, keepdims=True)
        xn_ref[...] = ((x * lax.rsqrt(ms + RMS_EPS)) * g_ref[...]).astype(BF16)

    xn = xn_ref[...]
    gate = _dot(xn, wg_ref[...])
    up = _dot(xn, wu_ref[...])
    o_ref[...] = (gate * _sigmoid(gate) * up).astype(o_ref.dtype)


def _rms_matmul(x2d, g, w_bf16, *, tm, tn, out_dtype=F32):
    m, d = x2d.shape
    n = w_bf16.shape[1]
    return pl.pallas_call(
        _rms_mm_kernel,
        grid=(m // tm, n // tn),
        in_specs=[pl.BlockSpec((tm, d), lambda i, j: (i, 0)),
                  pl.BlockSpec((1, d), lambda i, j: (0, 0)),
                  pl.BlockSpec((d, tn), lambda i, j: (0, j))],
        out_specs=pl.BlockSpec((tm, tn), lambda i, j: (i, j)),
        out_shape=jax.ShapeDtypeStruct((m, n), out_dtype),
        scratch_shapes=[pltpu.VMEM((tm, d), BF16)],
        compiler_params=pltpu.CompilerParams(
            dimension_semantics=("parallel", "arbitrary"), vmem_limit_bytes=VMEM_LIMIT),
        name="rms_matmul",
    )(x2d, g.reshape(1, d), w_bf16)


def _rms_matmul_glu(x2d, g, w_bf16, *, tm, tn):
    m, d = x2d.shape
    n = w_bf16.shape[1] // 2
    up_off = n // tn
    return pl.pallas_call(
        _rms_glu_kernel,
        grid=(m // tm, n // tn),
        in_specs=[pl.BlockSpec((tm, d), lambda i, j: (i, 0)),
                  pl.BlockSpec((1, d), lambda i, j: (0, 0)),
                  pl.BlockSpec((d, tn), lambda i, j: (0, j)),
                  pl.BlockSpec((d, tn), lambda i, j: (0, j + up_off))],
        out_specs=pl.BlockSpec((tm, tn), lambda i, j: (i, j)),
        out_shape=jax.ShapeDtypeStruct((m, n), BF16),
        scratch_shapes=[pltpu.VMEM((tm, d), BF16)],
        compiler_params=pltpu.CompilerParams(
            dimension_semantics=("parallel", "arbitrary"), vmem_limit_bytes=VMEM_LIMIT),
        name="rms_matmul_glu",
    )(x2d, g.reshape(1, d), w_bf16, w_bf16)


def _mm_res_kernel(a_ref, b_ref, r_ref, o_ref, acc_ref, *, nk):
    k = pl.program_id(2)

    @pl.when(k == 0)
    def _():
        acc_ref[...] = jnp.zeros_like(acc_ref)

    acc_ref[...] += _dot(a_ref[...], b_ref[...])

    @pl.when(k == nk - 1)
    def _():
        o_ref[...] = acc_ref[...] + r_ref[...]


def _matmul_res(a, b, res, *, tm, tn, tk):
    m, kk = a.shape
    n = b.shape[1]
    nk = kk // tk
    return pl.pallas_call(
        functools.partial(_mm_res_kernel, nk=nk),
        grid=(m // tm, n // tn, nk),
        in_specs=[pl.BlockSpec((tm, tk), lambda i, j, k: (i, k)),
                  pl.BlockSpec((tk, tn), lambda i, j, k: (k, j)),
                  pl.BlockSpec((tm, tn), lambda i, j, k: (i, j))],
        out_specs=pl.BlockSpec((tm, tn), lambda i, j, k: (i, j)),
        out_shape=jax.ShapeDtypeStruct((m, n), F32),
        scratch_shapes=[pltpu.VMEM((tm, tn), F32)],
        compiler_params=pltpu.CompilerParams(
            dimension_semantics=("parallel", "parallel", "arbitrary"), vmem_limit_bytes=VMEM_LIMIT),
        name="matmul_res",
    )(a, b, res)


def _merge_kernel(ya_ref, yb_ref, wa_ref, wb_ref, za_ref, zb_ref, o_ref):
    ta = _dot(ya_ref[...], wa_ref[...])
    tb = _dot(yb_ref[...], wb_ref[...])
    o_ref[...] = (_sigmoid(za_ref[...]) * ta + _sigmoid(zb_ref[...]) * tb).astype(o_ref.dtype)


def _merge(ya, yb, wa, wb, proj2d, *, tm, tn):
    m, kk = ya.shape
    n = wa.shape[1]
    oa, ob = C_GA // tn, C_GB // tn
    return pl.pallas_call(
        _merge_kernel,
        grid=(m // tm, n // tn),
        in_specs=[pl.BlockSpec((tm, kk), lambda i, j: (i, 0)),
                  pl.BlockSpec((tm, kk), lambda i, j: (i, 0)),
                  pl.BlockSpec((kk, tn), lambda i, j: (0, j)),
                  pl.BlockSpec((kk, tn), lambda i, j: (0, j)),
                  pl.BlockSpec((tm, tn), lambda i, j: (i, j + oa)),
                  pl.BlockSpec((tm, tn), lambda i, j: (i, j + ob))],
        out_specs=pl.BlockSpec((tm, tn), lambda i, j: (i, j)),
        out_shape=jax.ShapeDtypeStruct((m, n), BF16),
        compiler_params=pltpu.CompilerParams(
            dimension_semantics=("parallel", "arbitrary"), vmem_limit_bytes=VMEM_LIMIT),
        name="merge",
    )(ya, yb, wa, wb, proj2d, proj2d)


def _rwkv_prep_kernel(z_ref, zs_ref, pz_ref, ps_ref, p0z_ref, p0s_ref, mu_ref, mus_ref,
                      w0_ref, w2_ref, a0_ref, a2_ref, g2_ref, kk_ref, ka_ref,
                      r_out, lw_out, k_out, v_out, a_out, b_out, g_out, *, tt):
    first = pl.program_id(1) == 0
    z = z_ref[0]
    zs = zs_ref[0]
    pz = jnp.where(first, p0z_ref[7:8, :], pz_ref[0, 7:8, :])
    ps = jnp.where(first, p0s_ref[7:8, :], ps_ref[0, 7:8, :])
    row = lax.broadcasted_iota(jnp.int32, (tt, 1), 0)

    def lerp(cur, prev_last, mu):
        prev = jnp.where(row == 0, prev_last, pltpu.roll(cur, 1, axis=0))
        return cur + (prev - cur) * mu

    zm = lerp(z, pz, mu_ref[...])
    zsm = lerp(zs, ps, mus_ref[...])
    r = zm[:, C_R:C_R + WIDTH]
    k = zm[:, C_K:C_K + WIDTH]
    v = zm[:, C_V:C_V + WIDTH]
    wd = zsm[:, 0:LANES]
    ad = zsm[:, LANES:2 * LANES]
    gd = zsm[:, 2 * LANES:4 * LANES]

    w_log = -_softplus(-(w0_ref[...] + _dot(jnp.tanh(wd).astype(BF16), w2_ref[...]))) - 0.5
    lw_out[0] = -jnp.exp(w_log)
    a_sig = _sigmoid(a0_ref[...] + _dot(ad.astype(BF16), a2_ref[...]))
    g_out[0] = _dot(_sigmoid(gd).astype(BF16), g2_ref[...])

    ones_bd = _head_ones()
    kk = k * kk_ref[...]
    kk2 = kk * kk
    nrm = jnp.concatenate(
        [_head_sum(kk2[:, p * LANES:(p + 1) * LANES], ones_bd) for p in range(PAIRS)], axis=1)
    kk = kk / jnp.maximum(jnp.sqrt(nrm), 1e-12)
    r_out[0] = r
    k_out[0] = k * (1.0 + (a_sig - 1.0) * ka_ref[...])
    v_out[0] = v
    a_out[0] = -kk
    b_out[0] = kk * a_sig


def _rwkv_prep(proj3d, prev0_z, prev0_s, mu_z, mu_s, w0, w2p, a0, a2p, g2, k_k, k_a, *, tt):
    b, l, _ = proj3d.shape
    s_blk = C_S // S_COLS
    zw = 3 * WIDTH
    row = lambda a: a.reshape(1, -1)
    vec = lambda n: pl.BlockSpec((1, n), lambda bi, i: (0, 0))
    full = lambda r, c: pl.BlockSpec((r, c), lambda bi, i: (0, 0))
    out_spec = pl.BlockSpec((1, tt, WIDTH), lambda bi, i: (bi, i, 0))
    out_sds = jax.ShapeDtypeStruct((b, l, WIDTH), F32)
    return pl.pallas_call(
        functools.partial(_rwkv_prep_kernel, tt=tt),
        grid=(b, l // tt),
        in_specs=[pl.BlockSpec((1, tt, zw), lambda bi, i: (bi, i, 0)),
                  pl.BlockSpec((1, tt, S_COLS), lambda bi, i: (bi, i, s_blk)),
                  pl.BlockSpec((1, 8, zw), lambda bi, i: (bi, jnp.maximum(i * (tt // 8) - 1, 0), 0)),
                  pl.BlockSpec((1, 8, S_COLS),
                               lambda bi, i: (bi, jnp.maximum(i * (tt // 8) - 1, 0), s_blk)),
                  full(8, zw), full(8, S_COLS),
                  vec(zw), vec(S_COLS),
                  vec(WIDTH), full(LANES, WIDTH), vec(WIDTH), full(LANES, WIDTH),
                  full(GATE_LORA, WIDTH), vec(WIDTH), vec(WIDTH)],
        out_specs=[out_spec] * 7,
        out_shape=[out_sds] * 7,
        compiler_params=pltpu.CompilerParams(
            dimension_semantics=("parallel", "arbitrary"), vmem_limit_bytes=VMEM_LIMIT),
        name="rwkv_prep",
    )(proj3d, proj3d, proj3d, proj3d, prev0_z, prev0_s, row(mu_z), row(mu_s),
      row(w0), w2p, row(a0), a2p, g2, row(k_k), row(k_a))


WKV_PASSES_G = 3
WKV_PASSES_INV = 3
WKV_PASSES_OUT = 1


def _wkv_kernel(r_ref, lw_ref, k_ref, v_ref, a_ref, b_ref, g_ref, s0_ref,
                gnw_ref, gnb_ref, rk_ref, y_ref, sfin_ref, s_scr, *, nc):
    c = pl.program_id(2)

    @pl.when(c == 0)
    def _():
        s_scr[...] = s0_ref[0, 0]

    r = r_ref[0]
    lw = lw_ref[0]
    k = k_ref[0]
    v = v_ref[0]
    a = a_ref[0]
    b = b_ref[0]

    ti = lax.broadcasted_iota(jnp.int32, (CHUNK, CHUNK), 0)
    tj = lax.broadcasted_iota(jnp.int32, (CHUNK, CHUNK), 1)
    ltri = jnp.where(tj <= ti, 1.0, 0.0).astype(BF16)
    hi, mid, lo = _split3(lw)
    cum = _dot(jnp.concatenate([ltri, ltri, ltri], axis=1),
               jnp.concatenate([hi, mid, lo], axis=0))
    total = cum[CHUNK - 1:CHUNK, :]
    e_in = jnp.exp(cum)
    e_inv = jnp.exp(-cum)
    e_end = jnp.exp(total - cum)
    a_t = a * jnp.exp(cum - lw)
    r_t = r * e_in
    b_t = b * e_inv
    k_t = k * e_inv
    b_h = b * e_end
    k_h = k * e_end

    lane = lax.broadcasted_iota(jnp.int32, (1, LANES), 1)
    head0 = lane < HEAD_DIM

    def bd(x):
        return jnp.concatenate([jnp.where(head0, x, 0.0), jnp.where(head0, 0.0, x)], axis=0)

    lhs = jnp.concatenate([bd(a_t), bd(r_t)], axis=0)
    rhs = jnp.concatenate([bd(b_t), bd(k_t)], axis=0)
    gram = _mm(lhs, rhs, _NT, WKV_PASSES_G)

    bi = lax.broadcasted_iota(jnp.int32, (LANES, LANES), 0)
    bj = lax.broadcasted_iota(jnp.int32, (LANES, LANES), 1)
    same = (bi // CHUNK) == (bj // CHUNK)
    strict = same & ((bj % CHUNK) < (bi % CHUNK))
    lower = same & ((bj % CHUNK) <= (bi % CHUNK))
    n_ab = jnp.where(strict, gram[0:LANES, 0:LANES], 0.0)
    a_ak = jnp.where(strict, gram[0:LANES, LANES:2 * LANES], 0.0)
    a_rb = jnp.where(lower, gram[LANES:2 * LANES, 0:LANES], 0.0)
    a_rk = jnp.where(lower, gram[LANES:2 * LANES, LANES:2 * LANES], 0.0)

    s = s_scr[...]
    a_s = _mm(lhs, s, _NT, WKV_PASSES_OUT)
    v_bd = bd(v)
    rhs_u = a_s[0:LANES] + _mm(a_ak, v_bd, _NN, WKV_PASSES_OUT)

    eye = jnp.where(bi == bj, 1.0, 0.0)
    p_inv = eye + n_ab
    q = n_ab
    for _ in range(5):
        q = _mm(q, q, _NN, WKV_PASSES_INV)
        p_inv = p_inv + _mm(q, p_inv, _NN, WKV_PASSES_INV)
    u = _mm(p_inv, rhs_u, _NN, WKV_PASSES_INV)

    uv = jnp.concatenate([u, v_bd], axis=0)
    y_bd = a_s[LANES:2 * LANES] + _mm(jnp.concatenate([a_rb, a_rk], axis=1), uv, _NN, WKV_PASSES_OUT)
    y = y_bd[0:CHUNK] + y_bd[CHUNK:2 * CHUNK]

    s_new = jnp.exp(total) * s + _mm(uv, jnp.concatenate([bd(b_h), bd(k_h)], axis=0), _TN,
                                     WKV_PASSES_OUT)
    s_scr[...] = s_new

    @pl.when(c == nc - 1)
    def _():
        sfin_ref[0, 0] = s_new

    ones_bd = _head_ones()
    inv_n = 1.0 / HEAD_DIM
    mu = _head_sum(y, ones_bd) * inv_n
    d = y - mu
    var = _head_sum(d * d, ones_bd) * inv_n
    yn = d * lax.rsqrt(var + GN_EPS) * gnw_ref[...] + gnb_ref[...]
    bonus = _head_sum(r * k * rk_ref[...], ones_bd) * v
    y_ref[0] = ((yn + bonus) * g_ref[0]).astype(y_ref.dtype)


def _wkv(r, lw, k, v, a, b, g, s0, gn_w, gn_b, r_k):
    bsz, l, _ = r.shape
    nc = l // CHUNK
    tok = pl.BlockSpec((1, CHUNK, LANES), lambda bi, p, c: (bi, c, p))
    vec = pl.BlockSpec((1, LANES), lambda bi, p, c: (0, p))
    st_in = pl.BlockSpec((1, 1, LANES, LANES), lambda bi, p, c: (0, p, 0, 0))
    st_out = pl.BlockSpec((1, 1, LANES, LANES), lambda bi, p, c: (bi, p, 0, 0))
    row = lambda x: x.reshape(1, WIDTH)
    return pl.pallas_call(
        functools.partial(_wkv_kernel, nc=nc),
        grid=(bsz, PAIRS, nc),
        in_specs=[tok] * 7 + [st_in, vec, vec, vec],
        out_specs=[tok, st_out],
        out_shape=[jax.ShapeDtypeStruct((bsz, l, WIDTH), BF16),
                   jax.ShapeDtypeStruct((bsz, PAIRS, LANES, LANES), F32)],
        scratch_shapes=[pltpu.VMEM((LANES, LANES), F32)],
        compiler_params=pltpu.CompilerParams(
            dimension_semantics=("parallel", "parallel", "arbitrary"), vmem_limit_bytes=VMEM_LIMIT),
        name="wkv",
    )(r, lw, k, v, a, b, g, s0, row(gn_w), row(gn_b), row(r_k))


def _fox_prep_kernel(z_ref, zs_ref, c0_ref, qg_ref, kg_ref, fb_ref,
                     q_out, k_out, v_out, c_out, carry, *, tt, n_pad):
    i = pl.program_id(1)

    @pl.when(i == 0)
    def _():
        carry[...] = c0_ref[...]

    z = z_ref[0]
    fl = zs_ref[0][:, 0:LANES]
    log_f = -_softplus(-(fl + fb_ref[...]))
    ti = lax.broadcasted_iota(jnp.int32, (tt, tt), 0)
    tj = lax.broadcasted_iota(jnp.int32, (tt, tt), 1)
    ltri = jnp.where(tj <= ti, 1.0, 0.0).astype(BF16)
    hi, mid, lo = _split3(log_f)
    cum = carry[...] + _dot(jnp.concatenate([ltri, ltri, ltri], axis=1),
                            jnp.concatenate([hi, mid, lo], axis=0))
    carry[...] = cum[tt - 1:tt, :]
    c_out[0] = cum[tt - 8:tt, :]
    c_hi, c_mid, c_lo = (t.astype(F32) for t in _split3(cum))
    if n_pad:
        grow = i * tt + lax.broadcasted_iota(jnp.int32, (tt, 1), 0)
        neg_hi = jnp.where(grow < n_pad, MASKED_KEY, -c_hi)
    else:
        neg_hi = -c_hi

    v_out[0] = z[:, 2 * WIDTH:3 * WIDTH].astype(BF16)
    ones_bd = _head_ones()
    lane = lax.broadcasted_iota(jnp.int32, (1, LANES), 1)
    inv_n = 1.0 / HEAD_DIM
    for p in range(PAIRS):
        zq = z[:, p * LANES:(p + 1) * LANES]
        zk = z[:, WIDTH + p * LANES:WIDTH + (p + 1) * LANES]
        qn = zq * lax.rsqrt(_head_sum(zq * zq, ones_bd) * inv_n + RMS_EPS) * (qg_ref[...] * ATTN_SCALE)
        kn = zk * lax.rsqrt(_head_sum(zk * zk, ones_bd) * inv_n + RMS_EPS) * kg_ref[...]
        for j in range(2):
            h = 2 * p + j
            own = (lane < HEAD_DIM) if j == 0 else (lane >= HEAD_DIM)
            e = lane - (HEAD_DIM if j == 0 else 0)
            col = lambda t: t[:, FL_LANE + h:FL_LANE + h + 1]
            q_extra = jnp.where(e == 0, col(c_hi), jnp.where(e == 1, col(c_mid), jnp.where(
                e == 2, col(c_lo), jnp.where((e >= 3) & (e < 6), 1.0, 0.0))))
            k_extra = jnp.where((e >= 0) & (e < 3), 1.0, jnp.where(e == 3, col(neg_hi), jnp.where(
                e == 4, -col(c_mid), jnp.where(e == 5, -col(c_lo), 0.0))))
            q_out[0, h] = jnp.where(own, qn, q_extra).astype(BF16)
            k_out[0, h] = jnp.where(own, kn, k_extra).astype(BF16)


def _fox_prep(proj3d, c0, q_g, k_g, f_bias, *, tt, n_pad):
    b, l, _ = proj3d.shape
    zw = 3 * WIDTH
    tile2 = lambda g: jnp.concatenate([g, g]).reshape(1, LANES)
    fb = jnp.concatenate([jnp.zeros((FL_LANE,), F32), f_bias,
                          jnp.zeros((LANES - FL_LANE - HEADS,), F32)]).reshape(1, LANES)
    hd = pl.BlockSpec((1, HEADS, tt, LANES), lambda bi, i: (bi, 0, i, 0))
    vec = pl.BlockSpec((1, LANES), lambda bi, i: (0, 0))
    return pl.pallas_call(
        functools.partial(_fox_prep_kernel, tt=tt, n_pad=n_pad),
        grid=(b, l // tt),
        in_specs=[pl.BlockSpec((1, tt, zw), lambda bi, i: (bi, i, C_FQ // zw)),
                  pl.BlockSpec((1, tt, S_COLS), lambda bi, i: (bi, i, C_S // S_COLS)),
                  vec, vec, vec, vec],
        out_specs=[hd, hd,
                   pl.BlockSpec((1, tt, WIDTH), lambda bi, i: (bi, i, 0)),
                   pl.BlockSpec((1, 8, LANES), lambda bi, i: (bi, i, 0))],
        out_shape=[jax.ShapeDtypeStruct((b, HEADS, l, LANES), BF16),
                   jax.ShapeDtypeStruct((b, HEADS, l, LANES), BF16),
                   jax.ShapeDtypeStruct((b, l, WIDTH), BF16),
                   jax.ShapeDtypeStruct((b, (l // tt) * 8, LANES), F32)],
        scratch_shapes=[pltpu.VMEM((1, LANES), F32)],
        compiler_params=pltpu.CompilerParams(
            dimension_semantics=("parallel", "arbitrary"), vmem_limit_bytes=VMEM_LIMIT),
        name="fox_prep",
    )(proj3d, proj3d, c0, tile2(q_g), tile2(k_g), fb)


def _attn_kernel(q_ref, k_ref, v_ref, km_ref, vm_ref, o_ref, *, tq):
    qi = pl.program_id(2)
    lane = lax.broadcasted_iota(jnp.int32, (1, LANES), 1)
    ri = lax.broadcasted_iota(jnp.int32, (tq, tq), 0)
    ci = lax.broadcasted_iota(jnp.int32, (tq, tq), 1)
    causal = ci <= ri
    outs = []
    for j in range(2):
        q = q_ref[0, j]

        def step(carry, kblk, vblk, mask):
            m, l, acc = carry
            s = _dot_nt(q, kblk)
            if mask is not None:
                s = jnp.where(mask, s, MASKED_KEY)
            m_new = jnp.maximum(m, jnp.max(s, axis=1, keepdims=True))
            alpha = jnp.exp(m - m_new)
            p = jnp.exp(s - m_new)
            l = alpha * l + jnp.sum(p, axis=1, keepdims=True)
            acc = alpha * acc + _dot(p.astype(BF16), vblk)
            return m_new, l, acc

        init = (jnp.full((tq, 1), MASKED_KEY, F32), jnp.zeros((tq, 1), F32),
                jnp.zeros((tq, LANES), F32))
        carry = step(init, km_ref[0, j], vm_ref[0], None)

        def body(kb, carry, j=j):
            off = pl.multiple_of(kb * tq, tq)
            return step(carry, k_ref[0, j, pl.ds(off, tq), :], v_ref[0, pl.ds(off, tq), :], None)

        carry = lax.fori_loop(0, qi, body, carry)
        off = pl.multiple_of(qi * tq, tq)
        m, l, acc = step(carry, k_ref[0, j, pl.ds(off, tq), :], v_ref[0, pl.ds(off, tq), :], causal)
        outs.append(acc / l)
    o_ref[0] = jnp.where(lane < HEAD_DIM, outs[0], outs[1]).astype(o_ref.dtype)


def _attn(qp, kp, v, kp_meta, v_meta, *, tq):
    b, _, l, _ = qp.shape
    lm = kp_meta.shape[2]
    return pl.pallas_call(
        functools.partial(_attn_kernel, tq=tq),
        grid=(b, PAIRS, l // tq),
        in_specs=[pl.BlockSpec((1, 2, tq, LANES), lambda bi, p, qi: (bi, p, qi, 0)),
                  pl.BlockSpec((1, 2, l, LANES), lambda bi, p, qi: (bi, p, 0, 0)),
                  pl.BlockSpec((1, l, LANES), lambda bi, p, qi: (bi, 0, p)),
                  pl.BlockSpec((1, 2, lm, LANES), lambda bi, p, qi: (0, p, 0, 0)),
                  pl.BlockSpec((1, lm, LANES), lambda bi, p, qi: (0, 0, p))],
        out_specs=pl.BlockSpec((1, tq, LANES), lambda bi, p, qi: (bi, qi, p)),
        out_shape=jax.ShapeDtypeStruct((b, l, WIDTH), BF16),
        compiler_params=pltpu.CompilerParams(
            dimension_semantics=("parallel", "parallel", "arbitrary"), vmem_limit_bytes=VMEM_LIMIT),
        name="attn",
    )(qp, kp, v, kp_meta, v_meta)


def _pick(n, prefs):
    for t in prefs:
        if n % t == 0:
            return t
    raise ValueError(f"no tile for {n}")


def _pack_w_in(w_in):
    rw, fx = 3 * WIDTH + DECAY_LORA + AAA_LORA + GATE_LORA, 3 * WIDTH + HEADS
    wd0, ad0, gd0 = 3 * WIDTH, 3 * WIDTH + DECAY_LORA, 3 * WIDTH + DECAY_LORA + AAA_LORA
    z = lambda n: jnp.zeros((w_in.shape[0], n), w_in.dtype)
    return jnp.concatenate([
        w_in[:, 0:3 * WIDTH], w_in[:, rw:rw + 3 * WIDTH], w_in[:, rw + fx:],
        w_in[:, wd0:ad0], w_in[:, rw + 3 * WIDTH:rw + fx], z(LANES - DECAY_LORA - HEADS),
        w_in[:, ad0:gd0], z(LANES - AAA_LORA), w_in[:, gd0:rw]], axis=1).astype(BF16)


def _pack_mu(mu):
    wd0, ad0, gd0 = 3 * WIDTH, 3 * WIDTH + DECAY_LORA, 3 * WIDTH + DECAY_LORA + AAA_LORA
    z = lambda n: jnp.zeros((n,), mu.dtype)
    mu_s = jnp.concatenate([mu[wd0:ad0], z(LANES - DECAY_LORA), mu[ad0:gd0], z(LANES - AAA_LORA),
                            mu[gd0:]])
    return mu[:3 * WIDTH], mu_s


def _pad_rows(w, rows):
    return jnp.concatenate([w, jnp.zeros((rows - w.shape[0], w.shape[1]), w.dtype)], axis=0)


def _mixer_inputs(h2d, bsz, norm_g, w_pack, tm):
    proj = _rms_matmul(h2d, norm_g, w_pack, tm=tm, tn=512)
    return proj, proj.reshape(bsz, h2d.shape[0] // bsz, N_PACK)


def kernel(x, meta_tokens, norm1_g, w_in, rwkv_mu, rwkv_w0, rwkv_w2, rwkv_a0, rwkv_a2, rwkv_g2,
           rwkv_k_k, rwkv_k_a, rwkv_r_k, rwkv_gn_w, rwkv_gn_b, fox_q_norm_g, fox_k_norm_g,
           fox_f_bias, w_branch_a, w_branch_b, w_o, norm2_g, w_gate_up, w_down):
    bsz, seq, d = x.shape
    assert d == D_MODEL and norm1_g.shape[0] == 1 and seq % 256 == 0
    m = bsz * seq
    x2d = x.reshape(m, d)
    meta2d = jnp.concatenate(
        [jnp.zeros((META_ROWS - N_META, d), x.dtype), meta_tokens.astype(x.dtype)], axis=0)

    w_pack = _pack_w_in(w_in[0])
    mu_z, mu_s = _pack_mu(rwkv_mu[0])
    w2p = _pad_rows(rwkv_w2[0], LANES).astype(BF16)
    a2p = _pad_rows(rwkv_a2[0], LANES).astype(BF16)
    g2 = rwkv_g2[0].astype(BF16)
    r_k = rwkv_r_k[0].reshape(WIDTH)

    tm = _pick(m, (1024, 512, 256))
    tt = _pick(seq, (256,))

    _, pm3 = _mixer_inputs(meta2d, 1, norm1_g[0], w_pack, META_ROWS)
    zero_z = jnp.zeros((8, 3 * WIDTH), F32)
    zero_s = jnp.zeros((8, S_COLS), F32)
    prep = functools.partial(_rwkv_prep, mu_z=mu_z, mu_s=mu_s, w0=rwkv_w0[0], w2p=w2p,
                             a0=rwkv_a0[0], a2p=a2p, g2=g2, k_k=rwkv_k_k[0], k_a=rwkv_k_a[0])
    wkv = functools.partial(_wkv, gn_w=rwkv_gn_w[0], gn_b=rwkv_gn_b[0], r_k=r_k)
    fox = functools.partial(_fox_prep, q_g=fox_q_norm_g[0], k_g=fox_k_norm_g[0],
                            f_bias=fox_f_bias[0])
    mr = prep(pm3, zero_z, zero_s, tt=META_ROWS)
    _, s_meta = wkv(*mr, jnp.zeros((1, PAIRS, LANES, LANES), F32))
    _, kp_meta, v_meta, c_meta = fox(pm3, jnp.zeros((1, LANES), F32), tt=META_ROWS,
                                     n_pad=META_ROWS - N_META)

    proj2d, p3 = _mixer_inputs(x2d, bsz, norm1_g[0], w_pack, tm)
    rr = prep(p3, pm3[0, META_ROWS - 8:, 0:3 * WIDTH], pm3[0, META_ROWS - 8:, C_S:], tt=tt)
    ya, _ = wkv(*rr, s_meta)
    qp, kp, v, _ = fox(p3, c_meta[0, 7:8, :], tt=tt, n_pad=0)
    yb = _attn(qp, kp, v, kp_meta, v_meta, tq=256)

    merged = _merge(ya.reshape(m, WIDTH), yb.reshape(m, WIDTH), w_branch_a[0].astype(BF16),
                    w_branch_b[0].astype(BF16), proj2d, tm=tm, tn=512)
    h1 = _matmul_res(merged, w_o[0].astype(BF16), x2d, tm=tm, tn=512, tk=D_MODEL)
    act = _rms_matmul_glu(h1, norm2_g[0], w_gate_up[0].astype(BF16), tm=tm, tn=512)
    out = _matmul_res(act, w_down[0].astype(BF16), h1, tm=tm, tn=512, tk=512)
    return out.reshape(bsz, seq, d)
```

```python
import functools

import jax
import jax.numpy as jnp
from jax import lax
from jax.experimental import pallas as pl
from jax.experimental.pallas import tpu as pltpu

F32 = jnp.float32
BF16 = jnp.bfloat16

D_MODEL = 2048
N_META = 16
HEAD_DIM = 64
HEADS = 16
WIDTH = HEADS * HEAD_DIM
PAIRS = HEADS // 2
DECAY_LORA = 96
AAA_LORA = 96
GATE_LORA = 256
D_FF = 5632
RMS_EPS = 1e-6
GN_EPS = 64e-5
ATTN_SCALE = HEAD_DIM ** -0.5

LANES = 128
META_ROWS = 128
CHUNK = 64
MASKED_KEY = -1e30

C_R, C_K, C_V = 0, 1024, 2048
C_FQ, C_FK, C_FV = 3072, 4096, 5120
C_GA, C_GB = 6144, 8192
C_S = 10240
S_COLS = 512
FL_LANE = 96
N_PACK = C_S + S_COLS

VMEM_LIMIT = 56 * 1024 * 1024


def _sigmoid(x):
    return 1.0 / (1.0 + jnp.exp(-x))


def _softplus(x):
    return jnp.maximum(x, 0.0) + jnp.log(1.0 + jnp.exp(-jnp.abs(x)))


def _dot(a, b):
    return jnp.dot(a, b, preferred_element_type=F32)


def _dot_nt(a, b):
    return lax.dot_general(a, b, (((1,), (1,)), ((), ())), preferred_element_type=F32)


def _dot_tn(a, b):
    return lax.dot_general(a, b, (((0,), (0,)), ((), ())), preferred_element_type=F32)


def _split3(x):
    hi = x.astype(BF16)
    r1 = x - hi.astype(F32)
    mid = r1.astype(BF16)
    lo = (r1 - mid.astype(F32)).astype(BF16)
    return hi, mid, lo


_NN = (((1,), (0,)), ((), ()))
_NT = (((1,), (1,)), ((), ()))
_TN = (((0,), (0,)), ((), ()))


def _head_ones():
    r = lax.broadcasted_iota(jnp.int32, (LANES, LANES), 0) // HEAD_DIM
    c = lax.broadcasted_iota(jnp.int32, (LANES, LANES), 1) // HEAD_DIM
    return jnp.where(r == c, 1.0, 0.0).astype(BF16)


def _head_sum(x, ones_bd):
    hi, mid, lo = _split3(x)
    return _dot(jnp.concatenate([hi, mid, lo], axis=1),
                jnp.concatenate([ones_bd, ones_bd, ones_bd], axis=0))


def _rms_mm_kernel(x_ref, g_ref, w_ref, o_ref, xn_ref):
    @pl.when(pl.program_id(1) == 0)
    def _():
        x = x_ref[...]
        ms = jnp.mean(x * x, axis=-1, keepdims=True)
        xn_ref[...] = ((x * lax.rsqrt(ms + RMS_EPS)) * g_ref[...]).astype(BF16)

    o_ref[...] = _dot(xn_ref[...], w_ref[...]).astype(o_ref.dtype)


def _rms_glu_kernel(x_ref, g_ref, wg_ref, wu_ref, o_ref, xn_ref):
    @pl.when(pl.program_id(1) == 0)
    def _():
        x = x_ref[...]
        ms = jnp.mean(x * x, axis=-1, keepdims=True)
        xn_ref[...] = ((x * lax.rsqrt(ms + RMS_EPS)) * g_ref[...]).astype(BF16)

    xn = xn_ref[...]
    gate = _dot(xn, wg_ref[...])
    up = _dot(xn, wu_ref[...])
    o_ref[...] = (gate * _sigmoid(gate) * up).astype(o_ref.dtype)


def _rms_matmul(x2d, g, w_bf16, *, tm, tn, out_dtype=F32):
    m, d = x2d.shape
    n = w_bf16.shape[1]
    return pl.pallas_call(
        _rms_mm_kernel,
        grid=(m // tm, n // tn),
        in_specs=[pl.BlockSpec((tm, d), lambda i, j: (i, 0)),
                  pl.BlockSpec((1, d), lambda i, j: (0, 0)),
                  pl.BlockSpec((d, tn), lambda i, j: (0, j))],
        out_specs=pl.BlockSpec((tm, tn), lambda i, j: (i, j)),
        out_shape=jax.ShapeDtypeStruct((m, n), out_dtype),
        scratch_shapes=[pltpu.VMEM((tm, d), BF16)],
        compiler_params=pltpu.CompilerParams(
            dimension_semantics=("parallel", "arbitrary"), vmem_limit_bytes=VMEM_LIMIT),
        name="rms_matmul",
    )(x2d, g.reshape(1, d), w_bf16)


def _rms_matmul_glu(x2d, g, w_bf16, *, tm, tn):
    m, d = x2d.shape
    n = w_bf16.shape[1] // 2
    up_off = n // tn
    return pl.pallas_call(
        _rms_glu_kernel,
        grid=(m // tm, n // tn),
        in_specs=[pl.BlockSpec((tm, d), lambda i, j: (i, 0)),
                  pl.BlockSpec((1, d), lambda i, j: (0, 0)),
                  pl.BlockSpec((d, tn), lambda i, j: (0, j)),
                  pl.BlockSpec((d, tn), lambda i, j: (0, j + up_off))],
        out_specs=pl.BlockSpec((tm, tn), lambda i, j: (i, j)),
        out_shape=jax.ShapeDtypeStruct((m, n), BF16),
        scratch_shapes=[pltpu.VMEM((tm, d), BF16)],
        compiler_params=pltpu.CompilerParams(
            dimension_semantics=("parallel", "arbitrary"), vmem_limit_bytes=VMEM_LIMIT),
        name="rms_matmul_glu",
    )(x2d, g.reshape(1, d), w_bf16, w_bf16)


def _mm_res_kernel(a_ref, b_ref, r_ref, o_ref):
    o_ref[...] = _dot(a_ref[...], b_ref[...]) + r_ref[...]


def _matmul_res(a, b, res, *, tm, tn):
    m, kk = a.shape
    n = b.shape[1]
    return pl.pallas_call(
        _mm_res_kernel,
        grid=(m // tm, n // tn),
        in_specs=[pl.BlockSpec((tm, kk), lambda i, j: (i, 0)),
                  pl.BlockSpec((kk, tn), lambda i, j: (0, j)),
                  pl.BlockSpec((tm, tn), lambda i, j: (i, j))],
        out_specs=pl.BlockSpec((tm, tn), lambda i, j: (i, j)),
        out_shape=jax.ShapeDtypeStruct((m, n), F32),
        compiler_params=pltpu.CompilerParams(
            dimension_semantics=("parallel", "arbitrary"), vmem_limit_bytes=VMEM_LIMIT),
        name="matmul_res",
    )(a, b, res)


def _merge_kernel(ya_ref, yb_ref, wa_ref, wb_ref, za_ref, zb_ref, o_ref):
    ta = _dot(ya_ref[...], wa_ref[...])
    tb = _dot(yb_ref[...], wb_ref[...])
    o_ref[...] = (_sigmoid(za_ref[...]) * ta + _sigmoid(zb_ref[...]) * tb).astype(o_ref.dtype)


def _merge(ya, yb, wa, wb, proj2d, *, tm, tn):
    m, kk = ya.shape
    n = wa.shape[1]
    oa, ob = C_GA // tn, C_GB // tn
    return pl.pallas_call(
        _merge_kernel,
        grid=(m // tm, n // tn),
        in_specs=[pl.BlockSpec((tm, kk), lambda i, j: (i, 0)),
                  pl.BlockSpec((tm, kk), lambda i, j: (i, 0)),
                  pl.BlockSpec((kk, tn), lambda i, j: (0, j)),
                  pl.BlockSpec((kk, tn), lambda i, j: (0, j)),
                  pl.BlockSpec((tm, tn), lambda i, j: (i, j + oa)),
                  pl.BlockSpec((tm, tn), lambda i, j: (i, j + ob))],
        out_specs=pl.BlockSpec((tm, tn), lambda i, j: (i, j)),
        out_shape=jax.ShapeDtypeStruct((m, n), BF16),
        compiler_params=pltpu.CompilerParams(
            dimension_semantics=("parallel", "arbitrary"), vmem_limit_bytes=VMEM_LIMIT),
        name="merge",
    )(ya, yb, wa, wb, proj2d, proj2d)


def _rwkv_prep_kernel(z_ref, zs_ref, pz_ref, ps_ref, p0z_ref, p0s_ref, mu_ref, mus_ref,
                      w0_ref, w2_ref, a0_ref, a2_ref, g2_ref, kk_ref, ka_ref,
                      r_out, lw_out, k_out, v_out, a_out, b_out, g_out, *, tt):
    first = pl.program_id(1) == 0
    z = z_ref[0]
    zs = zs_ref[0]
    pz = jnp.where(first, p0z_ref[7:8, :], pz_ref[0, 7:8, :])
    ps = jnp.where(first, p0s_ref[7:8, :], ps_ref[0, 7:8, :])
    row = lax.broadcasted_iota(jnp.int32, (tt, 1), 0)

    def lerp(cur, prev_last, mu):
        prev = jnp.where(row == 0, prev_last, pltpu.roll(cur, 1, axis=0))
        return cur + (prev - cur) * mu

    zm = lerp(z, pz, mu_ref[...])
    zsm = lerp(zs, ps, mus_ref[...])
    r = zm[:, C_R:C_R + WIDTH]
    k = zm[:, C_K:C_K + WIDTH]
    v = zm[:, C_V:C_V + WIDTH]
    wd = zsm[:, 0:LANES]
    ad = zsm[:, LANES:2 * LANES]
    gd = zsm[:, 2 * LANES:4 * LANES]

    w_log = -_softplus(-(w0_ref[...] + _dot(jnp.tanh(wd).astype(BF16), w2_ref[...]))) - 0.5
    lw_out[0] = -jnp.exp(w_log)
    a_sig = _sigmoid(a0_ref[...] + _dot(ad.astype(BF16), a2_ref[...]))
    g_out[0] = _dot(_sigmoid(gd).astype(BF16), g2_ref[...])

    ones_bd = _head_ones()
    kk = k * kk_ref[...]
    kk2 = kk * kk
    nrm = jnp.concatenate(
        [_head_sum(kk2[:, p * LANES:(p + 1) * LANES], ones_bd) for p in range(PAIRS)], axis=1)
    kk = kk / jnp.maximum(jnp.sqrt(nrm), 1e-12)
    r_out[0] = r
    k_out[0] = k * (1.0 + (a_sig - 1.0) * ka_ref[...])
    v_out[0] = v
    a_out[0] = -kk
    b_out[0] = kk * a_sig


def _rwkv_prep(proj3d, prev0_z, prev0_s, mu_z, mu_s, w0, w2p, a0, a2p, g2, k_k, k_a, *, tt):
    b, l, _ = proj3d.shape
    s_blk = C_S // S_COLS
    zw = 3 * WIDTH
    row = lambda a: a.reshape(1, -1)
    vec = lambda n: pl.BlockSpec((1, n), lambda bi, i: (0, 0))
    full = lambda r, c: pl.BlockSpec((r, c), lambda bi, i: (0, 0))
    out_spec = pl.BlockSpec((1, tt, WIDTH), lambda bi, i: (bi, i, 0))
    out_sds = jax.ShapeDtypeStruct((b, l, WIDTH), F32)
    return pl.pallas_call(
        functools.partial(_rwkv_prep_kernel, tt=tt),
        grid=(b, l // tt),
        in_specs=[pl.BlockSpec((1, tt, zw), lambda bi, i: (bi, i, 0)),
                  pl.BlockSpec((1, tt, S_COLS), lambda bi, i: (bi, i, s_blk)),
                  pl.BlockSpec((1, 8, zw), lambda bi, i: (bi, jnp.maximum(i * (tt // 8) - 1, 0), 0)),
                  pl.BlockSpec((1, 8, S_COLS),
                               lambda bi, i: (bi, jnp.maximum(i * (tt // 8) - 1, 0), s_blk)),
                  full(8, zw), full(8, S_COLS),
                  vec(zw), vec(S_COLS),
                  vec(WIDTH), full(LANES, WIDTH), vec(WIDTH), full(LANES, WIDTH),
                  full(GATE_LORA, WIDTH), vec(WIDTH), vec(WIDTH)],
        out_specs=[out_spec] * 7,
        out_shape=[out_sds] * 7,
        compiler_params=pltpu.CompilerParams(
            dimension_semantics=("parallel", "arbitrary"), vmem_limit_bytes=VMEM_LIMIT),
        name="rwkv_prep",
    )(proj3d, proj3d, proj3d, proj3d, prev0_z, prev0_s, row(mu_z), row(mu_s),
      row(w0), w2p, row(a0), a2p, g2, row(k_k), row(k_a))


def _bf(x):
    return x.astype(BF16)


WKV_GROUP = 8


def _wkv_group(a_t, r_t, b_t, k_t, b_h, k_h, v, s, w_total, masks):
    head0, strict, lower, eye = masks
    dg = functools.partial(lax.dot_general, preferred_element_type=F32)
    n = range(len(s))

    def bd(x):
        return jnp.concatenate([jnp.where(head0, x, 0.0), jnp.where(head0, 0.0, x)], axis=0)

    lhs = [_bf(jnp.concatenate([bd(a_t[i]), bd(r_t[i])], axis=0)) for i in n]
    rhs = [_bf(jnp.concatenate([bd(b_t[i]), bd(k_t[i])], axis=0)) for i in n]
    v_bd = [_bf(bd(v[i])) for i in n]
    gram = [dg(lhs[i], rhs[i], _NT) for i in n]
    a_s = [dg(lhs[i], _bf(s[i]), _NT) for i in n]
    n_ab = [jnp.where(strict, g[0:LANES, 0:LANES], 0.0) for g in gram]
    a_ak = [jnp.where(strict, g[0:LANES, LANES:2 * LANES], 0.0) for g in gram]
    a_r = [_bf(jnp.concatenate([jnp.where(lower, g[LANES:2 * LANES, 0:LANES], 0.0),
                                jnp.where(lower, g[LANES:2 * LANES, LANES:2 * LANES], 0.0)], axis=1))
           for g in gram]
    rhs_u = [a_s[i][0:LANES] + dg(_bf(a_ak[i]), v_bd[i], _NN) for i in n]

    p_inv = [eye + x for x in n_ab]
    q = [_bf(x) for x in n_ab]
    for _ in range(5):
        q = [_bf(dg(x, x, _NN)) for x in q]
        p_inv = [p_inv[i] + dg(q[i], _bf(p_inv[i]), _NN) for i in n]
    u = [dg(_bf(p_inv[i]), _bf(rhs_u[i]), _NN) for i in n]

    uv = [jnp.concatenate([_bf(u[i]), v_bd[i]], axis=0) for i in n]
    y_bd = [a_s[i][LANES:2 * LANES] + dg(a_r[i], uv[i], _NN) for i in n]
    y = [x[0:CHUNK] + x[CHUNK:2 * CHUNK] for x in y_bd]
    s_new = [w_total[i] * s[i]
             + dg(uv[i], _bf(jnp.concatenate([bd(b_h[i]), bd(k_h[i])], axis=0)), _TN) for i in n]
    return y, s_new


def _wkv_kernel(r_ref, lw_ref, k_ref, v_ref, a_ref, b_ref, g_ref, s0_ref,
                gnw_ref, gnb_ref, rk_ref, y_ref, sfin_ref, s_scr, *, nc):
    c = pl.program_id(1)

    @pl.when(c == 0)
    def _():
        s_scr[...] = s0_ref[0]

    r = r_ref[0]
    lw = lw_ref[0]
    k = k_ref[0]
    v = v_ref[0]
    a = a_ref[0]
    b = b_ref[0]

    ti = lax.broadcasted_iota(jnp.int32, (CHUNK, CHUNK), 0)
    tj = lax.broadcasted_iota(jnp.int32, (CHUNK, CHUNK), 1)
    ltri = jnp.where(tj <= ti, 1.0, 0.0).astype(BF16)
    hi, mid, lo = _split3(lw)
    cum = _dot(jnp.concatenate([ltri, ltri, ltri], axis=1),
               jnp.concatenate([hi, mid, lo], axis=0))
    total = cum[CHUNK - 1:CHUNK, :]
    e_inv = jnp.exp(-cum)
    e_end = jnp.exp(total - cum)
    a_t = a * jnp.exp(cum - lw)
    r_t = r * jnp.exp(cum)
    b_t = b * e_inv
    k_t = k * e_inv
    b_h = b * e_end
    k_h = k * e_end
    w_total = jnp.exp(total)

    lane = lax.broadcasted_iota(jnp.int32, (1, LANES), 1)
    bi = lax.broadcasted_iota(jnp.int32, (LANES, LANES), 0)
    bj = lax.broadcasted_iota(jnp.int32, (LANES, LANES), 1)
    same = (bi // CHUNK) == (bj // CHUNK)
    masks = (lane < HEAD_DIM,
             same & ((bj % CHUNK) < (bi % CHUNK)),
             same & ((bj % CHUNK) <= (bi % CHUNK)),
             jnp.where(bi == bj, 1.0, 0.0))

    ones_bd = _head_ones()
    inv_n = 1.0 / HEAD_DIM
    for p0 in range(0, PAIRS, WKV_GROUP):
        pairs = range(p0, p0 + WKV_GROUP)
        sls = [slice(p * LANES, (p + 1) * LANES) for p in pairs]
        cut = lambda x: [x[:, sl] for sl in sls]
        ys, s_news = _wkv_group(cut(a_t), cut(r_t), cut(b_t), cut(k_t), cut(b_h), cut(k_h), cut(v),
                                [s_scr[p] for p in pairs], cut(w_total), masks)
        for p, s_new in zip(pairs, s_news):
            s_scr[p] = s_new

            @pl.when(c == nc - 1)
            def _(p=p, s_new=s_new):
                sfin_ref[0, p] = s_new

        y_all = jnp.concatenate(ys, axis=0)
        rk_all = jnp.concatenate([r[:, sl] * k[:, sl] * rk_ref[:, sl] for sl in sls], axis=0)
        bonus_all = _head_sum(rk_all, ones_bd)
        d_all = y_all - _head_sum(y_all, ones_bd) * inv_n
        rstd_all = lax.rsqrt(_head_sum(d_all * d_all, ones_bd) * inv_n + GN_EPS)
        for i, sl in enumerate(sls):
            rows = slice(i * CHUNK, (i + 1) * CHUNK)
            yn = d_all[rows] * rstd_all[rows] * gnw_ref[:, sl] + gnb_ref[:, sl]
            y_ref[0, :, sl] = ((yn + bonus_all[rows] * v[:, sl]) * g_ref[0, :, sl]).astype(y_ref.dtype)


def _wkv(r, lw, k, v, a, b, g, s0, gn_w, gn_b, r_k):
    bsz, l, _ = r.shape
    nc = l // CHUNK
    tok = pl.BlockSpec((1, CHUNK, WIDTH), lambda bi, c: (bi, c, 0))
    vec = pl.BlockSpec((1, WIDTH), lambda bi, c: (0, 0))
    st_in = pl.BlockSpec((1, PAIRS, LANES, LANES), lambda bi, c: (0, 0, 0, 0))
    st_out = pl.BlockSpec((1, PAIRS, LANES, LANES), lambda bi, c: (bi, 0, 0, 0))
    row = lambda x: x.reshape(1, WIDTH)
    return pl.pallas_call(
        functools.partial(_wkv_kernel, nc=nc),
        grid=(bsz, nc),
        in_specs=[tok] * 7 + [st_in, vec, vec, vec],
        out_specs=[tok, st_out],
        out_shape=[jax.ShapeDtypeStruct((bsz, l, WIDTH), BF16),
                   jax.ShapeDtypeStruct((bsz, PAIRS, LANES, LANES), F32)],
        scratch_shapes=[pltpu.VMEM((PAIRS, LANES, LANES), F32)],
        compiler_params=pltpu.CompilerParams(
            dimension_semantics=("parallel", "arbitrary"), vmem_limit_bytes=VMEM_LIMIT),
        name="wkv",
    )(r, lw, k, v, a, b, g, s0, row(gn_w), row(gn_b), row(r_k))


def _fox_prep_kernel(z_ref, zs_ref, c0_ref, qg_ref, kg_ref, fb_ref,
                     qt_out, k_out, vt_out, c_out, carry, *, tt, n_pad):
    i = pl.program_id(1)

    @pl.when(i == 0)
    def _():
        carry[...] = c0_ref[...]

    z = z_ref[0]
    fl = zs_ref[0][:, 0:LANES]
    log_f = -_softplus(-(fl + fb_ref[...]))
    ti = lax.broadcasted_iota(jnp.int32, (tt, tt), 0)
    tj = lax.broadcasted_iota(jnp.int32, (tt, tt), 1)
    ltri = jnp.where(tj <= ti, 1.0, 0.0).astype(BF16)
    hi, mid, lo = _split3(log_f)
    cum = carry[...] + _dot(jnp.concatenate([ltri, ltri, ltri], axis=1),
                            jnp.concatenate([hi, mid, lo], axis=0))
    carry[...] = cum[tt - 1:tt, :]
    c_out[0] = cum[tt - 8:tt, :]
    c_hi, c_mid, c_lo = (t.astype(F32) for t in _split3(cum))
    if n_pad:
        grow = i * tt + lax.broadcasted_iota(jnp.int32, (tt, 1), 0)
        neg_hi = jnp.where(grow < n_pad, MASKED_KEY, -c_hi)
    else:
        neg_hi = -c_hi

    for p in range(PAIRS):
        vt_out[0, p, 0] = z[:, 2 * WIDTH + p * LANES:2 * WIDTH + (p + 1) * LANES].T.astype(BF16)
    ones_bd = _head_ones()
    lane = lax.broadcasted_iota(jnp.int32, (1, LANES), 1)
    inv_n = 1.0 / HEAD_DIM
    for p in range(PAIRS):
        zq = z[:, p * LANES:(p + 1) * LANES]
        zk = z[:, WIDTH + p * LANES:WIDTH + (p + 1) * LANES]
        qn = zq * lax.rsqrt(_head_sum(zq * zq, ones_bd) * inv_n + RMS_EPS) * (qg_ref[...] * ATTN_SCALE)
        kn = zk * lax.rsqrt(_head_sum(zk * zk, ones_bd) * inv_n + RMS_EPS) * kg_ref[...]
        for j in range(2):
            h = 2 * p + j
            own = (lane < HEAD_DIM) if j == 0 else (lane >= HEAD_DIM)
            e = lane - (HEAD_DIM if j == 0 else 0)
            col = lambda t: t[:, FL_LANE + h:FL_LANE + h + 1]
            q_extra = jnp.where(e == 0, col(c_hi), jnp.where(e == 1, col(c_mid), jnp.where(
                e == 2, col(c_lo), jnp.where((e >= 3) & (e < 6), 1.0, 0.0))))
            k_extra = jnp.where((e >= 0) & (e < 3), 1.0, jnp.where(e == 3, col(neg_hi), jnp.where(
                e == 4, -col(c_mid), jnp.where(e == 5, -col(c_lo), 0.0))))
            qt_out[0, h, 0] = jnp.where(own, qn, q_extra).T.astype(BF16)
            k_out[0, h] = jnp.where(own, kn, k_extra).astype(BF16)


def _fox_prep(proj3d, c0, q_g, k_g, f_bias, *, tt, n_pad):
    b, l, _ = proj3d.shape
    zw = 3 * WIDTH
    tile2 = lambda g: jnp.concatenate([g, g]).reshape(1, LANES)
    fb = jnp.concatenate([jnp.zeros((FL_LANE,), F32), f_bias,
                          jnp.zeros((LANES - FL_LANE - HEADS,), F32)]).reshape(1, LANES)
    hd = pl.BlockSpec((1, HEADS, tt, LANES), lambda bi, i: (bi, 0, i, 0))
    vec = pl.BlockSpec((1, LANES), lambda bi, i: (0, 0))
    return pl.pallas_call(
        functools.partial(_fox_prep_kernel, tt=tt, n_pad=n_pad),
        grid=(b, l // tt),
        in_specs=[pl.BlockSpec((1, tt, zw), lambda bi, i: (bi, i, C_FQ // zw)),
                  pl.BlockSpec((1, tt, S_COLS), lambda bi, i: (bi, i, C_S // S_COLS)),
                  vec, vec, vec, vec],
        out_specs=[pl.BlockSpec((1, HEADS, 1, LANES, tt), lambda bi, i: (bi, 0, i, 0, 0)),
                   hd,
                   pl.BlockSpec((1, PAIRS, 1, LANES, tt), lambda bi, i: (bi, 0, i, 0, 0)),
                   pl.BlockSpec((1, 8, LANES), lambda bi, i: (bi, i, 0))],
        out_shape=[jax.ShapeDtypeStruct((b, HEADS, l // tt, LANES, tt), BF16),
                   jax.ShapeDtypeStruct((b, HEADS, l, LANES), BF16),
                   jax.ShapeDtypeStruct((b, PAIRS, l // tt, LANES, tt), BF16),
                   jax.ShapeDtypeStruct((b, (l // tt) * 8, LANES), F32)],
        scratch_shapes=[pltpu.VMEM((1, LANES), F32)],
        compiler_params=pltpu.CompilerParams(
            dimension_semantics=("parallel", "arbitrary"), vmem_limit_bytes=VMEM_LIMIT),
        name="fox_prep",
    )(proj3d, proj3d, c0, tile2(q_g), tile2(k_g), fb)


def _attn_kernel(qt_ref, k_ref, vt_ref, km_ref, vmt_ref, o_ref, *, tq):
    qi = pl.program_id(2)
    key_i = lax.broadcasted_iota(jnp.int32, (tq, tq), 0)
    qry_i = lax.broadcasted_iota(jnp.int32, (tq, tq), 1)
    causal = key_i <= qry_i
    qts = (qt_ref[0, 0, 0], qt_ref[0, 1, 0])
    heads = range(2)

    def scores(kblks):
        return tuple(_dot(kblks[j], qts[j]) for j in heads)

    def update(carries, s, vtblk, mask):
        if mask is not None:
            s = [jnp.where(mask, x, MASKED_KEY) for x in s]
        m_new = [jnp.maximum(carries[j][0], jnp.max(s[j], axis=0, keepdims=True)) for j in heads]
        alpha = [jnp.exp(carries[j][0] - m_new[j]) for j in heads]
        p = [jnp.exp(s[j] - m_new[j]) for j in heads]
        l = [alpha[j] * carries[j][1] + jnp.sum(p[j], axis=0, keepdims=True) for j in heads]
        pv = [_dot(vtblk[j * HEAD_DIM:(j + 1) * HEAD_DIM, :], p[j].astype(BF16)) for j in heads]
        return tuple((m_new[j], l[j], alpha[j] * carries[j][2] + pv[j]) for j in heads)

    def kblocks(kb):
        off = pl.multiple_of(kb * tq, tq)
        return (k_ref[0, 0, pl.ds(off, tq), :], k_ref[0, 1, pl.ds(off, tq), :])

    init = (jnp.full((1, tq), MASKED_KEY, F32), jnp.zeros((1, tq), F32),
            jnp.zeros((HEAD_DIM, tq), F32))
    s_meta = scores((km_ref[0, 0], km_ref[0, 1]))
    s_first = scores(kblocks(0))
    carries = update((init, init), s_meta, vmt_ref[0, 0, 0], None)

    def body(kb, state):
        carries, s_cur = state
        s_next = scores(kblocks(kb + 1))
        return update(carries, s_cur, vt_ref[0, 0, kb], None), s_next

    carries, s_last = lax.fori_loop(0, qi, body, (carries, s_first))
    (_, l0, acc0), (_, l1, acc1) = update(carries, s_last, vt_ref[0, 0, qi], causal)
    out_t = jnp.concatenate([acc0 / l0, acc1 / l1], axis=0)
    o_ref[0] = out_t.T.astype(o_ref.dtype)


def _attn(qt, kp, vt, kp_meta, vt_meta, *, tq):
    b, _, l, _ = kp.shape
    lm = kp_meta.shape[2]
    nk = vt.shape[2]
    assert vt.shape[4] == tq and qt.shape[4] == tq
    return pl.pallas_call(
        functools.partial(_attn_kernel, tq=tq),
        grid=(b, PAIRS, l // tq),
        in_specs=[pl.BlockSpec((1, 2, 1, LANES, tq), lambda bi, p, qi: (bi, p, qi, 0, 0)),
                  pl.BlockSpec((1, 2, l, LANES), lambda bi, p, qi: (bi, p, 0, 0)),
                  pl.BlockSpec((1, 1, nk, LANES, tq), lambda bi, p, qi: (bi, p, 0, 0, 0)),
                  pl.BlockSpec((1, 2, lm, LANES), lambda bi, p, qi: (0, p, 0, 0)),
                  pl.BlockSpec((1, 1, 1, LANES, lm), lambda bi, p, qi: (0, p, 0, 0, 0))],
        out_specs=pl.BlockSpec((1, tq, LANES), lambda bi, p, qi: (bi, qi, p)),
        out_shape=jax.ShapeDtypeStruct((b, l, WIDTH), BF16),
        compiler_params=pltpu.CompilerParams(
            dimension_semantics=("parallel", "parallel", "arbitrary"), vmem_limit_bytes=VMEM_LIMIT),
        name="attn",
    )(qt, kp, vt, kp_meta, vt_meta)


def _pick(n, prefs):
    for t in prefs:
        if n % t == 0:
            return t
    raise ValueError(f"no tile for {n}")


def _pack_w_in(w_in):
    rw, fx = 3 * WIDTH + DECAY_LORA + AAA_LORA + GATE_LORA, 3 * WIDTH + HEADS
    wd0, ad0, gd0 = 3 * WIDTH, 3 * WIDTH + DECAY_LORA, 3 * WIDTH + DECAY_LORA + AAA_LORA
    z = lambda n: jnp.zeros((w_in.shape[0], n), w_in.dtype)
    return jnp.concatenate([
        w_in[:, 0:3 * WIDTH], w_in[:, rw:rw + 3 * WIDTH], w_in[:, rw + fx:],
        w_in[:, wd0:ad0], w_in[:, rw + 3 * WIDTH:rw + fx], z(LANES - DECAY_LORA - HEADS),
        w_in[:, ad0:gd0], z(LANES - AAA_LORA), w_in[:, gd0:rw]], axis=1).astype(BF16)


def _pack_mu(mu):
    wd0, ad0, gd0 = 3 * WIDTH, 3 * WIDTH + DECAY_LORA, 3 * WIDTH + DECAY_LORA + AAA_LORA
    z = lambda n: jnp.zeros((n,), mu.dtype)
    mu_s = jnp.concatenate([mu[wd0:ad0], z(LANES - DECAY_LORA), mu[ad0:gd0], z(LANES - AAA_LORA),
                            mu[gd0:]])
    return mu[:3 * WIDTH], mu_s


def _pad_rows(w, rows):
    return jnp.concatenate([w, jnp.zeros((rows - w.shape[0], w.shape[1]), w.dtype)], axis=0)


def _mixer_inputs(h2d, bsz, norm_g, w_pack, tm):
    proj = _rms_matmul(h2d, norm_g, w_pack, tm=tm, tn=512)
    return proj, proj.reshape(bsz, h2d.shape[0] // bsz, N_PACK)


def kernel(x, meta_tokens, norm1_g, w_in, rwkv_mu, rwkv_w0, rwkv_w2, rwkv_a0, rwkv_a2, rwkv_g2,
           rwkv_k_k, rwkv_k_a, rwkv_r_k, rwkv_gn_w, rwkv_gn_b, fox_q_norm_g, fox_k_norm_g,
           fox_f_bias, w_branch_a, w_branch_b, w_o, norm2_g, w_gate_up, w_down):
    bsz, seq, d = x.shape
    assert d == D_MODEL and norm1_g.shape[0] == 1 and seq % 256 == 0
    m = bsz * seq
    x2d = x.reshape(m, d)
    meta2d = jnp.concatenate(
        [jnp.zeros((META_ROWS - N_META, d), x.dtype), meta_tokens.astype(x.dtype)], axis=0)

    w_pack = _pack_w_in(w_in[0])
    mu_z, mu_s = _pack_mu(rwkv_mu[0])
    w2p = _pad_rows(rwkv_w2[0], LANES).astype(BF16)
    a2p = _pad_rows(rwkv_a2[0], LANES).astype(BF16)
    g2 = rwkv_g2[0].astype(BF16)
    r_k = rwkv_r_k[0].reshape(WIDTH)

    tm = _pick(m, (1024, 512, 256))
    tt = _pick(seq, (256,))

    _, pm3 = _mixer_inputs(meta2d, 1, norm1_g[0], w_pack, META_ROWS)
    zero_z = jnp.zeros((8, 3 * WIDTH), F32)
    zero_s = jnp.zeros((8, S_COLS), F32)
    prep = functools.partial(_rwkv_prep, mu_z=mu_z, mu_s=mu_s, w0=rwkv_w0[0], w2p=w2p,
                             a0=rwkv_a0[0], a2p=a2p, g2=g2, k_k=rwkv_k_k[0], k_a=rwkv_k_a[0])
    wkv = functools.partial(_wkv, gn_w=rwkv_gn_w[0], gn_b=rwkv_gn_b[0], r_k=r_k)
    fox = functools.partial(_fox_prep, q_g=fox_q_norm_g[0], k_g=fox_k_norm_g[0],
                            f_bias=fox_f_bias[0])
    mr = prep(pm3, zero_z, zero_s, tt=META_ROWS)
    _, s_meta = wkv(*mr, jnp.zeros((1, PAIRS, LANES, LANES), F32))
    _, kp_meta, vt_meta, c_meta = fox(pm3, jnp.zeros((1, LANES), F32), tt=META_ROWS,
                                      n_pad=META_ROWS - N_META)

    proj2d, p3 = _mixer_inputs(x2d, bsz, norm1_g[0], w_pack, tm)
    rr = prep(p3, pm3[0, META_ROWS - 8:, 0:3 * WIDTH], pm3[0, META_ROWS - 8:, C_S:], tt=tt)
    ya, _ = wkv(*rr, s_meta)
    qt, kp, vt, _ = fox(p3, c_meta[0, 7:8, :], tt=tt, n_pad=0)
    yb = _attn(qt, kp, vt, kp_meta, vt_meta, tq=tt)

    merged = _merge(ya.reshape(m, WIDTH), yb.reshape(m, WIDTH), w_branch_a[0].astype(BF16),
                    w_branch_b[0].astype(BF16), proj2d, tm=tm, tn=512)
    h1 = _matmul_res(merged, w_o[0].astype(BF16), x2d, tm=tm, tn=512)
    act = _rms_matmul_glu(h1, norm2_g[0], w_gate_up[0].astype(BF16), tm=tm, tn=512)
    out = _matmul_res(act, w_down[0].astype(BF16), h1, tm=tm, tn=512)
    return out.reshape(bsz, seq, d)
```

```python
import functools

import jax
import jax.numpy as jnp
from jax import lax
from jax.experimental import pallas as pl
from jax.experimental.pallas import tpu as pltpu

F32 = jnp.float32
BF16 = jnp.bfloat16

D_MODEL = 2048
N_META = 16
HEAD_DIM = 64
HEADS = 16
WIDTH = HEADS * HEAD_DIM
PAIRS = HEADS // 2
DECAY_LORA = 96
AAA_LORA = 96
GATE_LORA = 256
D_FF = 5632
RMS_EPS = 1e-6
GN_EPS = 64e-5
ATTN_SCALE = HEAD_DIM ** -0.5

LANES = 128
META_ROWS = 256
CHUNK = 64
MASKED_KEY = -1e30

C_R, C_K, C_V = 0, 1024, 2048
C_FQ, C_FK, C_FV = 3072, 4096, 5120
C_GA, C_GB = 6144, 8192
C_S = 10240
S_COLS = 512
FL_LANE = 96
N_PACK = C_S + S_COLS

VMEM_LIMIT = 56 * 1024 * 1024


def _sigmoid(x):
    return 0.5 * jnp.tanh(0.5 * x) + 0.5


def _softplus(x):
    return jnp.maximum(x, 0.0) + jnp.log(1.0 + jnp.exp(-jnp.abs(x)))


def _dot(a, b):
    return jnp.dot(a, b, preferred_element_type=F32)


def _dot_nt(a, b):
    return lax.dot_general(a, b, (((1,), (1,)), ((), ())), preferred_element_type=F32)


def _dot_tn(a, b):
    return lax.dot_general(a, b, (((0,), (0,)), ((), ())), preferred_element_type=F32)


def _split3(x):
    hi = x.astype(BF16)
    r1 = x - hi.astype(F32)
    mid = r1.astype(BF16)
    lo = (r1 - mid.astype(F32)).astype(BF16)
    return hi, mid, lo


_NN = (((1,), (0,)), ((), ()))
_NT = (((1,), (1,)), ((), ()))
_TN = (((0,), (0,)), ((), ()))


def _head_ones():
    r = lax.broadcasted_iota(jnp.int32, (LANES, LANES), 0) // HEAD_DIM
    c = lax.broadcasted_iota(jnp.int32, (LANES, LANES), 1) // HEAD_DIM
    return jnp.where(r == c, 1.0, 0.0).astype(BF16)


def _head_sum(x, ones_bd):
    hi = x.astype(BF16)
    lo = (x - hi.astype(F32)).astype(BF16)
    return _dot(jnp.concatenate([hi, lo], axis=1), jnp.concatenate([ones_bd, ones_bd], axis=0))


def _rms_mm_kernel(x_ref, g_ref, w_ref, o_ref, xn_ref):
    @pl.when(pl.program_id(1) == 0)
    def _():
        x = x_ref[...]
        ms = jnp.mean(x * x, axis=-1, keepdims=True)
        xn_ref[...] = ((x * lax.rsqrt(ms + RMS_EPS)) * g_ref[...]).astype(BF16)

    o_ref[...] = _dot(xn_ref[...], w_ref[...]).astype(o_ref.dtype)


def _rms_glu_kernel(x_ref, g_ref, wg_ref, wu_ref, o_ref, xn_ref):
    @pl.when(pl.program_id(1) == 0)
    def _():
        x = x_ref[...]
        ms = jnp.mean(x * x, axis=-1, keepdims=True)
        xn_ref[...] = ((x * lax.rsqrt(ms + RMS_EPS)) * g_ref[...]).astype(BF16)

    xn = xn_ref[...]
    gate = _dot(xn, wg_ref[...])
    up = _dot(xn, wu_ref[...])
    o_ref[...] = (gate * _sigmoid(gate) * up).astype(o_ref.dtype)


def _rms_matmul(x2d, g, w_bf16, *, tm, tn, out_dtype=F32):
    m, d = x2d.shape
    n = w_bf16.shape[1]
    return pl.pallas_call(
        _rms_mm_kernel,
        grid=(m // tm, n // tn),
        in_specs=[pl.BlockSpec((tm, d), lambda i, j: (i, 0)),
                  pl.BlockSpec((1, d), lambda i, j: (0, 0)),
                  pl.BlockSpec((d, tn), lambda i, j: (0, j))],
        out_specs=pl.BlockSpec((tm, tn), lambda i, j: (i, j)),
        out_shape=jax.ShapeDtypeStruct((m, n), out_dtype),
        scratch_shapes=[pltpu.VMEM((tm, d), BF16)],
        compiler_params=pltpu.CompilerParams(
            dimension_semantics=("parallel", "arbitrary"), vmem_limit_bytes=VMEM_LIMIT),
        name="rms_matmul",
    )(x2d, g.reshape(1, d), w_bf16)


def _rms_matmul_glu(x2d, g, w_bf16, *, tm, tn):
    m, d = x2d.shape
    n = w_bf16.shape[1] // 2
    up_off = n // tn
    return pl.pallas_call(
        _rms_glu_kernel,
        grid=(m // tm, n // tn),
        in_specs=[pl.BlockSpec((tm, d), lambda i, j: (i, 0)),
                  pl.BlockSpec((1, d), lambda i, j: (0, 0)),
                  pl.BlockSpec((d, tn), lambda i, j: (0, j)),
                  pl.BlockSpec((d, tn), lambda i, j: (0, j + up_off))],
        out_specs=pl.BlockSpec((tm, tn), lambda i, j: (i, j)),
        out_shape=jax.ShapeDtypeStruct((m, n), BF16),
        scratch_shapes=[pltpu.VMEM((tm, d), BF16)],
        compiler_params=pltpu.CompilerParams(
            dimension_semantics=("parallel", "arbitrary"), vmem_limit_bytes=VMEM_LIMIT),
        name="rms_matmul_glu",
    )(x2d, g.reshape(1, d), w_bf16, w_bf16)


def _mm_res_kernel(a_ref, b_ref, r_ref, o_ref):
    o_ref[...] = _dot(a_ref[...], b_ref[...]) + r_ref[...]


def _matmul_res(a, b, res, *, tm, tn):
    m, kk = a.shape
    n = b.shape[1]
    return pl.pallas_call(
        _mm_res_kernel,
        grid=(m // tm, n // tn),
        in_specs=[pl.BlockSpec((tm, kk), lambda i, j: (i, 0)),
                  pl.BlockSpec((kk, tn), lambda i, j: (0, j)),
                  pl.BlockSpec((tm, tn), lambda i, j: (i, j))],
        out_specs=pl.BlockSpec((tm, tn), lambda i, j: (i, j)),
        out_shape=jax.ShapeDtypeStruct((m, n), F32),
        compiler_params=pltpu.CompilerParams(
            dimension_semantics=("parallel", "arbitrary"), vmem_limit_bytes=VMEM_LIMIT),
        name="matmul_res",
    )(a, b, res)


def _merge_kernel(ya_ref, yb_ref, wa_ref, wb_ref, za_ref, zb_ref, o_ref):
    ta = _dot(ya_ref[...], wa_ref[...])
    tb = _dot(yb_ref[...], wb_ref[...])
    o_ref[...] = (_sigmoid(za_ref[...]) * ta + _sigmoid(zb_ref[...]) * tb).astype(o_ref.dtype)


def _merge(ya, yb, wa, wb, proj2d, *, tm, tn):
    m, kk = ya.shape
    n = wa.shape[1]
    oa, ob = C_GA // tn, C_GB // tn
    return pl.pallas_call(
        _merge_kernel,
        grid=(m // tm, n // tn),
        in_specs=[pl.BlockSpec((tm, kk), lambda i, j: (i, 0)),
                  pl.BlockSpec((tm, kk), lambda i, j: (i, 0)),
                  pl.BlockSpec((kk, tn), lambda i, j: (0, j)),
                  pl.BlockSpec((kk, tn), lambda i, j: (0, j)),
                  pl.BlockSpec((tm, tn), lambda i, j: (i, j + oa)),
                  pl.BlockSpec((tm, tn), lambda i, j: (i, j + ob))],
        out_specs=pl.BlockSpec((tm, tn), lambda i, j: (i, j)),
        out_shape=jax.ShapeDtypeStruct((m, n), BF16),
        compiler_params=pltpu.CompilerParams(
            dimension_semantics=("parallel", "arbitrary"), vmem_limit_bytes=VMEM_LIMIT),
        name="merge",
    )(ya, yb, wa, wb, proj2d, proj2d)


def _rwkv_prep_kernel(z_ref, zs_ref, pz_ref, ps_ref, p0z_ref, p0s_ref, mu_ref, mus_ref,
                      w0_ref, w2_ref, a0_ref, a2_ref, g2_ref, kk_ref, ka_ref,
                      r_out, lw_out, k_out, v_out, a_out, b_out, g_out, *, tt):
    first = pl.program_id(1) == 0
    z = z_ref[0]
    zs = zs_ref[0]
    pz = jnp.where(first, p0z_ref[7:8, :], pz_ref[0, 7:8, :])
    ps = jnp.where(first, p0s_ref[7:8, :], ps_ref[0, 7:8, :])
    row = lax.broadcasted_iota(jnp.int32, (tt, 1), 0)

    def lerp(cur, prev_last, mu):
        prev = jnp.where(row == 0, prev_last, pltpu.roll(cur, 1, axis=0))
        return cur + (prev - cur) * mu

    zm = lerp(z, pz, mu_ref[...])
    zsm = lerp(zs, ps, mus_ref[...])
    r = zm[:, C_R:C_R + WIDTH]
    k = zm[:, C_K:C_K + WIDTH]
    v = zm[:, C_V:C_V + WIDTH]
    wd = zsm[:, 0:LANES]
    ad = zsm[:, LANES:2 * LANES]
    gd = zsm[:, 2 * LANES:4 * LANES]

    w_log = -_softplus(-(w0_ref[...] + _dot(jnp.tanh(wd).astype(BF16), w2_ref[...]))) - 0.5
    lw_out[0] = -jnp.exp(w_log)
    a_sig = _sigmoid(a0_ref[...] + _dot(ad.astype(BF16), a2_ref[...]))
    g_out[0] = _dot(_sigmoid(gd).astype(BF16), g2_ref[...]).astype(g_out.dtype)

    ones_bd = _head_ones()
    kk = k * kk_ref[...]
    kk2 = kk * kk
    nrm = jnp.concatenate(
        [_head_sum(kk2[:, p * LANES:(p + 1) * LANES], ones_bd) for p in range(PAIRS)], axis=1)
    kk = kk * lax.rsqrt(jnp.maximum(nrm, 1e-24))
    r_out[0] = r.astype(r_out.dtype)
    k_out[0] = (k * (1.0 + (a_sig - 1.0) * ka_ref[...])).astype(k_out.dtype)
    v_out[0] = v.astype(v_out.dtype)
    a_out[0] = (-kk).astype(a_out.dtype)
    b_out[0] = (kk * a_sig).astype(b_out.dtype)


def _rwkv_prep(proj3d, prev0_z, prev0_s, mu_z, mu_s, w0, w2p, a0, a2p, g2, k_k, k_a, *, tt):
    b, l, _ = proj3d.shape
    s_blk = C_S // S_COLS
    zw = 3 * WIDTH
    row = lambda a: a.reshape(1, -1)
    vec = lambda n: pl.BlockSpec((1, n), lambda bi, i: (0, 0))
    full = lambda r, c: pl.BlockSpec((r, c), lambda bi, i: (0, 0))
    out_spec = pl.BlockSpec((1, tt, WIDTH), lambda bi, i: (bi, i, 0))
    sds = lambda dt: jax.ShapeDtypeStruct((b, l, WIDTH), dt)
    return pl.pallas_call(
        functools.partial(_rwkv_prep_kernel, tt=tt),
        grid=(b, l // tt),
        in_specs=[pl.BlockSpec((1, tt, zw), lambda bi, i: (bi, i, 0)),
                  pl.BlockSpec((1, tt, S_COLS), lambda bi, i: (bi, i, s_blk)),
                  pl.BlockSpec((1, 8, zw), lambda bi, i: (bi, jnp.maximum(i * (tt // 8) - 1, 0), 0)),
                  pl.BlockSpec((1, 8, S_COLS),
                               lambda bi, i: (bi, jnp.maximum(i * (tt // 8) - 1, 0), s_blk)),
                  full(8, zw), full(8, S_COLS),
                  vec(zw), vec(S_COLS),
                  vec(WIDTH), full(LANES, WIDTH), vec(WIDTH), full(LANES, WIDTH),
                  full(GATE_LORA, WIDTH), vec(WIDTH), vec(WIDTH)],
        out_specs=[out_spec] * 7,
        out_shape=[sds(BF16), sds(F32)] + [sds(BF16)] * 5,
        compiler_params=pltpu.CompilerParams(
            dimension_semantics=("parallel", "arbitrary"), vmem_limit_bytes=VMEM_LIMIT),
        name="rwkv_prep",
    )(proj3d, proj3d, proj3d, proj3d, prev0_z, prev0_s, row(mu_z), row(mu_s),
      row(w0), w2p, row(a0), a2p, g2, row(k_k), row(k_a))


def _bf(x):
    return x.astype(BF16)


WKV_GROUP = 8


def _wkv_group(a_t, r_t, b_t, k_t, b_h, k_h, v, s, w_total, masks):
    head0, strict, lower, eye = masks
    dg = functools.partial(lax.dot_general, preferred_element_type=F32)
    n = range(len(s))

    def bd(x):
        return jnp.concatenate([jnp.where(head0, x, 0.0), jnp.where(head0, 0.0, x)], axis=0)

    lhs = [_bf(jnp.concatenate([bd(a_t[i]), bd(r_t[i])], axis=0)) for i in n]
    rhs = [_bf(jnp.concatenate([bd(b_t[i]), bd(k_t[i])], axis=0)) for i in n]
    v_bd = [_bf(bd(v[i])) for i in n]
    gram = [dg(lhs[i], rhs[i], _NT) for i in n]
    a_s = [dg(lhs[i], _bf(s[i]), _NT) for i in n]
    n_ab = [jnp.where(strict, g[0:LANES, 0:LANES], 0.0) for g in gram]
    a_ak = [jnp.where(strict, g[0:LANES, LANES:2 * LANES], 0.0) for g in gram]
    a_r = [_bf(jnp.concatenate([jnp.where(lower, g[LANES:2 * LANES, 0:LANES], 0.0),
                                jnp.where(lower, g[LANES:2 * LANES, LANES:2 * LANES], 0.0)], axis=1))
           for g in gram]
    rhs_u = [a_s[i][0:LANES] + dg(_bf(a_ak[i]), v_bd[i], _NN) for i in n]

    p_inv = [eye + x for x in n_ab]
    q = [_bf(x) for x in n_ab]
    for _ in range(5):
        q = [_bf(dg(x, x, _NN)) for x in q]
        p_inv = [p_inv[i] + dg(q[i], _bf(p_inv[i]), _NN) for i in n]
    u = [dg(_bf(p_inv[i]), _bf(rhs_u[i]), _NN) for i in n]

    uv = [jnp.concatenate([_bf(u[i]), v_bd[i]], axis=0) for i in n]
    y_bd = [a_s[i][LANES:2 * LANES] + dg(a_r[i], uv[i], _NN) for i in n]
    y = [x[0:CHUNK] + x[CHUNK:2 * CHUNK] for x in y_bd]
    s_new = [w_total[i] * s[i]
             + dg(uv[i], _bf(jnp.concatenate([bd(b_h[i]), bd(k_h[i])], axis=0)), _TN) for i in n]
    return y, s_new


def _wkv_kernel(r_ref, lw_ref, k_ref, v_ref, a_ref, b_ref, g_ref, s0_ref,
                gnw_ref, gnb_ref, rk_ref, y_ref, sfin_ref, s_scr, *, nc):
    c = pl.program_id(1)

    @pl.when(c == 0)
    def _():
        s_scr[...] = s0_ref[0]

    r = r_ref[0].astype(F32)
    lw = lw_ref[0]
    k = k_ref[0].astype(F32)
    v = v_ref[0].astype(F32)
    a = a_ref[0].astype(F32)
    b = b_ref[0].astype(F32)

    ti = lax.broadcasted_iota(jnp.int32, (CHUNK, CHUNK), 0)
    tj = lax.broadcasted_iota(jnp.int32, (CHUNK, CHUNK), 1)
    ltri = jnp.where(tj <= ti, 1.0, 0.0).astype(BF16)
    hi, mid, lo = _split3(lw)
    cum = _dot(jnp.concatenate([ltri, ltri, ltri], axis=1),
               jnp.concatenate([hi, mid, lo], axis=0))
    total = cum[CHUNK - 1:CHUNK, :]
    e_inv = jnp.exp(-cum)
    e_end = jnp.exp(total - cum)
    a_t = a * jnp.exp(cum - lw)
    r_t = r * jnp.exp(cum)
    b_t = b * e_inv
    k_t = k * e_inv
    b_h = b * e_end
    k_h = k * e_end
    w_total = jnp.exp(total)

    lane = lax.broadcasted_iota(jnp.int32, (1, LANES), 1)
    bi = lax.broadcasted_iota(jnp.int32, (LANES, LANES), 0)
    bj = lax.broadcasted_iota(jnp.int32, (LANES, LANES), 1)
    same = (bi // CHUNK) == (bj // CHUNK)
    masks = (lane < HEAD_DIM,
             same & ((bj % CHUNK) < (bi % CHUNK)),
             same & ((bj % CHUNK) <= (bi % CHUNK)),
             jnp.where(bi == bj, 1.0, 0.0))

    ones_bd = _head_ones()
    inv_n = 1.0 / HEAD_DIM
    for p0 in range(0, PAIRS, WKV_GROUP):
        pairs = range(p0, p0 + WKV_GROUP)
        sls = [slice(p * LANES, (p + 1) * LANES) for p in pairs]
        cut = lambda x: [x[:, sl] for sl in sls]
        ys, s_news = _wkv_group(cut(a_t), cut(r_t), cut(b_t), cut(k_t), cut(b_h), cut(k_h), cut(v),
                                [s_scr[p] for p in pairs], cut(w_total), masks)
        for p, s_new in zip(pairs, s_news):
            s_scr[p] = s_new

            @pl.when(c == nc - 1)
            def _(p=p, s_new=s_new):
                sfin_ref[0, p] = s_new

        y_all = jnp.concatenate(ys, axis=0)
        rk_all = jnp.concatenate([r[:, sl] * k[:, sl] * rk_ref[:, sl] for sl in sls], axis=0)
        bonus_all = _head_sum(rk_all, ones_bd)
        d_all = y_all - _head_sum(y_all, ones_bd) * inv_n
        rstd_all = lax.rsqrt(_head_sum(d_all * d_all, ones_bd) * inv_n + GN_EPS)
        for i, sl in enumerate(sls):
            rows = slice(i * CHUNK, (i + 1) * CHUNK)
            yn = d_all[rows] * rstd_all[rows] * gnw_ref[:, sl] + gnb_ref[:, sl]
            y_ref[0, :, sl] = ((yn + bonus_all[rows] * v[:, sl])
                               * g_ref[0, :, sl].astype(F32)).astype(y_ref.dtype)


def _wkv(r, lw, k, v, a, b, g, s0, gn_w, gn_b, r_k):
    bsz, l, _ = r.shape
    nc = l // CHUNK
    tok = pl.BlockSpec((1, CHUNK, WIDTH), lambda bi, c: (bi, c, 0))
    vec = pl.BlockSpec((1, WIDTH), lambda bi, c: (0, 0))
    st_in = pl.BlockSpec((1, PAIRS, LANES, LANES), lambda bi, c: (0, 0, 0, 0))
    st_out = pl.BlockSpec((1, PAIRS, LANES, LANES), lambda bi, c: (bi, 0, 0, 0))
    row = lambda x: x.reshape(1, WIDTH)
    return pl.pallas_call(
        functools.partial(_wkv_kernel, nc=nc),
        grid=(bsz, nc),
        in_specs=[tok] * 7 + [st_in, vec, vec, vec],
        out_specs=[tok, st_out],
        out_shape=[jax.ShapeDtypeStruct((bsz, l, WIDTH), BF16),
                   jax.ShapeDtypeStruct((bsz, PAIRS, LANES, LANES), F32)],
        scratch_shapes=[pltpu.VMEM((PAIRS, LANES, LANES), F32)],
        compiler_params=pltpu.CompilerParams(
            dimension_semantics=("parallel", "arbitrary"), vmem_limit_bytes=VMEM_LIMIT),
        name="wkv",
    )(r, lw, k, v, a, b, g, s0, row(gn_w), row(gn_b), row(r_k))


def _fox_prep_kernel(z_ref, zs_ref, c0_ref, qg_ref, kg_ref, fb_ref,
                     qt_out, k_out, vt_out, c_out, carry, *, tt, n_pad):
    i = pl.program_id(1)

    @pl.when(i == 0)
    def _():
        carry[...] = c0_ref[...]

    z = z_ref[0]
    fl = zs_ref[0][:, 0:LANES]
    log_f = -_softplus(-(fl + fb_ref[...]))
    ti = lax.broadcasted_iota(jnp.int32, (tt, tt), 0)
    tj = lax.broadcasted_iota(jnp.int32, (tt, tt), 1)
    ltri = jnp.where(tj <= ti, 1.0, 0.0).astype(BF16)
    hi, mid, lo = _split3(log_f)
    cum = carry[...] + _dot(jnp.concatenate([ltri, ltri, ltri], axis=1),
                            jnp.concatenate([hi, mid, lo], axis=0))
    carry[...] = cum[tt - 1:tt, :]
    c_out[0] = cum[tt - 8:tt, :]
    c_hi, c_mid, c_lo = (t.astype(F32) for t in _split3(cum))
    neg_hi, neg_mid, neg_lo = -c_hi, -c_mid, -c_lo
    if n_pad:
        grow = i * tt + lax.broadcasted_iota(jnp.int32, (tt, 1), 0)
        neg_hi = jnp.where(grow < n_pad, MASKED_KEY, neg_hi)

    for p in range(PAIRS):
        vt_out[0, p, 0] = z[:, 2 * WIDTH + p * LANES:2 * WIDTH + (p + 1) * LANES].T.astype(BF16)
    ones_bd = _head_ones()
    lane = lax.broadcasted_iota(jnp.int32, (1, LANES), 1)
    inv_n = 1.0 / HEAD_DIM
    for p in range(PAIRS):
        zq = z[:, p * LANES:(p + 1) * LANES]
        zk = z[:, WIDTH + p * LANES:WIDTH + (p + 1) * LANES]
        qn = zq * lax.rsqrt(_head_sum(zq * zq, ones_bd) * inv_n + RMS_EPS) * (qg_ref[...] * ATTN_SCALE)
        kn = zk * lax.rsqrt(_head_sum(zk * zk, ones_bd) * inv_n + RMS_EPS) * kg_ref[...]
        for j in range(2):
            h = 2 * p + j
            own = (lane < HEAD_DIM) if j == 0 else (lane >= HEAD_DIM)
            e = lane - (HEAD_DIM if j == 0 else 0)
            col = lambda t: t[:, FL_LANE + h:FL_LANE + h + 1]
            q_extra = jnp.where((e >= 0) & (e < 3), 1.0, 0.0)
            k_extra = jnp.where(e == 0, col(neg_hi), jnp.where(e == 1, col(neg_mid), jnp.where(
                e == 2, col(neg_lo), 0.0)))
            qt_out[0, h, 0] = jnp.where(own, qn, q_extra).T.astype(BF16)
            k_out[0, h] = jnp.where(own, kn, k_extra).astype(BF16)


def _fox_prep(proj3d, c0, q_g, k_g, f_bias, *, tt, n_pad):
    b, l, _ = proj3d.shape
    zw = 3 * WIDTH
    tile2 = lambda g: jnp.concatenate([g, g]).reshape(1, LANES)
    fb = jnp.concatenate([jnp.zeros((FL_LANE,), F32), f_bias,
                          jnp.zeros((LANES - FL_LANE - HEADS,), F32)]).reshape(1, LANES)
    hd = pl.BlockSpec((1, HEADS, tt, LANES), lambda bi, i: (bi, 0, i, 0))
    vec = pl.BlockSpec((1, LANES), lambda bi, i: (0, 0))
    return pl.pallas_call(
        functools.partial(_fox_prep_kernel, tt=tt, n_pad=n_pad),
        grid=(b, l // tt),
        in_specs=[pl.BlockSpec((1, tt, zw), lambda bi, i: (bi, i, C_FQ // zw)),
                  pl.BlockSpec((1, tt, S_COLS), lambda bi, i: (bi, i, C_S // S_COLS)),
                  vec, vec, vec, vec],
        out_specs=[pl.BlockSpec((1, HEADS, 1, LANES, tt), lambda bi, i: (bi, 0, i, 0, 0)),
                   hd,
                   pl.BlockSpec((1, PAIRS, 1, LANES, tt), lambda bi, i: (bi, 0, i, 0, 0)),
                   pl.BlockSpec((1, 8, LANES), lambda bi, i: (bi, i, 0))],
        out_shape=[jax.ShapeDtypeStruct((b, HEADS, l // tt, LANES, tt), BF16),
                   jax.ShapeDtypeStruct((b, HEADS, l, LANES), BF16),
                   jax.ShapeDtypeStruct((b, PAIRS, l // tt, LANES, tt), BF16),
                   jax.ShapeDtypeStruct((b, (l // tt) * 8, LANES), F32)],
        scratch_shapes=[pltpu.VMEM((1, LANES), F32)],
        compiler_params=pltpu.CompilerParams(
            dimension_semantics=("parallel", "arbitrary"), vmem_limit_bytes=VMEM_LIMIT),
        name="fox_prep",
    )(proj3d, proj3d, c0, tile2(q_g), tile2(k_g), fb)


def _attn_kernel(qt_ref, k_ref, vt_ref, km_ref, o_ref, s_buf, p_buf, acc_buf, *, tq):
    qi = pl.program_id(2)
    last = qi + 1
    key_i = lax.broadcasted_iota(jnp.int32, (tq, tq), 0)
    qry_i = lax.broadcasted_iota(jnp.int32, (tq, tq), 1)
    causal = key_i <= qry_i
    qts = (qt_ref[0, 0, 0], qt_ref[0, 1, 0])
    heads = range(2)

    def real_keys(e):
        off = pl.multiple_of((e - 1) * tq, tq)
        return (k_ref[0, 0, pl.ds(off, tq), :], k_ref[0, 1, pl.ds(off, tq), :])

    def stage_a(kblks, slot):
        for j in heads:
            s_buf[slot, j] = _dot(kblks[j], qts[j])

    def stage_b(slot, stats, mask):
        s = [s_buf[slot, j] for j in heads]
        if mask is not None:
            s = [jnp.where(mask, x, MASKED_KEY) for x in s]
        m_new = [jnp.maximum(stats[j][0], jnp.max(s[j], axis=0, keepdims=True)) for j in heads]
        alpha = [jnp.exp(stats[j][0] - m_new[j]) for j in heads]
        p = [jnp.exp(s[j] - m_new[j]) for j in heads]
        for j in heads:
            p_buf[slot, j] = p[j].astype(BF16)
        l = [alpha[j] * stats[j][1] + jnp.sum(p[j], axis=0, keepdims=True) for j in heads]
        return tuple((m_new[j], l[j]) for j in heads), tuple(alpha)

    def stage_c(e, alpha):
        slot = e % 2
        vt = vt_ref[0, 0, e]
        pv = [_dot(vt[j * HEAD_DIM:(j + 1) * HEAD_DIM, :], p_buf[slot, j]) for j in heads]
        for j in heads:
            acc_buf[j] = alpha[j] * acc_buf[j] + pv[j]

    stats = ((jnp.full((1, tq), MASKED_KEY, F32), jnp.zeros((1, tq), F32)),) * 2
    acc_buf[...] = jnp.zeros_like(acc_buf)
    stage_a((km_ref[0, 0], km_ref[0, 1]), 0)
    stage_a(real_keys(1), 1)
    stats, alpha = stage_b(0, stats, None)

    def body(i, state):
        stats, alpha = state
        stage_c(i, alpha)
        stats, alpha = stage_b((i + 1) % 2, stats, None)
        stage_a(real_keys(i + 2), i % 2)
        return stats, alpha

    stats, alpha = lax.fori_loop(0, qi, body, (stats, alpha))
    stage_c(last - 1, alpha)
    stats, alpha = stage_b(last % 2, stats, causal)
    stage_c(last, alpha)
    out_t = jnp.concatenate([acc_buf[j] / stats[j][1] for j in heads], axis=0)
    o_ref[0] = out_t.T.astype(o_ref.dtype)


def _attn(qt, kp, vt_all, kp_meta, *, tq):
    b, _, l, _ = kp.shape
    nk = vt_all.shape[2]
    assert vt_all.shape[4] == tq and qt.shape[4] == tq and kp_meta.shape[2] == tq
    return pl.pallas_call(
        functools.partial(_attn_kernel, tq=tq),
        grid=(b, PAIRS, l // tq),
        in_specs=[pl.BlockSpec((1, 2, 1, LANES, tq), lambda bi, p, qi: (bi, p, qi, 0, 0)),
                  pl.BlockSpec((1, 2, l, LANES), lambda bi, p, qi: (bi, p, 0, 0)),
                  pl.BlockSpec((1, 1, nk, LANES, tq), lambda bi, p, qi: (bi, p, 0, 0, 0)),
                  pl.BlockSpec((1, 2, tq, LANES), lambda bi, p, qi: (0, p, 0, 0))],
        out_specs=pl.BlockSpec((1, tq, LANES), lambda bi, p, qi: (bi, qi, p)),
        out_shape=jax.ShapeDtypeStruct((b, l, WIDTH), BF16),
        scratch_shapes=[pltpu.VMEM((2, 2, tq, tq), F32), pltpu.VMEM((2, 2, tq, tq), BF16),
                        pltpu.VMEM((2, HEAD_DIM, tq), F32)],
        compiler_params=pltpu.CompilerParams(
            dimension_semantics=("parallel", "parallel", "arbitrary"), vmem_limit_bytes=VMEM_LIMIT),
        name="attn",
    )(qt, kp, vt_all, kp_meta)


def _pick(n, prefs):
    for t in prefs:
        if n % t == 0:
            return t
    raise ValueError(f"no tile for {n}")


def _pack_w_in(w_in):
    rw, fx = 3 * WIDTH + DECAY_LORA + AAA_LORA + GATE_LORA, 3 * WIDTH + HEADS
    wd0, ad0, gd0 = 3 * WIDTH, 3 * WIDTH + DECAY_LORA, 3 * WIDTH + DECAY_LORA + AAA_LORA
    z = lambda n: jnp.zeros((w_in.shape[0], n), w_in.dtype)
    return jnp.concatenate([
        w_in[:, 0:3 * WIDTH], w_in[:, rw:rw + 3 * WIDTH], w_in[:, rw + fx:],
        w_in[:, wd0:ad0], w_in[:, rw + 3 * WIDTH:rw + fx], z(LANES - DECAY_LORA - HEADS),
        w_in[:, ad0:gd0], z(LANES - AAA_LORA), w_in[:, gd0:rw]], axis=1).astype(BF16)


def _pack_mu(mu):
    wd0, ad0, gd0 = 3 * WIDTH, 3 * WIDTH + DECAY_LORA, 3 * WIDTH + DECAY_LORA + AAA_LORA
    z = lambda n: jnp.zeros((n,), mu.dtype)
    mu_s = jnp.concatenate([mu[wd0:ad0], z(LANES - DECAY_LORA), mu[ad0:gd0], z(LANES - AAA_LORA),
                            mu[gd0:]])
    return mu[:3 * WIDTH], mu_s


def _pad_rows(w, rows):
    return jnp.concatenate([w, jnp.zeros((rows - w.shape[0], w.shape[1]), w.dtype)], axis=0)


def _mixer_inputs(h2d, bsz, norm_g, w_pack, tm, tn):
    proj = _rms_matmul(h2d, norm_g, w_pack, tm=tm, tn=tn)
    return proj, proj.reshape(bsz, h2d.shape[0] // bsz, N_PACK)


def kernel(x, meta_tokens, norm1_g, w_in, rwkv_mu, rwkv_w0, rwkv_w2, rwkv_a0, rwkv_a2, rwkv_g2,
           rwkv_k_k, rwkv_k_a, rwkv_r_k, rwkv_gn_w, rwkv_gn_b, fox_q_norm_g, fox_k_norm_g,
           fox_f_bias, w_branch_a, w_branch_b, w_o, norm2_g, w_gate_up, w_down):
    bsz, seq, d = x.shape
    assert d == D_MODEL and norm1_g.shape[0] == 1 and seq % 256 == 0
    m = bsz * seq
    x2d = x.reshape(m, d)
    meta2d = jnp.concatenate(
        [jnp.zeros((META_ROWS - N_META, d), x.dtype), meta_tokens.astype(x.dtype)], axis=0)

    w_pack = _pack_w_in(w_in[0].astype(BF16))
    mu_z, mu_s = _pack_mu(rwkv_mu[0])
    w2p = _pad_rows(rwkv_w2[0], LANES).astype(BF16)
    a2p = _pad_rows(rwkv_a2[0], LANES).astype(BF16)
    g2 = rwkv_g2[0].astype(BF16)
    r_k = rwkv_r_k[0].reshape(WIDTH)

    tm = _pick(m, (1024, 512, 256))
    tt = _pick(seq, (256,))

    _, pm3 = _mixer_inputs(meta2d, 1, norm1_g[0], w_pack, META_ROWS, tn=512)
    zero_z = jnp.zeros((8, 3 * WIDTH), F32)
    zero_s = jnp.zeros((8, S_COLS), F32)
    prep = functools.partial(_rwkv_prep, mu_z=mu_z, mu_s=mu_s, w0=rwkv_w0[0], w2p=w2p,
                             a0=rwkv_a0[0], a2p=a2p, g2=g2, k_k=rwkv_k_k[0], k_a=rwkv_k_a[0])
    wkv = functools.partial(_wkv, gn_w=rwkv_gn_w[0], gn_b=rwkv_gn_b[0], r_k=r_k)
    fox = functools.partial(_fox_prep, q_g=fox_q_norm_g[0], k_g=fox_k_norm_g[0],
                            f_bias=fox_f_bias[0])
    mr = prep(pm3, zero_z, zero_s, tt=META_ROWS)
    _, s_meta = wkv(*mr, jnp.zeros((1, PAIRS, LANES, LANES), F32))
    _, kp_meta, vt_meta, c_meta = fox(pm3, jnp.zeros((1, LANES), F32), tt=META_ROWS,
                                      n_pad=META_ROWS - N_META)

    proj2d, p3 = _mixer_inputs(x2d, bsz, norm1_g[0], w_pack, tm, tn=1536)
    rr = prep(p3, pm3[0, META_ROWS - 8:, 0:3 * WIDTH], pm3[0, META_ROWS - 8:, C_S:], tt=tt)
    ya, _ = wkv(*rr, s_meta)
    qt, kp, vt, _ = fox(p3, c_meta[0, 7:8, :], tt=tt, n_pad=0)
    vt_all = jnp.concatenate([jnp.broadcast_to(vt_meta, (bsz,) + vt_meta.shape[1:]), vt], axis=2)
    yb = _attn(qt, kp, vt_all, kp_meta, tq=tt)

    merged = _merge(ya.reshape(m, WIDTH), yb.reshape(m, WIDTH), w_branch_a[0].astype(BF16),
                    w_branch_b[0].astype(BF16), proj2d, tm=tm, tn=1024)
    h1 = _matmul_res(merged, w_o[0].astype(BF16), x2d, tm=tm, tn=1024)
    act = _rms_matmul_glu(h1, norm2_g[0], w_gate_up[0].astype(BF16), tm=tm, tn=512)
    out = _matmul_res(act, w_down[0].astype(BF16), h1, tm=tm, tn=512)
    return out.reshape(bsz, seq, d)
```

```python
import functools

import jax
import jax.numpy as jnp
from jax import lax
from jax.experimental import pallas as pl
from jax.experimental.pallas import tpu as pltpu

F32 = jnp.float32
BF16 = jnp.bfloat16

D_MODEL = 2048
N_META = 16
HEAD_DIM = 64
HEADS = 16
WIDTH = HEADS * HEAD_DIM
PAIRS = HEADS // 2
DECAY_LORA = 96
AAA_LORA = 96
GATE_LORA = 256
D_FF = 5632
RMS_EPS = 1e-6
GN_EPS = 64e-5
ATTN_SCALE = HEAD_DIM ** -0.5
LOG2E = 1.4426950408889634

LANES = 128
META_ROWS = 256
CHUNK = 64
MASKED_KEY = -1e30

C_R, C_K, C_V = 0, 1024, 2048
C_FQ, C_FK, C_FV = 3072, 4096, 5120
C_GA, C_GB = 6144, 8192
C_S = 10240
S_COLS = 512
FL_LANE = 96
N_PACK = C_S + S_COLS

VMEM_LIMIT = 56 * 1024 * 1024


def _sigmoid(x):
    return 0.5 * jnp.tanh(0.5 * x) + 0.5


def _softplus(x):
    return jnp.maximum(x, 0.0) + jnp.log(1.0 + jnp.exp(-jnp.abs(x)))


def _dot(a, b):
    return jnp.dot(a, b, preferred_element_type=F32)


def _dot_nt(a, b):
    return lax.dot_general(a, b, (((1,), (1,)), ((), ())), preferred_element_type=F32)


def _dot_tn(a, b):
    return lax.dot_general(a, b, (((0,), (0,)), ((), ())), preferred_element_type=F32)


def _split3(x):
    hi = x.astype(BF16)
    r1 = x - hi.astype(F32)
    mid = r1.astype(BF16)
    lo = (r1 - mid.astype(F32)).astype(BF16)
    return hi, mid, lo


_NN = (((1,), (0,)), ((), ()))
_NT = (((1,), (1,)), ((), ()))
_TN = (((0,), (0,)), ((), ()))


def _head_ones():
    r = lax.broadcasted_iota(jnp.int32, (LANES, LANES), 0) // HEAD_DIM
    c = lax.broadcasted_iota(jnp.int32, (LANES, LANES), 1) // HEAD_DIM
    return jnp.where(r == c, 1.0, 0.0).astype(BF16)


def _head_sum(x, ones_bd):
    hi = x.astype(BF16)
    lo = (x - hi.astype(F32)).astype(BF16)
    return _dot(jnp.concatenate([hi, lo], axis=1), jnp.concatenate([ones_bd, ones_bd], axis=0))


def _rms_mm_kernel(x_ref, g_ref, w_ref, o_ref, xn_ref):
    @pl.when(pl.program_id(1) == 0)
    def _():
        x = x_ref[...]
        ms = jnp.mean(x * x, axis=-1, keepdims=True)
        xn_ref[...] = ((x * lax.rsqrt(ms + RMS_EPS)) * g_ref[...]).astype(BF16)

    o_ref[...] = _dot(xn_ref[...], w_ref[...]).astype(o_ref.dtype)


def _rms_glu_kernel(x_ref, g_ref, wg_ref, wu_ref, o_ref, xn_ref):
    @pl.when(pl.program_id(1) == 0)
    def _():
        x = x_ref[...]
        ms = jnp.mean(x * x, axis=-1, keepdims=True)
        xn_ref[...] = ((x * lax.rsqrt(ms + RMS_EPS)) * g_ref[...]).astype(BF16)

    xn = xn_ref[...]
    gate = _dot(xn, wg_ref[...])
    up = _dot(xn, wu_ref[...])
    o_ref[...] = (gate * _sigmoid(gate) * up).astype(o_ref.dtype)


def _rms_matmul(x2d, g, w_bf16, *, tm, tn, out_dtype=F32):
    m, d = x2d.shape
    n = w_bf16.shape[1]
    return pl.pallas_call(
        _rms_mm_kernel,
        grid=(m // tm, n // tn),
        in_specs=[pl.BlockSpec((tm, d), lambda i, j: (i, 0)),
                  pl.BlockSpec((1, d), lambda i, j: (0, 0)),
                  pl.BlockSpec((d, tn), lambda i, j: (0, j))],
        out_specs=pl.BlockSpec((tm, tn), lambda i, j: (i, j)),
        out_shape=jax.ShapeDtypeStruct((m, n), out_dtype),
        scratch_shapes=[pltpu.VMEM((tm, d), BF16)],
        compiler_params=pltpu.CompilerParams(
            dimension_semantics=("parallel", "arbitrary"), vmem_limit_bytes=VMEM_LIMIT),
        name="rms_matmul",
    )(x2d, g.reshape(1, d), w_bf16)


def _rms_matmul_glu(x2d, g, w_bf16, *, tm, tn):
    m, d = x2d.shape
    n = w_bf16.shape[1] // 2
    up_off = n // tn
    return pl.pallas_call(
        _rms_glu_kernel,
        grid=(m // tm, n // tn),
        in_specs=[pl.BlockSpec((tm, d), lambda i, j: (i, 0)),
                  pl.BlockSpec((1, d), lambda i, j: (0, 0)),
                  pl.BlockSpec((d, tn), lambda i, j: (0, j)),
                  pl.BlockSpec((d, tn), lambda i, j: (0, j + up_off))],
        out_specs=pl.BlockSpec((tm, tn), lambda i, j: (i, j)),
        out_shape=jax.ShapeDtypeStruct((m, n), BF16),
        scratch_shapes=[pltpu.VMEM((tm, d), BF16)],
        compiler_params=pltpu.CompilerParams(
            dimension_semantics=("parallel", "arbitrary"), vmem_limit_bytes=VMEM_LIMIT),
        name="rms_matmul_glu",
    )(x2d, g.reshape(1, d), w_bf16, w_bf16)


def _mm_res_kernel(a_ref, b_ref, r_ref, o_ref):
    o_ref[...] = _dot(a_ref[...], b_ref[...]) + r_ref[...]


def _matmul_res(a, b, res, *, tm, tn):
    m, kk = a.shape
    n = b.shape[1]
    return pl.pallas_call(
        _mm_res_kernel,
        grid=(m // tm, n // tn),
        in_specs=[pl.BlockSpec((tm, kk), lambda i, j: (i, 0)),
                  pl.BlockSpec((kk, tn), lambda i, j: (0, j)),
                  pl.BlockSpec((tm, tn), lambda i, j: (i, j))],
        out_specs=pl.BlockSpec((tm, tn), lambda i, j: (i, j)),
        out_shape=jax.ShapeDtypeStruct((m, n), F32),
        compiler_params=pltpu.CompilerParams(
            dimension_semantics=("parallel", "arbitrary"), vmem_limit_bytes=VMEM_LIMIT),
        name="matmul_res",
    )(a, b, res)


def _merge_kernel(ya_ref, yb_ref, wa_ref, wb_ref, za_ref, zb_ref, o_ref):
    ta = _dot(ya_ref[...], wa_ref[...])
    tb = _dot(yb_ref[...], wb_ref[...])
    o_ref[...] = (_sigmoid(za_ref[...]) * ta + _sigmoid(zb_ref[...]) * tb).astype(o_ref.dtype)


def _merge(ya, yb, wa, wb, proj2d, *, tm, tn):
    m, kk = ya.shape
    n = wa.shape[1]
    oa, ob = C_GA // tn, C_GB // tn
    return pl.pallas_call(
        _merge_kernel,
        grid=(m // tm, n // tn),
        in_specs=[pl.BlockSpec((tm, kk), lambda i, j: (i, 0)),
                  pl.BlockSpec((tm, kk), lambda i, j: (i, 0)),
                  pl.BlockSpec((kk, tn), lambda i, j: (0, j)),
                  pl.BlockSpec((kk, tn), lambda i, j: (0, j)),
                  pl.BlockSpec((tm, tn), lambda i, j: (i, j + oa)),
                  pl.BlockSpec((tm, tn), lambda i, j: (i, j + ob))],
        out_specs=pl.BlockSpec((tm, tn), lambda i, j: (i, j)),
        out_shape=jax.ShapeDtypeStruct((m, n), BF16),
        compiler_params=pltpu.CompilerParams(
            dimension_semantics=("parallel", "arbitrary"), vmem_limit_bytes=VMEM_LIMIT),
        name="merge",
    )(ya, yb, wa, wb, proj2d, proj2d)


def _rwkv_prep_kernel(z_ref, zs_ref, pz_ref, ps_ref, p0z_ref, p0s_ref, mu_ref, mus_ref,
                      w0_ref, w2_ref, a0_ref, a2_ref, g2_ref, kk_ref, ka_ref,
                      r_out, lw_out, k_out, v_out, a_out, b_out, g_out, *, tt):
    first = pl.program_id(1) == 0
    z = z_ref[0]
    zs = zs_ref[0]
    pz = jnp.where(first, p0z_ref[7:8, :], pz_ref[0, 7:8, :])
    ps = jnp.where(first, p0s_ref[7:8, :], ps_ref[0, 7:8, :])
    row = lax.broadcasted_iota(jnp.int32, (tt, 1), 0)

    def lerp(cur, prev_last, mu):
        prev = jnp.where(row == 0, prev_last, pltpu.roll(cur, 1, axis=0))
        return cur + (prev - cur) * mu

    zm = lerp(z, pz, mu_ref[...])
    zsm = lerp(zs, ps, mus_ref[...])
    r = zm[:, C_R:C_R + WIDTH]
    k = zm[:, C_K:C_K + WIDTH]
    v = zm[:, C_V:C_V + WIDTH]
    wd = zsm[:, 0:LANES]
    ad = zsm[:, LANES:2 * LANES]
    gd = zsm[:, 2 * LANES:4 * LANES]

    w_log = -_softplus(-(w0_ref[...] + _dot(jnp.tanh(wd).astype(BF16), w2_ref[...]))) - 0.5
    lw_out[0] = -jnp.exp(w_log)
    a_sig = _sigmoid(a0_ref[...] + _dot(ad.astype(BF16), a2_ref[...]))
    g_out[0] = _dot(_sigmoid(gd).astype(BF16), g2_ref[...]).astype(g_out.dtype)

    ones_bd = _head_ones()
    kk = k * kk_ref[...]
    kk2 = kk * kk
    nrm = jnp.concatenate(
        [_head_sum(kk2[:, p * LANES:(p + 1) * LANES], ones_bd) for p in range(PAIRS)], axis=1)
    kk = kk * lax.rsqrt(jnp.maximum(nrm, 1e-24))
    r_out[0] = r.astype(r_out.dtype)
    k_out[0] = (k * (1.0 + (a_sig - 1.0) * ka_ref[...])).astype(k_out.dtype)
    v_out[0] = v.astype(v_out.dtype)
    a_out[0] = (-kk).astype(a_out.dtype)
    b_out[0] = (kk * a_sig).astype(b_out.dtype)


def _rwkv_prep(proj3d, prev0_z, prev0_s, mu_z, mu_s, w0, w2p, a0, a2p, g2, k_k, k_a, *, tt):
    b, l, _ = proj3d.shape
    s_blk = C_S // S_COLS
    zw = 3 * WIDTH
    row = lambda a: a.reshape(1, -1)
    vec = lambda n: pl.BlockSpec((1, n), lambda bi, i: (0, 0))
    full = lambda r, c: pl.BlockSpec((r, c), lambda bi, i: (0, 0))
    out_spec = pl.BlockSpec((1, tt, WIDTH), lambda bi, i: (bi, i, 0))
    sds = lambda dt: jax.ShapeDtypeStruct((b, l, WIDTH), dt)
    return pl.pallas_call(
        functools.partial(_rwkv_prep_kernel, tt=tt),
        grid=(b, l // tt),
        in_specs=[pl.BlockSpec((1, tt, zw), lambda bi, i: (bi, i, 0)),
                  pl.BlockSpec((1, tt, S_COLS), lambda bi, i: (bi, i, s_blk)),
                  pl.BlockSpec((1, 8, zw), lambda bi, i: (bi, jnp.maximum(i * (tt // 8) - 1, 0), 0)),
                  pl.BlockSpec((1, 8, S_COLS),
                               lambda bi, i: (bi, jnp.maximum(i * (tt // 8) - 1, 0), s_blk)),
                  full(8, zw), full(8, S_COLS),
                  vec(zw), vec(S_COLS),
                  vec(WIDTH), full(LANES, WIDTH), vec(WIDTH), full(LANES, WIDTH),
                  full(GATE_LORA, WIDTH), vec(WIDTH), vec(WIDTH)],
        out_specs=[out_spec] * 7,
        out_shape=[sds(BF16), sds(F32)] + [sds(BF16)] * 5,
        compiler_params=pltpu.CompilerParams(
            dimension_semantics=("parallel", "arbitrary"), vmem_limit_bytes=VMEM_LIMIT),
        name="rwkv_prep",
    )(proj3d, proj3d, proj3d, proj3d, prev0_z, prev0_s, row(mu_z), row(mu_s),
      row(w0), w2p, row(a0), a2p, g2, row(k_k), row(k_a))


def _bf(x):
    return x.astype(BF16)


WKV_GROUP = 8


def _wkv_group(a_t, r_t, b_t, k_t, b_h, k_h, v, s, w_total, masks):
    head0, strict, lower, eye = masks
    dg = functools.partial(lax.dot_general, preferred_element_type=F32)
    n = range(len(s))

    def bd(x):
        return jnp.concatenate([jnp.where(head0, x, 0.0), jnp.where(head0, 0.0, x)], axis=0)

    lhs = [_bf(jnp.concatenate([bd(a_t[i]), bd(r_t[i])], axis=0)) for i in n]
    rhs = [_bf(jnp.concatenate([bd(b_t[i]), bd(k_t[i])], axis=0)) for i in n]
    v_bd = [_bf(bd(v[i])) for i in n]
    gram = [dg(lhs[i], rhs[i], _NT) for i in n]
    a_s = [dg(lhs[i], _bf(s[i]), _NT) for i in n]
    n_ab = [jnp.where(strict, g[0:LANES, 0:LANES], 0.0) for g in gram]
    a_ak = [jnp.where(strict, g[0:LANES, LANES:2 * LANES], 0.0) for g in gram]
    a_r = [_bf(jnp.concatenate([jnp.where(lower, g[LANES:2 * LANES, 0:LANES], 0.0),
                                jnp.where(lower, g[LANES:2 * LANES, LANES:2 * LANES], 0.0)], axis=1))
           for g in gram]
    rhs_u = [a_s[i][0:LANES] + dg(_bf(a_ak[i]), v_bd[i], _NN) for i in n]

    p_inv = [eye + x for x in n_ab]
    q = [_bf(x) for x in n_ab]
    for _ in range(5):
        q = [_bf(dg(x, x, _NN)) for x in q]
        p_inv = [p_inv[i] + dg(q[i], _bf(p_inv[i]), _NN) for i in n]
    u = [dg(_bf(p_inv[i]), _bf(rhs_u[i]), _NN) for i in n]

    uv = [jnp.concatenate([_bf(u[i]), v_bd[i]], axis=0) for i in n]
    y_bd = [a_s[i][LANES:2 * LANES] + dg(a_r[i], uv[i], _NN) for i in n]
    y = [x[0:CHUNK] + x[CHUNK:2 * CHUNK] for x in y_bd]
    s_new = [w_total[i] * s[i]
             + dg(uv[i], _bf(jnp.concatenate([bd(b_h[i]), bd(k_h[i])], axis=0)), _TN) for i in n]
    return y, s_new


def _wkv_kernel(r_ref, lw_ref, k_ref, v_ref, a_ref, b_ref, g_ref, s0_ref,
                gnw_ref, gnb_ref, rk_ref, y_ref, sfin_ref, s_scr, *, nc):
    c = pl.program_id(1)

    @pl.when(c == 0)
    def _():
        s_scr[...] = s0_ref[0]

    r = r_ref[0].astype(F32)
    lw = lw_ref[0]
    k = k_ref[0].astype(F32)
    v = v_ref[0].astype(F32)
    a = a_ref[0].astype(F32)
    b = b_ref[0].astype(F32)

    ti = lax.broadcasted_iota(jnp.int32, (CHUNK, CHUNK), 0)
    tj = lax.broadcasted_iota(jnp.int32, (CHUNK, CHUNK), 1)
    ltri = jnp.where(tj <= ti, 1.0, 0.0).astype(BF16)
    hi, mid, lo = _split3(lw)
    cum = _dot(jnp.concatenate([ltri, ltri, ltri], axis=1),
               jnp.concatenate([hi, mid, lo], axis=0))
    total = cum[CHUNK - 1:CHUNK, :]
    e_inv = jnp.exp(-cum)
    e_end = jnp.exp(total - cum)
    a_t = a * jnp.exp(cum - lw)
    r_t = r * jnp.exp(cum)
    b_t = b * e_inv
    k_t = k * e_inv
    b_h = b * e_end
    k_h = k * e_end
    w_total = jnp.exp(total)

    lane = lax.broadcasted_iota(jnp.int32, (1, LANES), 1)
    bi = lax.broadcasted_iota(jnp.int32, (LANES, LANES), 0)
    bj = lax.broadcasted_iota(jnp.int32, (LANES, LANES), 1)
    same = (bi // CHUNK) == (bj // CHUNK)
    masks = (lane < HEAD_DIM,
             same & ((bj % CHUNK) < (bi % CHUNK)),
             same & ((bj % CHUNK) <= (bi % CHUNK)),
             jnp.where(bi == bj, 1.0, 0.0))

    ones_bd = _head_ones()
    inv_n = 1.0 / HEAD_DIM
    for p0 in range(0, PAIRS, WKV_GROUP):
        pairs = range(p0, p0 + WKV_GROUP)
        sls = [slice(p * LANES, (p + 1) * LANES) for p in pairs]
        cut = lambda x: [x[:, sl] for sl in sls]
        ys, s_news = _wkv_group(cut(a_t), cut(r_t), cut(b_t), cut(k_t), cut(b_h), cut(k_h), cut(v),
                                [s_scr[p] for p in pairs], cut(w_total), masks)
        for p, s_new in zip(pairs, s_news):
            s_scr[p] = s_new

            @pl.when(c == nc - 1)
            def _(p=p, s_new=s_new):
                sfin_ref[0, p] = s_new

        y_all = jnp.concatenate(ys, axis=0)
        rk_all = jnp.concatenate([r[:, sl] * k[:, sl] * rk_ref[:, sl] for sl in sls], axis=0)
        bonus_all = _head_sum(rk_all, ones_bd)
        d_all = y_all - _head_sum(y_all, ones_bd) * inv_n
        rstd_all = lax.rsqrt(_head_sum(d_all * d_all, ones_bd) * inv_n + GN_EPS)
        for i, sl in enumerate(sls):
            rows = slice(i * CHUNK, (i + 1) * CHUNK)
            yn = d_all[rows] * rstd_all[rows] * gnw_ref[:, sl] + gnb_ref[:, sl]
            y_ref[0, :, sl] = ((yn + bonus_all[rows] * v[:, sl])
                               * g_ref[0, :, sl].astype(F32)).astype(y_ref.dtype)


def _wkv(r, lw, k, v, a, b, g, s0, gn_w, gn_b, r_k):
    bsz, l, _ = r.shape
    nc = l // CHUNK
    tok = pl.BlockSpec((1, CHUNK, WIDTH), lambda bi, c: (bi, c, 0))
    vec = pl.BlockSpec((1, WIDTH), lambda bi, c: (0, 0))
    st_in = pl.BlockSpec((1, PAIRS, LANES, LANES), lambda bi, c: (0, 0, 0, 0))
    st_out = pl.BlockSpec((1, PAIRS, LANES, LANES), lambda bi, c: (bi, 0, 0, 0))
    row = lambda x: x.reshape(1, WIDTH)
    return pl.pallas_call(
        functools.partial(_wkv_kernel, nc=nc),
        grid=(bsz, nc),
        in_specs=[tok] * 7 + [st_in, vec, vec, vec],
        out_specs=[tok, st_out],
        out_shape=[jax.ShapeDtypeStruct((bsz, l, WIDTH), BF16),
                   jax.ShapeDtypeStruct((bsz, PAIRS, LANES, LANES), F32)],
        scratch_shapes=[pltpu.VMEM((PAIRS, LANES, LANES), F32)],
        compiler_params=pltpu.CompilerParams(
            dimension_semantics=("parallel", "arbitrary"), vmem_limit_bytes=VMEM_LIMIT),
        name="wkv",
    )(r, lw, k, v, a, b, g, s0, row(gn_w), row(gn_b), row(r_k))


def _fox_prep_kernel(z_ref, zs_ref, c0_ref, qg_ref, kg_ref, fb_ref,
                     qt_out, k_out, vt_out, c_out, carry, *, tt, n_pad):
    i = pl.program_id(1)

    @pl.when(i == 0)
    def _():
        carry[...] = c0_ref[...]

    z = z_ref[0]
    fl = zs_ref[0][:, 0:LANES]
    log_f = -_softplus(-(fl + fb_ref[...]))
    ti = lax.broadcasted_iota(jnp.int32, (tt, tt), 0)
    tj = lax.broadcasted_iota(jnp.int32, (tt, tt), 1)
    ltri = jnp.where(tj <= ti, 1.0, 0.0).astype(BF16)
    hi, mid, lo = _split3(log_f)
    cum = carry[...] + _dot(jnp.concatenate([ltri, ltri, ltri], axis=1),
                            jnp.concatenate([hi, mid, lo], axis=0))
    carry[...] = cum[tt - 1:tt, :]
    c_out[0] = cum[tt - 8:tt, :]
    c_hi, c_mid, c_lo = (t.astype(F32) for t in _split3(cum * LOG2E))
    neg_hi, neg_mid, neg_lo = -c_hi, -c_mid, -c_lo
    if n_pad:
        grow = i * tt + lax.broadcasted_iota(jnp.int32, (tt, 1), 0)
        neg_hi = jnp.where(grow < n_pad, MASKED_KEY, neg_hi)

    for p in range(PAIRS):
        vt_out[0, p, 0] = z[:, 2 * WIDTH + p * LANES:2 * WIDTH + (p + 1) * LANES].T.astype(BF16)
    ones_bd = _head_ones()
    lane = lax.broadcasted_iota(jnp.int32, (1, LANES), 1)
    inv_n = 1.0 / HEAD_DIM
    for p in range(PAIRS):
        zq = z[:, p * LANES:(p + 1) * LANES]
        zk = z[:, WIDTH + p * LANES:WIDTH + (p + 1) * LANES]
        qn = zq * lax.rsqrt(_head_sum(zq * zq, ones_bd) * inv_n + RMS_EPS) * (qg_ref[...] * (ATTN_SCALE * LOG2E))
        kn = zk * lax.rsqrt(_head_sum(zk * zk, ones_bd) * inv_n + RMS_EPS) * kg_ref[...]
        for j in range(2):
            h = 2 * p + j
            own = (lane < HEAD_DIM) if j == 0 else (lane >= HEAD_DIM)
            e = lane - (HEAD_DIM if j == 0 else 0)
            col = lambda t: t[:, FL_LANE + h:FL_LANE + h + 1]
            q_extra = jnp.where((e >= 0) & (e < 3), 1.0, 0.0)
            k_extra = jnp.where(e == 0, col(neg_hi), jnp.where(e == 1, col(neg_mid), jnp.where(
                e == 2, col(neg_lo), 0.0)))
            qt_out[0, h, 0] = jnp.where(own, qn, q_extra).T.astype(BF16)
            k_out[0, h] = jnp.where(own, kn, k_extra).astype(BF16)


def _fox_prep(proj3d, c0, q_g, k_g, f_bias, *, tt, n_pad):
    b, l, _ = proj3d.shape
    zw = 3 * WIDTH
    tile2 = lambda g: jnp.concatenate([g, g]).reshape(1, LANES)
    fb = jnp.concatenate([jnp.zeros((FL_LANE,), F32), f_bias,
                          jnp.zeros((LANES - FL_LANE - HEADS,), F32)]).reshape(1, LANES)
    hd = pl.BlockSpec((1, HEADS, tt, LANES), lambda bi, i: (bi, 0, i, 0))
    vec = pl.BlockSpec((1, LANES), lambda bi, i: (0, 0))
    return pl.pallas_call(
        functools.partial(_fox_prep_kernel, tt=tt, n_pad=n_pad),
        grid=(b, l // tt),
        in_specs=[pl.BlockSpec((1, tt, zw), lambda bi, i: (bi, i, C_FQ // zw)),
                  pl.BlockSpec((1, tt, S_COLS), lambda bi, i: (bi, i, C_S // S_COLS)),
                  vec, vec, vec, vec],
        out_specs=[pl.BlockSpec((1, HEADS, 1, LANES, tt), lambda bi, i: (bi, 0, i, 0, 0)),
                   hd,
                   pl.BlockSpec((1, PAIRS, 1, LANES, tt), lambda bi, i: (bi, 0, i, 0, 0)),
                   pl.BlockSpec((1, 8, LANES), lambda bi, i: (bi, i, 0))],
        out_shape=[jax.ShapeDtypeStruct((b, HEADS, l // tt, LANES, tt), BF16),
                   jax.ShapeDtypeStruct((b, HEADS, l, LANES), BF16),
                   jax.ShapeDtypeStruct((b, PAIRS, l // tt, LANES, tt), BF16),
                   jax.ShapeDtypeStruct((b, (l // tt) * 8, LANES), F32)],
        scratch_shapes=[pltpu.VMEM((1, LANES), F32)],
        compiler_params=pltpu.CompilerParams(
            dimension_semantics=("parallel", "arbitrary"), vmem_limit_bytes=VMEM_LIMIT),
        name="fox_prep",
    )(proj3d, proj3d, c0, tile2(q_g), tile2(k_g), fb)


def _attn_kernel(qt_ref, k_ref, vt_ref, km_ref, o_ref, s_buf, p_buf, acc_buf, *, tk):
    qi = pl.program_id(2)
    key_i = lax.broadcasted_iota(jnp.int32, (tk, tk), 0)
    qry_i = lax.broadcasted_iota(jnp.int32, (tk, tk), 1)
    causal = key_i <= qry_i
    every = (0, 1, 2, 3)
    half1 = (1, 3)
    qts = [qt_ref[0, n // 2, n % 2] for n in every]

    def real_keys(r):
        off = pl.multiple_of(r * tk, tk)
        return (k_ref[0, 0, pl.ds(off, tk), :], k_ref[0, 1, pl.ds(off, tk), :])

    def stage_a(kblks, slot, which):
        for n in which:
            s_buf[slot, n] = _dot(kblks[n // 2], qts[n])

    def stage_b(slot, stats, alpha, masks):
        which = tuple(masks)
        s = {n: s_buf[slot, n] for n in which}
        for n in which:
            if masks[n] is not None:
                s[n] = jnp.where(masks[n], s[n], MASKED_KEY)
        m_new = {n: jnp.maximum(stats[n][0], jnp.max(s[n], axis=0, keepdims=True)) for n in which}
        a_new = {n: jnp.exp2(stats[n][0] - m_new[n]) for n in which}
        p = {n: jnp.exp2(s[n] - m_new[n]) for n in which}
        for n in which:
            p_buf[slot, n] = p[n].astype(BF16)
        l_new = {n: a_new[n] * stats[n][1] + jnp.sum(p[n], axis=0, keepdims=True) for n in which}
        stats = tuple((m_new[n], l_new[n]) if n in which else stats[n] for n in every)
        alpha = tuple(a_new[n] if n in which else alpha[n] for n in every)
        return stats, alpha

    def stage_c(slot, e, alpha, which):
        vt = vt_ref[0, 0, e]
        pv = {n: _dot(vt[(n // 2) * HEAD_DIM:(n // 2 + 1) * HEAD_DIM, :], p_buf[slot, n])
              for n in which}
        for n in which:
            acc_buf[n] = alpha[n] * acc_buf[n] + pv[n]

    none4 = {n: None for n in every}
    stats = ((jnp.full((1, tk), MASKED_KEY, F32), jnp.zeros((1, tk), F32)),) * 4
    alpha = (jnp.ones((1, tk), F32),) * 4
    acc_buf[...] = jnp.zeros_like(acc_buf)
    p_buf[1] = jnp.zeros(p_buf.shape[1:], BF16)
    stage_a((km_ref[0, 0], km_ref[0, 1]), 0, every)

    def body(i, state):
        stats, alpha_prev = state
        stage_a(real_keys(2 * i), 1, every)
        stats, alpha_a = stage_b(0, stats, alpha_prev, none4)
        stage_c(1, jnp.maximum(2 * i - 1, 0), alpha_prev, every)
        stage_a(real_keys(2 * i + 1), 0, every)
        stats, alpha_b = stage_b(1, stats, alpha_a, none4)
        stage_c(0, 2 * i, alpha_a, every)
        return stats, alpha_b

    stats, alpha_prev = lax.fori_loop(0, qi, body, (stats, alpha))
    e0 = 2 * qi
    stage_a(real_keys(e0), 1, every)
    stats, alpha_a = stage_b(0, stats, alpha_prev, none4)
    stage_c(1, jnp.maximum(e0 - 1, 0), alpha_prev, every)
    stage_a(real_keys(e0 + 1), 0, half1)
    stats, alpha_b = stage_b(1, stats, alpha_a, {0: causal, 1: None, 2: causal, 3: None})
    stage_c(0, e0, alpha_a, every)
    stats, alpha_c = stage_b(0, stats, alpha_b, {1: causal, 3: causal})
    stage_c(1, e0 + 1, alpha_b, every)
    stage_c(0, e0 + 2, alpha_c, half1)
    for u in range(2):
        out_t = jnp.concatenate([acc_buf[u] / stats[u][1], acc_buf[2 + u] / stats[2 + u][1]], axis=0)
        o_ref[0, u * tk:(u + 1) * tk, :] = out_t.T.astype(o_ref.dtype)


def _attn(qt, kp, vt_all, kp_meta, *, tk):
    b, _, l, _ = kp.shape
    nk = vt_all.shape[2]
    assert vt_all.shape[4] == tk and qt.shape[4] == tk and kp_meta.shape[2] == tk
    return pl.pallas_call(
        functools.partial(_attn_kernel, tk=tk),
        grid=(b, PAIRS, l // (2 * tk)),
        in_specs=[pl.BlockSpec((1, 2, 2, LANES, tk), lambda bi, p, qi: (bi, p, qi, 0, 0)),
                  pl.BlockSpec((1, 2, l, LANES), lambda bi, p, qi: (bi, p, 0, 0)),
                  pl.BlockSpec((1, 1, nk, LANES, tk), lambda bi, p, qi: (bi, p, 0, 0, 0)),
                  pl.BlockSpec((1, 2, tk, LANES), lambda bi, p, qi: (0, p, 0, 0))],
        out_specs=pl.BlockSpec((1, 2 * tk, LANES), lambda bi, p, qi: (bi, qi, p)),
        out_shape=jax.ShapeDtypeStruct((b, l, WIDTH), BF16),
        scratch_shapes=[pltpu.VMEM((2, 4, tk, tk), F32), pltpu.VMEM((2, 4, tk, tk), BF16),
                        pltpu.VMEM((4, HEAD_DIM, tk), F32)],
        compiler_params=pltpu.CompilerParams(
            dimension_semantics=("parallel", "parallel", "arbitrary"), vmem_limit_bytes=VMEM_LIMIT),
        name="attn",
    )(qt, kp, vt_all, kp_meta)


def _pick(n, prefs):
    for t in prefs:
        if n % t == 0:
            return t
    raise ValueError(f"no tile for {n}")


def _pack_w_in(w_in):
    rw, fx = 3 * WIDTH + DECAY_LORA + AAA_LORA + GATE_LORA, 3 * WIDTH + HEADS
    wd0, ad0, gd0 = 3 * WIDTH, 3 * WIDTH + DECAY_LORA, 3 * WIDTH + DECAY_LORA + AAA_LORA
    z = lambda n: jnp.zeros((w_in.shape[0], n), w_in.dtype)
    return jnp.concatenate([
        w_in[:, 0:3 * WIDTH], w_in[:, rw:rw + 3 * WIDTH], w_in[:, rw + fx:],
        w_in[:, wd0:ad0], w_in[:, rw + 3 * WIDTH:rw + fx], z(LANES - DECAY_LORA - HEADS),
        w_in[:, ad0:gd0], z(LANES - AAA_LORA), w_in[:, gd0:rw]], axis=1).astype(BF16)


def _pack_mu(mu):
    wd0, ad0, gd0 = 3 * WIDTH, 3 * WIDTH + DECAY_LORA, 3 * WIDTH + DECAY_LORA + AAA_LORA
    z = lambda n: jnp.zeros((n,), mu.dtype)
    mu_s = jnp.concatenate([mu[wd0:ad0], z(LANES - DECAY_LORA), mu[ad0:gd0], z(LANES - AAA_LORA),
                            mu[gd0:]])
    return mu[:3 * WIDTH], mu_s


def _pad_rows(w, rows):
    return jnp.concatenate([w, jnp.zeros((rows - w.shape[0], w.shape[1]), w.dtype)], axis=0)


def _mixer_inputs(h2d, bsz, norm_g, w_pack, tm, tn):
    proj = _rms_matmul(h2d, norm_g, w_pack, tm=tm, tn=tn)
    return proj, proj.reshape(bsz, h2d.shape[0] // bsz, N_PACK)


def kernel(x, meta_tokens, norm1_g, w_in, rwkv_mu, rwkv_w0, rwkv_w2, rwkv_a0, rwkv_a2, rwkv_g2,
           rwkv_k_k, rwkv_k_a, rwkv_r_k, rwkv_gn_w, rwkv_gn_b, fox_q_norm_g, fox_k_norm_g,
           fox_f_bias, w_branch_a, w_branch_b, w_o, norm2_g, w_gate_up, w_down):
    bsz, seq, d = x.shape
    assert d == D_MODEL and norm1_g.shape[0] == 1 and seq % 512 == 0
    m = bsz * seq
    x2d = x.reshape(m, d)
    meta2d = jnp.concatenate(
        [jnp.zeros((META_ROWS - N_META, d), x.dtype), meta_tokens.astype(x.dtype)], axis=0)

    w_pack = _pack_w_in(w_in[0].astype(BF16))
    mu_z, mu_s = _pack_mu(rwkv_mu[0])
    w2p = _pad_rows(rwkv_w2[0], LANES).astype(BF16)
    a2p = _pad_rows(rwkv_a2[0], LANES).astype(BF16)
    g2 = rwkv_g2[0].astype(BF16)
    r_k = rwkv_r_k[0].reshape(WIDTH)

    tm = _pick(m, (1024, 512, 256))
    tt = _pick(seq, (256,))

    _, pm3 = _mixer_inputs(meta2d, 1, norm1_g[0], w_pack, META_ROWS, tn=512)
    zero_z = jnp.zeros((8, 3 * WIDTH), F32)
    zero_s = jnp.zeros((8, S_COLS), F32)
    prep = functools.partial(_rwkv_prep, mu_z=mu_z, mu_s=mu_s, w0=rwkv_w0[0], w2p=w2p,
                             a0=rwkv_a0[0], a2p=a2p, g2=g2, k_k=rwkv_k_k[0], k_a=rwkv_k_a[0])
    wkv = functools.partial(_wkv, gn_w=rwkv_gn_w[0], gn_b=rwkv_gn_b[0], r_k=r_k)
    fox = functools.partial(_fox_prep, q_g=fox_q_norm_g[0], k_g=fox_k_norm_g[0],
                            f_bias=fox_f_bias[0])
    mr = prep(pm3, zero_z, zero_s, tt=META_ROWS)
    _, s_meta = wkv(*mr, jnp.zeros((1, PAIRS, LANES, LANES), F32))
    _, kp_meta, vt_meta, c_meta = fox(pm3, jnp.zeros((1, LANES), F32), tt=META_ROWS,
                                      n_pad=META_ROWS - N_META)

    proj2d, p3 = _mixer_inputs(x2d, bsz, norm1_g[0], w_pack, tm, tn=1536)
    rr = prep(p3, pm3[0, META_ROWS - 8:, 0:3 * WIDTH], pm3[0, META_ROWS - 8:, C_S:], tt=tt)
    ya, _ = wkv(*rr, s_meta)
    qt, kp, vt, _ = fox(p3, c_meta[0, 7:8, :], tt=tt, n_pad=0)
    vt_all = jnp.concatenate([jnp.broadcast_to(vt_meta, (bsz,) + vt_meta.shape[1:]), vt], axis=2)
    yb = _attn(qt, kp, vt_all, kp_meta, tk=tt)

    merged = _merge(ya.reshape(m, WIDTH), yb.reshape(m, WIDTH), w_branch_a[0].astype(BF16),
                    w_branch_b[0].astype(BF16), proj2d, tm=tm, tn=1024)
    h1 = _matmul_res(merged, w_o[0].astype(BF16), x2d, tm=tm, tn=1024)
    act = _rms_matmul_glu(h1, norm2_g[0], w_gate_up[0].astype(BF16), tm=tm, tn=512)
    out = _matmul_res(act, w_down[0].astype(BF16), h1, tm=tm, tn=512)
    return out.reshape(bsz, seq, d)
```

```python
import functools

import jax
import jax.numpy as jnp
from jax import lax
from jax.experimental import pallas as pl
from jax.experimental.pallas import tpu as pltpu

F32 = jnp.float32
BF16 = jnp.bfloat16

D_MODEL = 2048
N_META = 16
HEAD_DIM = 64
HEADS = 16
WIDTH = HEADS * HEAD_DIM
PAIRS = HEADS // 2
DECAY_LORA = 96
AAA_LORA = 96
GATE_LORA = 256
D_FF = 5632
RMS_EPS = 1e-6
GN_EPS = 64e-5
ATTN_SCALE = HEAD_DIM ** -0.5
LOG2E = 1.4426950408889634

LANES = 128
META_ROWS = 256
CHUNK = 64
MASKED_KEY = -1e30

C_R, C_K, C_V = 0, 1024, 2048
C_FQ, C_FK, C_FV = 3072, 4096, 5120
C_GA, C_GB = 6144, 8192
C_S = 10240
S_COLS = 512
FL_LANE = 96
N_PACK = C_S + S_COLS

VMEM_LIMIT = 56 * 1024 * 1024


def _sigmoid(x):
    return 0.5 * jnp.tanh(0.5 * x) + 0.5


def _softplus(x):
    return jnp.maximum(x, 0.0) + jnp.log(1.0 + jnp.exp(-jnp.abs(x)))


def _dot(a, b):
    return jnp.dot(a, b, preferred_element_type=F32)


def _dot_nt(a, b):
    return lax.dot_general(a, b, (((1,), (1,)), ((), ())), preferred_element_type=F32)


def _dot_tn(a, b):
    return lax.dot_general(a, b, (((0,), (0,)), ((), ())), preferred_element_type=F32)


def _row_halves(rows):
    half = rows // 2 if rows % 32 == 0 else rows
    return [slice(s, s + half) for s in range(0, rows, half)]


def _split3(x):
    hi = x.astype(BF16)
    r1 = x - hi.astype(F32)
    mid = r1.astype(BF16)
    lo = (r1 - mid.astype(F32)).astype(BF16)
    return hi, mid, lo


_NN = (((1,), (0,)), ((), ()))
_NT = (((1,), (1,)), ((), ()))
_TN = (((0,), (0,)), ((), ()))


def _head_ones():
    r = lax.broadcasted_iota(jnp.int32, (LANES, LANES), 0) // HEAD_DIM
    c = lax.broadcasted_iota(jnp.int32, (LANES, LANES), 1) // HEAD_DIM
    return jnp.where(r == c, 1.0, 0.0).astype(BF16)


def _head_sum(x, ones_bd):
    hi = x.astype(BF16)
    lo = (x - hi.astype(F32)).astype(BF16)
    return _dot(jnp.concatenate([hi, lo], axis=1), jnp.concatenate([ones_bd, ones_bd], axis=0))


def _rms_mm_kernel(x_ref, g_ref, w_ref, o_ref, xn_ref):
    @pl.when(pl.program_id(1) == 0)
    def _():
        x = x_ref[...]
        ms = jnp.mean(x * x, axis=-1, keepdims=True)
        xn_ref[...] = ((x * lax.rsqrt(ms + RMS_EPS)) * g_ref[...]).astype(BF16)

    o_ref[...] = _dot(xn_ref[...], w_ref[...]).astype(o_ref.dtype)


def _rms_glu_kernel(x_ref, g_ref, wg_ref, wu_ref, o_ref, xn_ref):
    @pl.when(pl.program_id(1) == 0)
    def _():
        x = x_ref[...]
        ms = jnp.mean(x * x, axis=-1, keepdims=True)
        xn_ref[...] = ((x * lax.rsqrt(ms + RMS_EPS)) * g_ref[...]).astype(BF16)

    for rows in _row_halves(o_ref.shape[0]):
        xn = xn_ref[rows, :]
        gate = _dot(xn, wg_ref[...])
        up = _dot(xn, wu_ref[...])
        o_ref[rows, :] = (gate * _sigmoid(gate) * up).astype(o_ref.dtype)


def _rms_matmul(x2d, g, w_bf16, *, tm, tn, out_dtype=F32):
    m, d = x2d.shape
    n = w_bf16.shape[1]
    return pl.pallas_call(
        _rms_mm_kernel,
        grid=(m // tm, n // tn),
        in_specs=[pl.BlockSpec((tm, d), lambda i, j: (i, 0)),
                  pl.BlockSpec((1, d), lambda i, j: (0, 0)),
                  pl.BlockSpec((d, tn), lambda i, j: (0, j))],
        out_specs=pl.BlockSpec((tm, tn), lambda i, j: (i, j)),
        out_shape=jax.ShapeDtypeStruct((m, n), out_dtype),
        scratch_shapes=[pltpu.VMEM((tm, d), BF16)],
        compiler_params=pltpu.CompilerParams(
            dimension_semantics=("parallel", "arbitrary"), vmem_limit_bytes=VMEM_LIMIT),
        name="rms_matmul",
    )(x2d, g.reshape(1, d), w_bf16)


def _rms_matmul_glu(x2d, g, w_bf16, *, tm, tn):
    m, d = x2d.shape
    n = w_bf16.shape[1] // 2
    up_off = n // tn
    return pl.pallas_call(
        _rms_glu_kernel,
        grid=(m // tm, n // tn),
        in_specs=[pl.BlockSpec((tm, d), lambda i, j: (i, 0)),
                  pl.BlockSpec((1, d), lambda i, j: (0, 0)),
                  pl.BlockSpec((d, tn), lambda i, j: (0, j)),
                  pl.BlockSpec((d, tn), lambda i, j: (0, j + up_off))],
        out_specs=pl.BlockSpec((tm, tn), lambda i, j: (i, j)),
        out_shape=jax.ShapeDtypeStruct((m, n), BF16),
        scratch_shapes=[pltpu.VMEM((tm, d), BF16)],
        compiler_params=pltpu.CompilerParams(
            dimension_semantics=("parallel", "arbitrary"), vmem_limit_bytes=VMEM_LIMIT),
        name="rms_matmul_glu",
    )(x2d, g.reshape(1, d), w_bf16, w_bf16)


def _mm_res_kernel(a_ref, b_ref, r_ref, o_ref):
    for rows in _row_halves(o_ref.shape[0]):
        o_ref[rows, :] = _dot(a_ref[rows, :], b_ref[...]) + r_ref[rows, :]


def _matmul_res(a, b, res, *, tm, tn):
    m, kk = a.shape
    n = b.shape[1]
    return pl.pallas_call(
        _mm_res_kernel,
        grid=(m // tm, n // tn),
        in_specs=[pl.BlockSpec((tm, kk), lambda i, j: (i, 0)),
                  pl.BlockSpec((kk, tn), lambda i, j: (0, j)),
                  pl.BlockSpec((tm, tn), lambda i, j: (i, j))],
        out_specs=pl.BlockSpec((tm, tn), lambda i, j: (i, j)),
        out_shape=jax.ShapeDtypeStruct((m, n), F32),
        compiler_params=pltpu.CompilerParams(
            dimension_semantics=("parallel", "arbitrary"), vmem_limit_bytes=VMEM_LIMIT),
        name="matmul_res",
    )(a, b, res)


def _merge_kernel(ya_ref, yb_ref, wa_ref, wb_ref, za_ref, zb_ref, o_ref):
    for rows in _row_halves(o_ref.shape[0]):
        ta = _dot(ya_ref[rows, :], wa_ref[...])
        tb = _dot(yb_ref[rows, :], wb_ref[...])
        o_ref[rows, :] = (_sigmoid(za_ref[rows, :]) * ta
                          + _sigmoid(zb_ref[rows, :]) * tb).astype(o_ref.dtype)


def _merge(ya, yb, wa, wb, proj2d, *, tm, tn):
    m, kk = ya.shape
    n = wa.shape[1]
    oa, ob = C_GA // tn, C_GB // tn
    return pl.pallas_call(
        _merge_kernel,
        grid=(m // tm, n // tn),
        in_specs=[pl.BlockSpec((tm, kk), lambda i, j: (i, 0)),
                  pl.BlockSpec((tm, kk), lambda i, j: (i, 0)),
                  pl.BlockSpec((kk, tn), lambda i, j: (0, j)),
                  pl.BlockSpec((kk, tn), lambda i, j: (0, j)),
                  pl.BlockSpec((tm, tn), lambda i, j: (i, j + oa)),
                  pl.BlockSpec((tm, tn), lambda i, j: (i, j + ob))],
        out_specs=pl.BlockSpec((tm, tn), lambda i, j: (i, j)),
        out_shape=jax.ShapeDtypeStruct((m, n), BF16),
        compiler_params=pltpu.CompilerParams(
            dimension_semantics=("parallel", "arbitrary"), vmem_limit_bytes=VMEM_LIMIT),
        name="merge",
    )(ya, yb, wa, wb, proj2d, proj2d)


def _rwkv_prep_kernel(z_ref, zs_ref, pz_ref, ps_ref, p0z_ref, p0s_ref, mu_ref, mus_ref,
                      w0_ref, w2_ref, a0_ref, a2_ref, g2_ref, kk_ref, ka_ref,
                      r_out, lw_out, k_out, v_out, a_out, b_out, g_out, *, tt):
    first = pl.program_id(1) == 0
    z = z_ref[0]
    zs = zs_ref[0]
    pz = jnp.where(first, p0z_ref[7:8, :], pz_ref[0, 7:8, :])
    ps = jnp.where(first, p0s_ref[7:8, :], ps_ref[0, 7:8, :])
    row = lax.broadcasted_iota(jnp.int32, (tt, 1), 0)

    def lerp(cur, prev_last, mu):
        prev = jnp.where(row == 0, prev_last, pltpu.roll(cur, 1, axis=0))
        return cur + (prev - cur) * mu

    zm = lerp(z, pz, mu_ref[...])
    zsm = lerp(zs, ps, mus_ref[...])
    r = zm[:, C_R:C_R + WIDTH]
    k = zm[:, C_K:C_K + WIDTH]
    v = zm[:, C_V:C_V + WIDTH]
    wd = zsm[:, 0:LANES]
    ad = zsm[:, LANES:2 * LANES]
    gd = zsm[:, 2 * LANES:4 * LANES]

    w_log = -_softplus(-(w0_ref[...] + _dot(jnp.tanh(wd).astype(BF16), w2_ref[...]))) - 0.5
    lw_out[0] = -jnp.exp(w_log)
    a_sig = _sigmoid(a0_ref[...] + _dot(ad.astype(BF16), a2_ref[...]))
    g_out[0] = _dot(_sigmoid(gd).astype(BF16), g2_ref[...]).astype(g_out.dtype)

    ones_bd = _head_ones()
    kk = k * kk_ref[...]
    kk2 = kk * kk
    nrm = jnp.concatenate(
        [_head_sum(kk2[:, p * LANES:(p + 1) * LANES], ones_bd) for p in range(PAIRS)], axis=1)
    kk = kk * lax.rsqrt(jnp.maximum(nrm, 1e-24))
    r_out[0] = r.astype(r_out.dtype)
    k_out[0] = (k * (1.0 + (a_sig - 1.0) * ka_ref[...])).astype(k_out.dtype)
    v_out[0] = v.astype(v_out.dtype)
    a_out[0] = (-kk).astype(a_out.dtype)
    b_out[0] = (kk * a_sig).astype(b_out.dtype)


def _rwkv_prep(proj3d, prev0_z, prev0_s, mu_z, mu_s, w0, w2p, a0, a2p, g2, k_k, k_a, *, tt):
    b, l, _ = proj3d.shape
    s_blk = C_S // S_COLS
    zw = 3 * WIDTH
    row = lambda a: a.reshape(1, -1)
    vec = lambda n: pl.BlockSpec((1, n), lambda bi, i: (0, 0))
    full = lambda r, c: pl.BlockSpec((r, c), lambda bi, i: (0, 0))
    out_spec = pl.BlockSpec((1, tt, WIDTH), lambda bi, i: (bi, i, 0))
    sds = lambda dt: jax.ShapeDtypeStruct((b, l, WIDTH), dt)
    return pl.pallas_call(
        functools.partial(_rwkv_prep_kernel, tt=tt),
        grid=(b, l // tt),
        in_specs=[pl.BlockSpec((1, tt, zw), lambda bi, i: (bi, i, 0)),
                  pl.BlockSpec((1, tt, S_COLS), lambda bi, i: (bi, i, s_blk)),
                  pl.BlockSpec((1, 8, zw), lambda bi, i: (bi, jnp.maximum(i * (tt // 8) - 1, 0), 0)),
                  pl.BlockSpec((1, 8, S_COLS),
                               lambda bi, i: (bi, jnp.maximum(i * (tt // 8) - 1, 0), s_blk)),
                  full(8, zw), full(8, S_COLS),
                  vec(zw), vec(S_COLS),
                  vec(WIDTH), full(LANES, WIDTH), vec(WIDTH), full(LANES, WIDTH),
                  full(GATE_LORA, WIDTH), vec(WIDTH), vec(WIDTH)],
        out_specs=[out_spec] * 7,
        out_shape=[sds(BF16), sds(F32)] + [sds(BF16)] * 5,
        compiler_params=pltpu.CompilerParams(
            dimension_semantics=("parallel", "arbitrary"), vmem_limit_bytes=VMEM_LIMIT),
        name="rwkv_prep",
    )(proj3d, proj3d, proj3d, proj3d, prev0_z, prev0_s, row(mu_z), row(mu_s),
      row(w0), w2p, row(a0), a2p, g2, row(k_k), row(k_a))


def _bf(x):
    return x.astype(BF16)


WKV_CHUNKS = 4


def _wkv_group(a_t, r_t, b_t, k_t, b_h, k_h, v, s, w_total, masks):
    head0, strict, lower, eye = masks
    dg = functools.partial(lax.dot_general, preferred_element_type=F32)
    n = range(len(s))

    def bd(x):
        return jnp.concatenate([jnp.where(head0, x, 0.0), jnp.where(head0, 0.0, x)], axis=0)

    lhs = [_bf(jnp.concatenate([bd(a_t[i]), bd(r_t[i])], axis=0)) for i in n]
    rhs = [_bf(jnp.concatenate([bd(b_t[i]), bd(k_t[i])], axis=0)) for i in n]
    v_bd = [_bf(bd(v[i])) for i in n]
    gram = [dg(lhs[i], rhs[i], _NT) for i in n]
    a_s = [dg(lhs[i], _bf(s[i]), _NT) for i in n]
    n_ab = [jnp.where(strict, g[0:LANES, 0:LANES], 0.0) for g in gram]
    a_ak = [jnp.where(strict, g[0:LANES, LANES:2 * LANES], 0.0) for g in gram]
    a_r = [_bf(jnp.concatenate([jnp.where(lower, g[LANES:2 * LANES, 0:LANES], 0.0),
                                jnp.where(lower, g[LANES:2 * LANES, LANES:2 * LANES], 0.0)], axis=1))
           for g in gram]
    rhs_u = [a_s[i][0:LANES] + dg(_bf(a_ak[i]), v_bd[i], _NN) for i in n]

    p_inv = [eye + x for x in n_ab]
    q = [_bf(x) for x in n_ab]
    for _ in range(5):
        q = [_bf(dg(x, x, _NN)) for x in q]
        p_inv = [p_inv[i] + dg(q[i], _bf(p_inv[i]), _NN) for i in n]
    u = [dg(_bf(p_inv[i]), _bf(rhs_u[i]), _NN) for i in n]

    uv = [jnp.concatenate([_bf(u[i]), v_bd[i]], axis=0) for i in n]
    y_bd = [a_s[i][LANES:2 * LANES] + dg(a_r[i], uv[i], _NN) for i in n]
    y = [x[0:CHUNK] + x[CHUNK:2 * CHUNK] for x in y_bd]
    s_new = [w_total[i] * s[i]
             + dg(uv[i], _bf(jnp.concatenate([bd(b_h[i]), bd(k_h[i])], axis=0)), _TN) for i in n]
    return y, s_new


def _wkv_kernel(r_ref, lw_ref, k_ref, v_ref, a_ref, b_ref, g_ref, s0_ref,
                gnw_ref, gnb_ref, rk_ref, y_ref, sfin_ref, s_scr, *, nc):
    c = pl.program_id(1)

    @pl.when(c == 0)
    def _():
        s_scr[...] = s0_ref[0]

    rows_all = WKV_CHUNKS * CHUNK
    r = r_ref[0].astype(F32)
    lw = lw_ref[0]
    k = k_ref[0].astype(F32)
    v = v_ref[0].astype(F32)
    a = a_ref[0].astype(F32)
    b = b_ref[0].astype(F32)

    ti = lax.broadcasted_iota(jnp.int32, (rows_all, rows_all), 0)
    tj = lax.broadcasted_iota(jnp.int32, (rows_all, rows_all), 1)
    ltri = jnp.where((tj <= ti) & (ti // CHUNK == tj // CHUNK), 1.0, 0.0).astype(BF16)
    hi, mid, lo = _split3(lw)
    cum = _dot(jnp.concatenate([ltri, ltri, ltri], axis=1),
               jnp.concatenate([hi, mid, lo], axis=0))

    lane = lax.broadcasted_iota(jnp.int32, (1, LANES), 1)
    bi = lax.broadcasted_iota(jnp.int32, (LANES, LANES), 0)
    bj = lax.broadcasted_iota(jnp.int32, (LANES, LANES), 1)
    same = (bi // CHUNK) == (bj // CHUNK)
    masks = (lane < HEAD_DIM,
             same & ((bj % CHUNK) < (bi % CHUNK)),
             same & ((bj % CHUNK) <= (bi % CHUNK)),
             jnp.where(bi == bj, 1.0, 0.0))

    ones_bd = _head_ones()
    inv_n = 1.0 / HEAD_DIM
    sls = [slice(p * LANES, (p + 1) * LANES) for p in range(PAIRS)]
    cut = lambda x: [x[:, sl] for sl in sls]
    state = [s_scr[p] for p in range(PAIRS)]
    for sc in range(WKV_CHUNKS):
        rows = slice(sc * CHUNK, (sc + 1) * CHUNK)
        cum_c, lw_c = cum[rows], lw[rows]
        total = cum_c[CHUNK - 1:CHUNK, :]
        e_inv = jnp.exp(-cum_c)
        e_end = jnp.exp(total - cum_c)
        ys, state = _wkv_group(
            cut(a[rows] * jnp.exp(cum_c - lw_c)), cut(r[rows] * jnp.exp(cum_c)),
            cut(b[rows] * e_inv), cut(k[rows] * e_inv), cut(b[rows] * e_end), cut(k[rows] * e_end),
            cut(v[rows]), state, cut(jnp.exp(total)), masks)

        y_all = jnp.concatenate(ys, axis=0)
        rk_all = jnp.concatenate([r[rows, sl] * k[rows, sl] * rk_ref[:, sl] for sl in sls], axis=0)
        bonus_all = _head_sum(rk_all, ones_bd)
        d_all = y_all - _head_sum(y_all, ones_bd) * inv_n
        rstd_all = lax.rsqrt(_head_sum(d_all * d_all, ones_bd) * inv_n + GN_EPS)
        for i, sl in enumerate(sls):
            prow = slice(i * CHUNK, (i + 1) * CHUNK)
            yn = d_all[prow] * rstd_all[prow] * gnw_ref[:, sl] + gnb_ref[:, sl]
            y_ref[0, rows, sl] = ((yn + bonus_all[prow] * v[rows, sl])
                                  * g_ref[0, rows, sl].astype(F32)).astype(y_ref.dtype)

    for p in range(PAIRS):
        s_scr[p] = state[p]

        @pl.when(c == nc - 1)
        def _(p=p):
            sfin_ref[0, p] = state[p]


def _wkv(r, lw, k, v, a, b, g, s0, gn_w, gn_b, r_k):
    bsz, l, _ = r.shape
    nc = l // (WKV_CHUNKS * CHUNK)
    tok = pl.BlockSpec((1, WKV_CHUNKS * CHUNK, WIDTH), lambda bi, c: (bi, c, 0))
    vec = pl.BlockSpec((1, WIDTH), lambda bi, c: (0, 0))
    st_in = pl.BlockSpec((1, PAIRS, LANES, LANES), lambda bi, c: (0, 0, 0, 0))
    st_out = pl.BlockSpec((1, PAIRS, LANES, LANES), lambda bi, c: (bi, 0, 0, 0))
    row = lambda x: x.reshape(1, WIDTH)
    return pl.pallas_call(
        functools.partial(_wkv_kernel, nc=nc),
        grid=(bsz, nc),
        in_specs=[tok] * 7 + [st_in, vec, vec, vec],
        out_specs=[tok, st_out],
        out_shape=[jax.ShapeDtypeStruct((bsz, l, WIDTH), BF16),
                   jax.ShapeDtypeStruct((bsz, PAIRS, LANES, LANES), F32)],
        scratch_shapes=[pltpu.VMEM((PAIRS, LANES, LANES), F32)],
        compiler_params=pltpu.CompilerParams(
            dimension_semantics=("parallel", "arbitrary"), vmem_limit_bytes=VMEM_LIMIT),
        name="wkv",
    )(r, lw, k, v, a, b, g, s0, row(gn_w), row(gn_b), row(r_k))


def _fox_prep_kernel(z_ref, zs_ref, c0_ref, qg_ref, kg_ref, fb_ref,
                     qt_out, k_out, vt_out, c_out, carry, *, tt, n_pad):
    i = pl.program_id(1)

    @pl.when(i == 0)
    def _():
        carry[...] = c0_ref[...]

    z = z_ref[0]
    fl = zs_ref[0][:, 0:LANES]
    log_f = -_softplus(-(fl + fb_ref[...]))
    ti = lax.broadcasted_iota(jnp.int32, (tt, tt), 0)
    tj = lax.broadcasted_iota(jnp.int32, (tt, tt), 1)
    ltri = jnp.where(tj <= ti, 1.0, 0.0).astype(BF16)
    hi, mid, lo = _split3(log_f)
    cum = carry[...] + _dot(jnp.concatenate([ltri, ltri, ltri], axis=1),
                            jnp.concatenate([hi, mid, lo], axis=0))
    carry[...] = cum[tt - 1:tt, :]
    c_out[0] = cum[tt - 8:tt, :]
    c_hi, c_mid, c_lo = (t.astype(F32) for t in _split3(cum * LOG2E))
    neg_hi, neg_mid, neg_lo = -c_hi, -c_mid, -c_lo
    if n_pad:
        grow = i * tt + lax.broadcasted_iota(jnp.int32, (tt, 1), 0)
        neg_hi = jnp.where(grow < n_pad, MASKED_KEY, neg_hi)

    for p in range(PAIRS):
        vt_out[0, p, 0] = z[:, 2 * WIDTH + p * LANES:2 * WIDTH + (p + 1) * LANES].T.astype(BF16)
    ones_bd = _head_ones()
    lane = lax.broadcasted_iota(jnp.int32, (1, LANES), 1)
    inv_n = 1.0 / HEAD_DIM
    for p in range(PAIRS):
        zq = z[:, p * LANES:(p + 1) * LANES]
        zk = z[:, WIDTH + p * LANES:WIDTH + (p + 1) * LANES]
        qn = zq * lax.rsqrt(_head_sum(zq * zq, ones_bd) * inv_n + RMS_EPS) * (qg_ref[...] * (ATTN_SCALE * LOG2E))
        kn = zk * lax.rsqrt(_head_sum(zk * zk, ones_bd) * inv_n + RMS_EPS) * kg_ref[...]
        for j in range(2):
            h = 2 * p + j
            own = (lane < HEAD_DIM) if j == 0 else (lane >= HEAD_DIM)
            e = lane - (HEAD_DIM if j == 0 else 0)
            col = lambda t: t[:, FL_LANE + h:FL_LANE + h + 1]
            q_extra = jnp.where((e >= 0) & (e < 3), 1.0, 0.0)
            k_extra = jnp.where(e == 0, col(neg_hi), jnp.where(e == 1, col(neg_mid), jnp.where(
                e == 2, col(neg_lo), 0.0)))
            qt_out[0, h, 0] = jnp.where(own, qn, q_extra).T.astype(BF16)
            k_out[0, h] = jnp.where(own, kn, k_extra).astype(BF16)


def _fox_prep(proj3d, c0, q_g, k_g, f_bias, *, tt, n_pad):
    b, l, _ = proj3d.shape
    zw = 3 * WIDTH
    tile2 = lambda g: jnp.concatenate([g, g]).reshape(1, LANES)
    fb = jnp.concatenate([jnp.zeros((FL_LANE,), F32), f_bias,
                          jnp.zeros((LANES - FL_LANE - HEADS,), F32)]).reshape(1, LANES)
    hd = pl.BlockSpec((1, HEADS, tt, LANES), lambda bi, i: (bi, 0, i, 0))
    vec = pl.BlockSpec((1, LANES), lambda bi, i: (0, 0))
    return pl.pallas_call(
        functools.partial(_fox_prep_kernel, tt=tt, n_pad=n_pad),
        grid=(b, l // tt),
        in_specs=[pl.BlockSpec((1, tt, zw), lambda bi, i: (bi, i, C_FQ // zw)),
                  pl.BlockSpec((1, tt, S_COLS), lambda bi, i: (bi, i, C_S // S_COLS)),
                  vec, vec, vec, vec],
        out_specs=[pl.BlockSpec((1, HEADS, 1, LANES, tt), lambda bi, i: (bi, 0, i, 0, 0)),
                   hd,
                   pl.BlockSpec((1, PAIRS, 1, LANES, tt), lambda bi, i: (bi, 0, i, 0, 0)),
                   pl.BlockSpec((1, 8, LANES), lambda bi, i: (bi, i, 0))],
        out_shape=[jax.ShapeDtypeStruct((b, HEADS, l // tt, LANES, tt), BF16),
                   jax.ShapeDtypeStruct((b, HEADS, l, LANES), BF16),
                   jax.ShapeDtypeStruct((b, PAIRS, l // tt, LANES, tt), BF16),
                   jax.ShapeDtypeStruct((b, (l // tt) * 8, LANES), F32)],
        scratch_shapes=[pltpu.VMEM((1, LANES), F32)],
        compiler_params=pltpu.CompilerParams(
            dimension_semantics=("parallel", "arbitrary"), vmem_limit_bytes=VMEM_LIMIT),
        name="fox_prep",
    )(proj3d, proj3d, c0, tile2(q_g), tile2(k_g), fb)


def _attn_kernel(qt_ref, k_ref, vt_ref, km_ref, o_ref, s_buf, p_buf, acc_buf, *, tk):
    qi = pl.program_id(2)
    key_i = lax.broadcasted_iota(jnp.int32, (tk, tk), 0)
    qry_i = lax.broadcasted_iota(jnp.int32, (tk, tk), 1)
    causal = key_i <= qry_i
    every = (0, 1, 2, 3)
    half1 = (1, 3)
    qts = [qt_ref[0, n // 2, n % 2] for n in every]

    def real_keys(r):
        off = pl.multiple_of(r * tk, tk)
        return (k_ref[0, 0, pl.ds(off, tk), :], k_ref[0, 1, pl.ds(off, tk), :])

    def stage_a(kblks, slot, which):
        for n in which:
            s_buf[slot, n] = _dot(kblks[n // 2], qts[n])

    def stage_b(slot, stats, alpha, masks):
        which = tuple(masks)
        s = {n: s_buf[slot, n] for n in which}
        for n in which:
            if masks[n] is not None:
                s[n] = jnp.where(masks[n], s[n], MASKED_KEY)
        m_new = {n: jnp.maximum(stats[n][0], jnp.max(s[n], axis=0, keepdims=True)) for n in which}
        a_new = {n: jnp.exp2(stats[n][0] - m_new[n]) for n in which}
        p = {n: jnp.exp2(s[n] - m_new[n]) for n in which}
        for n in which:
            p_buf[slot, n] = p[n].astype(BF16)
        l_new = {n: a_new[n] * stats[n][1] + jnp.sum(p[n], axis=0, keepdims=True) for n in which}
        stats = tuple((m_new[n], l_new[n]) if n in which else stats[n] for n in every)
        alpha = tuple(a_new[n] if n in which else alpha[n] for n in every)
        return stats, alpha

    def stage_c(slot, e, alpha, which):
        vt = vt_ref[0, 0, e]
        pv = {n: _dot(vt[(n // 2) * HEAD_DIM:(n // 2 + 1) * HEAD_DIM, :], p_buf[slot, n])
              for n in which}
        for n in which:
            acc_buf[n] = alpha[n] * acc_buf[n] + pv[n]

    none4 = {n: None for n in every}
    stats = ((jnp.full((1, tk), MASKED_KEY, F32), jnp.zeros((1, tk), F32)),) * 4
    alpha = (jnp.ones((1, tk), F32),) * 4
    acc_buf[...] = jnp.zeros_like(acc_buf)
    p_buf[1] = jnp.zeros(p_buf.shape[1:], BF16)
    stage_a((km_ref[0, 0], km_ref[0, 1]), 0, every)

    def body(i, state):
        stats, alpha_prev = state
        stage_a(real_keys(2 * i), 1, every)
        stats, alpha_a = stage_b(0, stats, alpha_prev, none4)
        stage_c(1, jnp.maximum(2 * i - 1, 0), alpha_prev, every)
        stage_a(real_keys(2 * i + 1), 0, every)
        stats, alpha_b = stage_b(1, stats, alpha_a, none4)
        stage_c(0, 2 * i, alpha_a, every)
        return stats, alpha_b

    stats, alpha_prev = lax.fori_loop(0, qi, body, (stats, alpha))
    e0 = 2 * qi
    stage_a(real_keys(e0), 1, every)
    stats, alpha_a = stage_b(0, stats, alpha_prev, none4)
    stage_c(1, jnp.maximum(e0 - 1, 0), alpha_prev, every)
    stage_a(real_keys(e0 + 1), 0, half1)
    stats, alpha_b = stage_b(1, stats, alpha_a, {0: causal, 1: None, 2: causal, 3: None})
    stage_c(0, e0, alpha_a, every)
    stats, alpha_c = stage_b(0, stats, alpha_b, {1: causal, 3: causal})
    stage_c(1, e0 + 1, alpha_b, every)
    stage_c(0, e0 + 2, alpha_c, half1)
    for u in range(2):
        out_t = jnp.concatenate([acc_buf[u] / stats[u][1], acc_buf[2 + u] / stats[2 + u][1]], axis=0)
        o_ref[0, u * tk:(u + 1) * tk, :] = out_t.T.astype(o_ref.dtype)


def _attn(qt, kp, vt_all, kp_meta, *, tk):
    b, _, l, _ = kp.shape
    nk = vt_all.shape[2]
    assert vt_all.shape[4] == tk and qt.shape[4] == tk and kp_meta.shape[2] == tk
    return pl.pallas_call(
        functools.partial(_attn_kernel, tk=tk),
        grid=(b, PAIRS, l // (2 * tk)),
        in_specs=[pl.BlockSpec((1, 2, 2, LANES, tk), lambda bi, p, qi: (bi, p, qi, 0, 0)),
                  pl.BlockSpec((1, 2, l, LANES), lambda bi, p, qi: (bi, p, 0, 0)),
                  pl.BlockSpec((1, 1, nk, LANES, tk), lambda bi, p, qi: (bi, p, 0, 0, 0)),
                  pl.BlockSpec((1, 2, tk, LANES), lambda bi, p, qi: (0, p, 0, 0))],
        out_specs=pl.BlockSpec((1, 2 * tk, LANES), lambda bi, p, qi: (bi, qi, p)),
        out_shape=jax.ShapeDtypeStruct((b, l, WIDTH), BF16),
        scratch_shapes=[pltpu.VMEM((2, 4, tk, tk), F32), pltpu.VMEM((2, 4, tk, tk), BF16),
                        pltpu.VMEM((4, HEAD_DIM, tk), F32)],
        compiler_params=pltpu.CompilerParams(
            dimension_semantics=("parallel", "parallel", "arbitrary"), vmem_limit_bytes=VMEM_LIMIT),
        name="attn",
    )(qt, kp, vt_all, kp_meta)


def _pick(n, prefs):
    for t in prefs:
        if n % t == 0:
            return t
    raise ValueError(f"no tile for {n}")


def _pack_w_in(w_in):
    rw, fx = 3 * WIDTH + DECAY_LORA + AAA_LORA + GATE_LORA, 3 * WIDTH + HEADS
    wd0, ad0, gd0 = 3 * WIDTH, 3 * WIDTH + DECAY_LORA, 3 * WIDTH + DECAY_LORA + AAA_LORA
    z = lambda n: jnp.zeros((w_in.shape[0], n), w_in.dtype)
    return jnp.concatenate([
        w_in[:, 0:3 * WIDTH], w_in[:, rw:rw + 3 * WIDTH], w_in[:, rw + fx:],
        w_in[:, wd0:ad0], w_in[:, rw + 3 * WIDTH:rw + fx], z(LANES - DECAY_LORA - HEADS),
        w_in[:, ad0:gd0], z(LANES - AAA_LORA), w_in[:, gd0:rw]], axis=1).astype(BF16)


def _pack_mu(mu):
    wd0, ad0, gd0 = 3 * WIDTH, 3 * WIDTH + DECAY_LORA, 3 * WIDTH + DECAY_LORA + AAA_LORA
    z = lambda n: jnp.zeros((n,), mu.dtype)
    mu_s = jnp.concatenate([mu[wd0:ad0], z(LANES - DECAY_LORA), mu[ad0:gd0], z(LANES - AAA_LORA),
                            mu[gd0:]])
    return mu[:3 * WIDTH], mu_s


def _pad_rows(w, rows):
    return jnp.concatenate([w, jnp.zeros((rows - w.shape[0], w.shape[1]), w.dtype)], axis=0)


def _mixer_inputs(h2d, bsz, norm_g, w_pack, tm, tn):
    proj = _rms_matmul(h2d, norm_g, w_pack, tm=tm, tn=tn)
    return proj, proj.reshape(bsz, h2d.shape[0] // bsz, N_PACK)


def kernel(x, meta_tokens, norm1_g, w_in, rwkv_mu, rwkv_w0, rwkv_w2, rwkv_a0, rwkv_a2, rwkv_g2,
           rwkv_k_k, rwkv_k_a, rwkv_r_k, rwkv_gn_w, rwkv_gn_b, fox_q_norm_g, fox_k_norm_g,
           fox_f_bias, w_branch_a, w_branch_b, w_o, norm2_g, w_gate_up, w_down):
    bsz, seq, d = x.shape
    assert d == D_MODEL and norm1_g.shape[0] == 1 and seq % 512 == 0
    m = bsz * seq
    x2d = x.reshape(m, d)
    meta2d = jnp.concatenate(
        [jnp.zeros((META_ROWS - N_META, d), x.dtype), meta_tokens.astype(x.dtype)], axis=0)

    w_pack = _pack_w_in(w_in[0].astype(BF16))
    mu_z, mu_s = _pack_mu(rwkv_mu[0])
    w2p = _pad_rows(rwkv_w2[0], LANES).astype(BF16)
    a2p = _pad_rows(rwkv_a2[0], LANES).astype(BF16)
    g2 = rwkv_g2[0].astype(BF16)
    r_k = rwkv_r_k[0].reshape(WIDTH)

    tm = _pick(m, (1024, 512, 256))
    tt = _pick(seq, (256,))

    _, pm3 = _mixer_inputs(meta2d, 1, norm1_g[0], w_pack, META_ROWS, tn=512)
    zero_z = jnp.zeros((8, 3 * WIDTH), F32)
    zero_s = jnp.zeros((8, S_COLS), F32)
    prep = functools.partial(_rwkv_prep, mu_z=mu_z, mu_s=mu_s, w0=rwkv_w0[0], w2p=w2p,
                             a0=rwkv_a0[0], a2p=a2p, g2=g2, k_k=rwkv_k_k[0], k_a=rwkv_k_a[0])
    wkv = functools.partial(_wkv, gn_w=rwkv_gn_w[0], gn_b=rwkv_gn_b[0], r_k=r_k)
    fox = functools.partial(_fox_prep, q_g=fox_q_norm_g[0], k_g=fox_k_norm_g[0],
                            f_bias=fox_f_bias[0])
    mr = prep(pm3, zero_z, zero_s, tt=META_ROWS)
    _, s_meta = wkv(*mr, jnp.zeros((1, PAIRS, LANES, LANES), F32))
    _, kp_meta, vt_meta, c_meta = fox(pm3, jnp.zeros((1, LANES), F32), tt=META_ROWS,
                                      n_pad=META_ROWS - N_META)

    proj2d, p3 = _mixer_inputs(x2d, bsz, norm1_g[0], w_pack, tm, tn=1536)
    rr = prep(p3, pm3[0, META_ROWS - 8:, 0:3 * WIDTH], pm3[0, META_ROWS - 8:, C_S:], tt=tt)
    ya, _ = wkv(*rr, s_meta)
    qt, kp, vt, _ = fox(p3, c_meta[0, 7:8, :], tt=tt, n_pad=0)
    vt_all = jnp.concatenate([jnp.broadcast_to(vt_meta, (bsz,) + vt_meta.shape[1:]), vt], axis=2)
    yb = _attn(qt, kp, vt_all, kp_meta, tk=tt)

    merged = _merge(ya.reshape(m, WIDTH), yb.reshape(m, WIDTH), w_branch_a[0].astype(BF16),
                    w_branch_b[0].astype(BF16), proj2d, tm=tm, tn=1024)
    h1 = _matmul_res(merged, w_o[0].astype(BF16), x2d, tm=tm, tn=1024)
    act = _rms_matmul_glu(h1, norm2_g[0], w_gate_up[0].astype(BF16), tm=tm, tn=512)
    out = _matmul_res(act, w_down[0].astype(BF16), h1, tm=tm, tn=512)
    return out.reshape(bsz, seq, d)
```

```python
import functools

import jax
import jax.numpy as jnp
from jax import lax
from jax.experimental import pallas as pl
from jax.experimental.pallas import tpu as pltpu

F32 = jnp.float32
BF16 = jnp.bfloat16

D_MODEL = 2048
N_META = 16
HEAD_DIM = 64
HEADS = 16
WIDTH = HEADS * HEAD_DIM
PAIRS = HEADS // 2
DECAY_LORA = 96
AAA_LORA = 96
GATE_LORA = 256
D_FF = 5632
RMS_EPS = 1e-6
GN_EPS = 64e-5
ATTN_SCALE = HEAD_DIM ** -0.5
LOG2E = 1.4426950408889634

LANES = 128
META_ROWS = 256
CHUNK = 64
MASKED_KEY = -1e30

C_R, C_K, C_V = 0, 1024, 2048
C_FQ, C_FK, C_FV = 3072, 4096, 5120
C_GA, C_GB = 6144, 8192
C_S = 10240
S_COLS = 512
FL_LANE = 96
N_PACK = C_S + S_COLS

VMEM_LIMIT = 56 * 1024 * 1024


def _sigmoid(x):
    return 0.5 * jnp.tanh(0.5 * x) + 0.5


def _softplus(x):
    return jnp.maximum(x, 0.0) + jnp.log(1.0 + jnp.exp(-jnp.abs(x)))


def _dot(a, b):
    return jnp.dot(a, b, preferred_element_type=F32)


def _dot_nt(a, b):
    return lax.dot_general(a, b, (((1,), (1,)), ((), ())), preferred_element_type=F32)


def _dot_tn(a, b):
    return lax.dot_general(a, b, (((0,), (0,)), ((), ())), preferred_element_type=F32)


def _split3(x):
    hi = x.astype(BF16)
    r1 = x - hi.astype(F32)
    mid = r1.astype(BF16)
    lo = (r1 - mid.astype(F32)).astype(BF16)
    return hi, mid, lo


_NN = (((1,), (0,)), ((), ()))
_NT = (((1,), (1,)), ((), ()))
_TN = (((0,), (0,)), ((), ()))


def _head_ones():
    r = lax.broadcasted_iota(jnp.int32, (LANES, LANES), 0) // HEAD_DIM
    c = lax.broadcasted_iota(jnp.int32, (LANES, LANES), 1) // HEAD_DIM
    return jnp.where(r == c, 1.0, 0.0).astype(BF16)


def _head_sum(x, ones_bd):
    hi = x.astype(BF16)
    lo = (x - hi.astype(F32)).astype(BF16)
    return _dot(jnp.concatenate([hi, lo], axis=1), jnp.concatenate([ones_bd, ones_bd], axis=0))


def _rms_mm_kernel(x_ref, g_ref, w_ref, o_ref, xn_ref):
    @pl.when(pl.program_id(1) == 0)
    def _():
        x = x_ref[...]
        ms = jnp.mean(x * x, axis=-1, keepdims=True)
        xn_ref[...] = ((x * lax.rsqrt(ms + RMS_EPS)) * g_ref[...]).astype(BF16)

    o_ref[...] = _dot(xn_ref[...], w_ref[...]).astype(o_ref.dtype)


def _rms_glu_kernel(x_ref, g_ref, wg_ref, wu_ref, o_ref, xn_ref):
    @pl.when(pl.program_id(1) == 0)
    def _():
        x = x_ref[...]
        ms = jnp.mean(x * x, axis=-1, keepdims=True)
        xn_ref[...] = ((x * lax.rsqrt(ms + RMS_EPS)) * g_ref[...]).astype(BF16)

    xn = xn_ref[...]
    gate = _dot(xn, wg_ref[...])
    up = _dot(xn, wu_ref[...])
    o_ref[...] = (gate * _sigmoid(gate) * up).astype(o_ref.dtype)


def _rms_matmul(x2d, g, w_bf16, *, tm, tn, out_dtype=F32):
    m, d = x2d.shape
    n = w_bf16.shape[1]
    return pl.pallas_call(
        _rms_mm_kernel,
        grid=(m // tm, n // tn),
        in_specs=[pl.BlockSpec((tm, d), lambda i, j: (i, 0)),
                  pl.BlockSpec((1, d), lambda i, j: (0, 0)),
                  pl.BlockSpec((d, tn), lambda i, j: (0, j))],
        out_specs=pl.BlockSpec((tm, tn), lambda i, j: (i, j)),
        out_shape=jax.ShapeDtypeStruct((m, n), out_dtype),
        scratch_shapes=[pltpu.VMEM((tm, d), BF16)],
        compiler_params=pltpu.CompilerParams(
            dimension_semantics=("parallel", "arbitrary"), vmem_limit_bytes=VMEM_LIMIT),
        name="rms_matmul",
    )(x2d, g.reshape(1, d), w_bf16)


def _rms_matmul_glu(x2d, g, w_bf16, *, tm, tn):
    m, d = x2d.shape
    n = w_bf16.shape[1] // 2
    up_off = n // tn
    return pl.pallas_call(
        _rms_glu_kernel,
        grid=(m // tm, n // tn),
        in_specs=[pl.BlockSpec((tm, d), lambda i, j: (i, 0)),
                  pl.BlockSpec((1, d), lambda i, j: (0, 0)),
                  pl.BlockSpec((d, tn), lambda i, j: (0, j)),
                  pl.BlockSpec((d, tn), lambda i, j: (0, j + up_off))],
        out_specs=pl.BlockSpec((tm, tn), lambda i, j: (i, j)),
        out_shape=jax.ShapeDtypeStruct((m, n), BF16),
        scratch_shapes=[pltpu.VMEM((tm, d), BF16)],
        compiler_params=pltpu.CompilerParams(
            dimension_semantics=("parallel", "arbitrary"), vmem_limit_bytes=VMEM_LIMIT),
        name="rms_matmul_glu",
    )(x2d, g.reshape(1, d), w_bf16, w_bf16)


def _mm_res_kernel(a_ref, b_ref, r_ref, o_ref):
    o_ref[...] = _dot(a_ref[...], b_ref[...]) + r_ref[...]


def _matmul_res(a, b, res, *, tm, tn):
    m, kk = a.shape
    n = b.shape[1]
    return pl.pallas_call(
        _mm_res_kernel,
        grid=(m // tm, n // tn),
        in_specs=[pl.BlockSpec((tm, kk), lambda i, j: (i, 0)),
                  pl.BlockSpec((kk, tn), lambda i, j: (0, j)),
                  pl.BlockSpec((tm, tn), lambda i, j: (i, j))],
        out_specs=pl.BlockSpec((tm, tn), lambda i, j: (i, j)),
        out_shape=jax.ShapeDtypeStruct((m, n), F32),
        compiler_params=pltpu.CompilerParams(
            dimension_semantics=("parallel", "arbitrary"), vmem_limit_bytes=VMEM_LIMIT),
        name="matmul_res",
    )(a, b, res)


def _merge_kernel(ya_ref, yb_ref, wa_ref, wb_ref, za_ref, zb_ref, o_ref):
    ta = _dot(ya_ref[...], wa_ref[...])
    tb = _dot(yb_ref[...], wb_ref[...])
    o_ref[...] = (_sigmoid(za_ref[...]) * ta + _sigmoid(zb_ref[...]) * tb).astype(o_ref.dtype)


def _merge(ya, yb, wa, wb, proj2d, *, tm, tn):
    m, kk = ya.shape
    n = wa.shape[1]
    oa, ob = C_GA // tn, C_GB // tn
    return pl.pallas_call(
        _merge_kernel,
        grid=(m // tm, n // tn),
        in_specs=[pl.BlockSpec((tm, kk), lambda i, j: (i, 0)),
                  pl.BlockSpec((tm, kk), lambda i, j: (i, 0)),
                  pl.BlockSpec((kk, tn), lambda i, j: (0, j)),
                  pl.BlockSpec((kk, tn), lambda i, j: (0, j)),
                  pl.BlockSpec((tm, tn), lambda i, j: (i, j + oa)),
                  pl.BlockSpec((tm, tn), lambda i, j: (i, j + ob))],
        out_specs=pl.BlockSpec((tm, tn), lambda i, j: (i, j)),
        out_shape=jax.ShapeDtypeStruct((m, n), BF16),
        compiler_params=pltpu.CompilerParams(
            dimension_semantics=("parallel", "arbitrary"), vmem_limit_bytes=VMEM_LIMIT),
        name="merge",
    )(ya, yb, wa, wb, proj2d, proj2d)


def _rwkv_prep_kernel(z_ref, zs_ref, pz_ref, ps_ref, p0z_ref, p0s_ref, mu_ref, mus_ref,
                      w0_ref, w2_ref, a0_ref, a2_ref, g2_ref, kk_ref, ka_ref,
                      r_out, lw_out, k_out, v_out, a_out, b_out, g_out, *, tt):
    first = pl.program_id(1) == 0
    z = z_ref[0]
    zs = zs_ref[0]
    pz = jnp.where(first, p0z_ref[7:8, :], pz_ref[0, 7:8, :])
    ps = jnp.where(first, p0s_ref[7:8, :], ps_ref[0, 7:8, :])
    row = lax.broadcasted_iota(jnp.int32, (tt, 1), 0)

    def lerp(cur, prev_last, mu):
        prev = jnp.where(row == 0, prev_last, pltpu.roll(cur, 1, axis=0))
        return cur + (prev - cur) * mu

    zm = lerp(z, pz, mu_ref[...])
    zsm = lerp(zs, ps, mus_ref[...])
    r = zm[:, C_R:C_R + WIDTH]
    k = zm[:, C_K:C_K + WIDTH]
    v = zm[:, C_V:C_V + WIDTH]
    wd = zsm[:, 0:LANES]
    ad = zsm[:, LANES:2 * LANES]
    gd = zsm[:, 2 * LANES:4 * LANES]

    w_log = -_softplus(-(w0_ref[...] + _dot(jnp.tanh(wd).astype(BF16), w2_ref[...]))) - 0.5
    lw_out[0] = -jnp.exp(w_log)
    a_sig = _sigmoid(a0_ref[...] + _dot(ad.astype(BF16), a2_ref[...]))
    g_out[0] = _dot(_sigmoid(gd).astype(BF16), g2_ref[...]).astype(g_out.dtype)

    ones_bd = _head_ones()
    kk = k * kk_ref[...]
    kk2 = kk * kk
    nrm = jnp.concatenate(
        [_head_sum(kk2[:, p * LANES:(p + 1) * LANES], ones_bd) for p in range(PAIRS)], axis=1)
    kk = kk * lax.rsqrt(jnp.maximum(nrm, 1e-24))
    r_out[0] = r.astype(r_out.dtype)
    k_out[0] = (k * (1.0 + (a_sig - 1.0) * ka_ref[...])).astype(k_out.dtype)
    v_out[0] = v.astype(v_out.dtype)
    a_out[0] = (-kk).astype(a_out.dtype)
    b_out[0] = (kk * a_sig).astype(b_out.dtype)


def _rwkv_prep(proj3d, prev0_z, prev0_s, mu_z, mu_s, w0, w2p, a0, a2p, g2, k_k, k_a, *, tt):
    b, l, _ = proj3d.shape
    s_blk = C_S // S_COLS
    zw = 3 * WIDTH
    row = lambda a: a.reshape(1, -1)
    vec = lambda n: pl.BlockSpec((1, n), lambda bi, i: (0, 0))
    full = lambda r, c: pl.BlockSpec((r, c), lambda bi, i: (0, 0))
    out_spec = pl.BlockSpec((1, tt, WIDTH), lambda bi, i: (bi, i, 0))
    sds = lambda dt: jax.ShapeDtypeStruct((b, l, WIDTH), dt)
    return pl.pallas_call(
        functools.partial(_rwkv_prep_kernel, tt=tt),
        grid=(b, l // tt),
        in_specs=[pl.BlockSpec((1, tt, zw), lambda bi, i: (bi, i, 0)),
                  pl.BlockSpec((1, tt, S_COLS), lambda bi, i: (bi, i, s_blk)),
                  pl.BlockSpec((1, 8, zw), lambda bi, i: (bi, jnp.maximum(i * (tt // 8) - 1, 0), 0)),
                  pl.BlockSpec((1, 8, S_COLS),
                               lambda bi, i: (bi, jnp.maximum(i * (tt // 8) - 1, 0), s_blk)),
                  full(8, zw), full(8, S_COLS),
                  vec(zw), vec(S_COLS),
                  vec(WIDTH), full(LANES, WIDTH), vec(WIDTH), full(LANES, WIDTH),
                  full(GATE_LORA, WIDTH), vec(WIDTH), vec(WIDTH)],
        out_specs=[out_spec] * 7,
        out_shape=[sds(BF16), sds(F32)] + [sds(BF16)] * 5,
        compiler_params=pltpu.CompilerParams(
            dimension_semantics=("parallel", "arbitrary"), vmem_limit_bytes=VMEM_LIMIT),
        name="rwkv_prep",
    )(proj3d, proj3d, proj3d, proj3d, prev0_z, prev0_s, row(mu_z), row(mu_s),
      row(w0), w2p, row(a0), a2p, g2, row(k_k), row(k_a))


def _bf(x):
    return x.astype(BF16)


WKV_CHUNKS = 4


def _wkv_group(a_t, r_t, b_t, k_t, b_h, k_h, v, s, w_total, masks):
    head0, strict, lower, eye = masks
    dg = functools.partial(lax.dot_general, preferred_element_type=F32)
    n = range(len(s))

    def bd(x):
        return jnp.concatenate([jnp.where(head0, x, 0.0), jnp.where(head0, 0.0, x)], axis=0)

    lhs = [_bf(jnp.concatenate([bd(a_t[i]), bd(r_t[i])], axis=0)) for i in n]
    rhs = [_bf(jnp.concatenate([bd(b_t[i]), bd(k_t[i])], axis=0)) for i in n]
    v_bd = [_bf(bd(v[i])) for i in n]
    gram = [dg(lhs[i], rhs[i], _NT) for i in n]
    a_s = [dg(lhs[i], _bf(s[i]), _NT) for i in n]
    n_ab = [jnp.where(strict, g[0:LANES, 0:LANES], 0.0) for g in gram]
    a_ak = [jnp.where(strict, g[0:LANES, LANES:2 * LANES], 0.0) for g in gram]
    a_r = [_bf(jnp.concatenate([jnp.where(lower, g[LANES:2 * LANES, 0:LANES], 0.0),
                                jnp.where(lower, g[LANES:2 * LANES, LANES:2 * LANES], 0.0)], axis=1))
           for g in gram]
    rhs_u = [a_s[i][0:LANES] + dg(_bf(a_ak[i]), v_bd[i], _NN) for i in n]

    p_inv = [eye + x for x in n_ab]
    q = [_bf(x) for x in n_ab]
    for _ in range(5):
        q = [_bf(dg(x, x, _NN)) for x in q]
        p_inv = [p_inv[i] + dg(q[i], _bf(p_inv[i]), _NN) for i in n]
    u = [dg(_bf(p_inv[i]), _bf(rhs_u[i]), _NN) for i in n]

    uv = [jnp.concatenate([_bf(u[i]), v_bd[i]], axis=0) for i in n]
    y_bd = [a_s[i][LANES:2 * LANES] + dg(a_r[i], uv[i], _NN) for i in n]
    y = [x[0:CHUNK] + x[CHUNK:2 * CHUNK] for x in y_bd]
    s_new = [w_total[i] * s[i]
             + dg(uv[i], _bf(jnp.concatenate([bd(b_h[i]), bd(k_h[i])], axis=0)), _TN) for i in n]
    return y, s_new


def _wkv_kernel(r_ref, lw_ref, k_ref, v_ref, a_ref, b_ref, g_ref, s0_ref,
                gnw_ref, gnb_ref, rk_ref, y_ref, sfin_ref, s_scr, *, nc):
    c = pl.program_id(1)

    @pl.when(c == 0)
    def _():
        s_scr[...] = s0_ref[0]

    rows_all = WKV_CHUNKS * CHUNK
    r = r_ref[0].astype(F32)
    lw = lw_ref[0]
    k = k_ref[0].astype(F32)
    v = v_ref[0].astype(F32)
    a = a_ref[0].astype(F32)
    b = b_ref[0].astype(F32)

    ti = lax.broadcasted_iota(jnp.int32, (rows_all, rows_all), 0)
    tj = lax.broadcasted_iota(jnp.int32, (rows_all, rows_all), 1)
    ltri = jnp.where((tj <= ti) & (ti // CHUNK == tj // CHUNK), 1.0, 0.0).astype(BF16)
    hi, mid, lo = _split3(lw)
    cum = _dot(jnp.concatenate([ltri, ltri, ltri], axis=1),
               jnp.concatenate([hi, mid, lo], axis=0))

    lane = lax.broadcasted_iota(jnp.int32, (1, LANES), 1)
    bi = lax.broadcasted_iota(jnp.int32, (LANES, LANES), 0)
    bj = lax.broadcasted_iota(jnp.int32, (LANES, LANES), 1)
    same = (bi // CHUNK) == (bj // CHUNK)
    masks = (lane < HEAD_DIM,
             same & ((bj % CHUNK) < (bi % CHUNK)),
             same & ((bj % CHUNK) <= (bi % CHUNK)),
             jnp.where(bi == bj, 1.0, 0.0))

    ones_bd = _head_ones()
    inv_n = 1.0 / HEAD_DIM
    sls = [slice(p * LANES, (p + 1) * LANES) for p in range(PAIRS)]
    cut = lambda x: [x[:, sl] for sl in sls]
    state = [s_scr[p] for p in range(PAIRS)]
    for sc in range(WKV_CHUNKS):
        rows = slice(sc * CHUNK, (sc + 1) * CHUNK)
        cum_c, lw_c = cum[rows], lw[rows]
        total = cum_c[CHUNK - 1:CHUNK, :]
        e_inv = jnp.exp(-cum_c)
        e_end = jnp.exp(total - cum_c)
        ys, state = _wkv_group(
            cut(a[rows] * jnp.exp(cum_c - lw_c)), cut(r[rows] * jnp.exp(cum_c)),
            cut(b[rows] * e_inv), cut(k[rows] * e_inv), cut(b[rows] * e_end), cut(k[rows] * e_end),
            cut(v[rows]), state, cut(jnp.exp(total)), masks)

        y_all = jnp.concatenate(ys, axis=0)
        rk_all = jnp.concatenate([r[rows, sl] * k[rows, sl] * rk_ref[:, sl] for sl in sls], axis=0)
        bonus_all = _head_sum(rk_all, ones_bd)
        d_all = y_all - _head_sum(y_all, ones_bd) * inv_n
        rstd_all = lax.rsqrt(_head_sum(d_all * d_all, ones_bd) * inv_n + GN_EPS)
        for i, sl in enumerate(sls):
            prow = slice(i * CHUNK, (i + 1) * CHUNK)
            yn = d_all[prow] * rstd_all[prow] * gnw_ref[:, sl] + gnb_ref[:, sl]
            y_ref[0, rows, sl] = ((yn + bonus_all[prow] * v[rows, sl])
                                  * g_ref[0, rows, sl].astype(F32)).astype(y_ref.dtype)

    for p in range(PAIRS):
        s_scr[p] = state[p]

        @pl.when(c == nc - 1)
        def _(p=p):
            sfin_ref[0, p] = state[p]


def _wkv(r, lw, k, v, a, b, g, s0, gn_w, gn_b, r_k):
    bsz, l, _ = r.shape
    nc = l // (WKV_CHUNKS * CHUNK)
    tok = pl.BlockSpec((1, WKV_CHUNKS * CHUNK, WIDTH), lambda bi, c: (bi, c, 0))
    vec = pl.BlockSpec((1, WIDTH), lambda bi, c: (0, 0))
    st_in = pl.BlockSpec((1, PAIRS, LANES, LANES), lambda bi, c: (0, 0, 0, 0))
    st_out = pl.BlockSpec((1, PAIRS, LANES, LANES), lambda bi, c: (bi, 0, 0, 0))
    row = lambda x: x.reshape(1, WIDTH)
    return pl.pallas_call(
        functools.partial(_wkv_kernel, nc=nc),
        grid=(bsz, nc),
        in_specs=[tok] * 7 + [st_in, vec, vec, vec],
        out_specs=[tok, st_out],
        out_shape=[jax.ShapeDtypeStruct((bsz, l, WIDTH), BF16),
                   jax.ShapeDtypeStruct((bsz, PAIRS, LANES, LANES), F32)],
        scratch_shapes=[pltpu.VMEM((PAIRS, LANES, LANES), F32)],
        compiler_params=pltpu.CompilerParams(
            dimension_semantics=("parallel", "arbitrary"), vmem_limit_bytes=VMEM_LIMIT),
        name="wkv",
    )(r, lw, k, v, a, b, g, s0, row(gn_w), row(gn_b), row(r_k))


def _fox_prep_kernel(z_ref, zs_ref, c0_ref, qg_ref, kg_ref, fb_ref,
                     qt_out, k_out, vt_out, c_out, carry, *, tt, n_pad):
    i = pl.program_id(1)

    @pl.when(i == 0)
    def _():
        carry[...] = c0_ref[...]

    z = z_ref[0]
    fl = zs_ref[0][:, 0:LANES]
    log_f = -_softplus(-(fl + fb_ref[...]))
    ti = lax.broadcasted_iota(jnp.int32, (tt, tt), 0)
    tj = lax.broadcasted_iota(jnp.int32, (tt, tt), 1)
    ltri = jnp.where(tj <= ti, 1.0, 0.0).astype(BF16)
    hi, mid, lo = _split3(log_f)
    cum = carry[...] + _dot(jnp.concatenate([ltri, ltri, ltri], axis=1),
                            jnp.concatenate([hi, mid, lo], axis=0))
    carry[...] = cum[tt - 1:tt, :]
    c_out[0] = cum[tt - 8:tt, :]
    c_hi, c_mid, c_lo = (t.astype(F32) for t in _split3(cum * LOG2E))
    neg_hi, neg_mid, neg_lo = -c_hi, -c_mid, -c_lo
    if n_pad:
        grow = i * tt + lax.broadcasted_iota(jnp.int32, (tt, 1), 0)
        neg_hi = jnp.where(grow < n_pad, MASKED_KEY, neg_hi)

    for p in range(PAIRS):
        vt_out[0, p, 0] = z[:, 2 * WIDTH + p * LANES:2 * WIDTH + (p + 1) * LANES].T.astype(BF16)
    ones_bd = _head_ones()
    lane = lax.broadcasted_iota(jnp.int32, (1, LANES), 1)
    inv_n = 1.0 / HEAD_DIM
    for p in range(PAIRS):
        zq = z[:, p * LANES:(p + 1) * LANES]
        zk = z[:, WIDTH + p * LANES:WIDTH + (p + 1) * LANES]
        qn = zq * lax.rsqrt(_head_sum(zq * zq, ones_bd) * inv_n + RMS_EPS) * (qg_ref[...] * (ATTN_SCALE * LOG2E))
        kn = zk * lax.rsqrt(_head_sum(zk * zk, ones_bd) * inv_n + RMS_EPS) * kg_ref[...]
        for j in range(2):
            h = 2 * p + j
            own = (lane < HEAD_DIM) if j == 0 else (lane >= HEAD_DIM)
            e = lane - (HEAD_DIM if j == 0 else 0)
            col = lambda t: t[:, FL_LANE + h:FL_LANE + h + 1]
            q_extra = jnp.where((e >= 0) & (e < 3), 1.0, 0.0)
            k_extra = jnp.where(e == 0, col(neg_hi), jnp.where(e == 1, col(neg_mid), jnp.where(
                e == 2, col(neg_lo), 0.0)))
            qt_out[0, h, 0] = jnp.where(own, qn, q_extra).T.astype(BF16)
            k_out[0, h] = jnp.where(own, kn, k_extra).astype(BF16)


def _fox_prep(proj3d, c0, q_g, k_g, f_bias, *, tt, n_pad):
    b, l, _ = proj3d.shape
    zw = 3 * WIDTH
    tile2 = lambda g: jnp.concatenate([g, g]).reshape(1, LANES)
    fb = jnp.concatenate([jnp.zeros((FL_LANE,), F32), f_bias,
                          jnp.zeros((LANES - FL_LANE - HEADS,), F32)]).reshape(1, LANES)
    hd = pl.BlockSpec((1, HEADS, tt, LANES), lambda bi, i: (bi, 0, i, 0))
    vec = pl.BlockSpec((1, LANES), lambda bi, i: (0, 0))
    return pl.pallas_call(
        functools.partial(_fox_prep_kernel, tt=tt, n_pad=n_pad),
        grid=(b, l // tt),
        in_specs=[pl.BlockSpec((1, tt, zw), lambda bi, i: (bi, i, C_FQ // zw)),
                  pl.BlockSpec((1, tt, S_COLS), lambda bi, i: (bi, i, C_S // S_COLS)),
                  vec, vec, vec, vec],
        out_specs=[pl.BlockSpec((1, HEADS, 1, LANES, tt), lambda bi, i: (bi, 0, i, 0, 0)),
                   hd,
                   pl.BlockSpec((1, PAIRS, 1, LANES, tt), lambda bi, i: (bi, 0, i, 0, 0)),
                   pl.BlockSpec((1, 8, LANES), lambda bi, i: (bi, i, 0))],
        out_shape=[jax.ShapeDtypeStruct((b, HEADS, l // tt, LANES, tt), BF16),
                   jax.ShapeDtypeStruct((b, HEADS, l, LANES), BF16),
                   jax.ShapeDtypeStruct((b, PAIRS, l // tt, LANES, tt), BF16),
                   jax.ShapeDtypeStruct((b, (l // tt) * 8, LANES), F32)],
        scratch_shapes=[pltpu.VMEM((1, LANES), F32)],
        compiler_params=pltpu.CompilerParams(
            dimension_semantics=("parallel", "arbitrary"), vmem_limit_bytes=VMEM_LIMIT),
        name="fox_prep",
    )(proj3d, proj3d, c0, tile2(q_g), tile2(k_g), fb)


ATTN_SUBTILES = 4


def _attn_kernel(qt_ref, k_ref, vt_ref, km_ref, o_ref, s0_buf, s1_buf, p0_buf, p1_buf, acc_buf,
                 *, tk):
    hq = ATTN_SUBTILES
    s_buf = (s0_buf, s1_buf)
    p_buf = (p0_buf, p1_buf)
    qi = pl.program_id(2)
    key_i = lax.broadcasted_iota(jnp.int32, (tk, tk), 0)
    qry_i = lax.broadcasted_iota(jnp.int32, (tk, tk), 1)
    causal = key_i <= qry_i
    every = tuple(range(2 * hq))
    from_sub = lambda u0: tuple(n for n in every if n % hq >= u0)
    qts = [qt_ref[0, n // hq, n % hq] for n in every]

    def real_keys(r):
        off = pl.multiple_of(r * tk, tk)
        return (k_ref[0, 0, pl.ds(off, tk), :], k_ref[0, 1, pl.ds(off, tk), :])

    def stage_a(kblks, slot, which):
        for n in which:
            s_buf[slot][n] = _dot(kblks[n // hq], qts[n])

    def stage_b(slot, stats, alpha, masks):
        which = tuple(masks)
        s = {n: s_buf[slot][n] for n in which}
        for n in which:
            if masks[n] is not None:
                s[n] = jnp.where(masks[n], s[n], MASKED_KEY)
        m_new = {n: jnp.maximum(stats[n][0], jnp.max(s[n], axis=0, keepdims=True)) for n in which}
        a_new = {n: jnp.exp2(stats[n][0] - m_new[n]) for n in which}
        p = {n: jnp.exp2(s[n] - m_new[n]) for n in which}
        for n in which:
            p_buf[slot][n] = p[n].astype(BF16)
        l_new = {n: a_new[n] * stats[n][1] + jnp.sum(p[n], axis=0, keepdims=True) for n in which}
        stats = tuple((m_new[n], l_new[n]) if n in which else stats[n] for n in every)
        alpha = tuple(a_new[n] if n in which else alpha[n] for n in every)
        return stats, alpha

    def stage_c(slot, e, alpha, which):
        vt = vt_ref[0, 0, e]
        pv = {n: _dot(vt[(n // hq) * HEAD_DIM:(n // hq + 1) * HEAD_DIM, :], p_buf[slot][n])
              for n in which}
        for n in which:
            acc_buf[n] = alpha[n] * acc_buf[n] + pv[n]

    unmasked = {n: None for n in every}
    stats = ((jnp.full((1, tk), MASKED_KEY, F32), jnp.zeros((1, tk), F32)),) * len(every)
    alpha = (jnp.ones((1, tk), F32),) * len(every)
    acc_buf[...] = jnp.zeros_like(acc_buf)
    p1_buf[...] = jnp.zeros_like(p1_buf)
    stage_a((km_ref[0, 0], km_ref[0, 1]), 0, every)

    def body(i, state):
        stats, alpha_prev = state
        stats, alpha_a = stage_b(0, stats, alpha_prev, unmasked)
        stage_a(real_keys(2 * i), 1, every)
        stage_c(1, jnp.maximum(2 * i - 1, 0), alpha_prev, every)
        stage_a(real_keys(2 * i + 1), 0, every)
        stats, alpha_b = stage_b(1, stats, alpha_a, unmasked)
        stage_c(0, 2 * i, alpha_a, every)
        return stats, alpha_b

    stats, alpha_prev = lax.fori_loop(0, (hq // 2) * qi, body, (stats, alpha))
    e0 = hq * qi
    alphas = {-1: alpha_prev}
    for r in range(1, hq + 3):
        if r <= hq:
            stage_a(real_keys(e0 + r - 1), r % 2, from_sub(r - 1))
        j = r - 1
        if j <= hq:
            masks = unmasked if j == 0 else {n: (causal if n % hq == j - 1 else None)
                                             for n in from_sub(j - 1)}
            stats, alphas[j] = stage_b(j % 2, stats, alphas[j - 1], masks)
        j = r - 2
        if j == -1:
            stage_c(1, jnp.maximum(e0 - 1, 0), alphas[-1], every)
        else:
            stage_c(j % 2, e0 + j, alphas[j], every if j == 0 else from_sub(j - 1))
    for u in range(hq):
        out_t = jnp.concatenate([acc_buf[u] / stats[u][1], acc_buf[hq + u] / stats[hq + u][1]], axis=0)
        o_ref[0, u * tk:(u + 1) * tk, :] = out_t.T.astype(o_ref.dtype)


def _attn(qt, kp, vt_all, kp_meta, *, tk):
    b, _, l, _ = kp.shape
    nk = vt_all.shape[2]
    hq = ATTN_SUBTILES
    assert vt_all.shape[4] == tk and qt.shape[4] == tk and kp_meta.shape[2] == tk and l % (hq * tk) == 0
    return pl.pallas_call(
        functools.partial(_attn_kernel, tk=tk),
        grid=(b, PAIRS, l // (hq * tk)),
        in_specs=[pl.BlockSpec((1, 2, hq, LANES, tk), lambda bi, p, qi: (bi, p, qi, 0, 0)),
                  pl.BlockSpec((1, 2, l, LANES), lambda bi, p, qi: (bi, p, 0, 0)),
                  pl.BlockSpec((1, 1, nk, LANES, tk), lambda bi, p, qi: (bi, p, 0, 0, 0)),
                  pl.BlockSpec((1, 2, tk, LANES), lambda bi, p, qi: (0, p, 0, 0))],
        out_specs=pl.BlockSpec((1, hq * tk, LANES), lambda bi, p, qi: (bi, qi, p)),
        out_shape=jax.ShapeDtypeStruct((b, l, WIDTH), BF16),
        scratch_shapes=[pltpu.VMEM((2 * hq, tk, tk), F32), pltpu.VMEM((2 * hq, tk, tk), F32),
                        pltpu.VMEM((2 * hq, tk, tk), BF16), pltpu.VMEM((2 * hq, tk, tk), BF16),
                        pltpu.VMEM((2 * hq, HEAD_DIM, tk), F32)],
        compiler_params=pltpu.CompilerParams(
            dimension_semantics=("parallel", "parallel", "arbitrary"), vmem_limit_bytes=VMEM_LIMIT),
        name="attn",
    )(qt, kp, vt_all, kp_meta)


def _pick(n, prefs):
    for t in prefs:
        if n % t == 0:
            return t
    raise ValueError(f"no tile for {n}")


def _pack_w_in(w_in):
    rw, fx = 3 * WIDTH + DECAY_LORA + AAA_LORA + GATE_LORA, 3 * WIDTH + HEADS
    wd0, ad0, gd0 = 3 * WIDTH, 3 * WIDTH + DECAY_LORA, 3 * WIDTH + DECAY_LORA + AAA_LORA
    z = lambda n: jnp.zeros((w_in.shape[0], n), w_in.dtype)
    return jnp.concatenate([
        w_in[:, 0:3 * WIDTH], w_in[:, rw:rw + 3 * WIDTH], w_in[:, rw + fx:],
        w_in[:, wd0:ad0], w_in[:, rw + 3 * WIDTH:rw + fx], z(LANES - DECAY_LORA - HEADS),
        w_in[:, ad0:gd0], z(LANES - AAA_LORA), w_in[:, gd0:rw]], axis=1).astype(BF16)


def _pack_mu(mu):
    wd0, ad0, gd0 = 3 * WIDTH, 3 * WIDTH + DECAY_LORA, 3 * WIDTH + DECAY_LORA + AAA_LORA
    z = lambda n: jnp.zeros((n,), mu.dtype)
    mu_s = jnp.concatenate([mu[wd0:ad0], z(LANES - DECAY_LORA), mu[ad0:gd0], z(LANES - AAA_LORA),
                            mu[gd0:]])
    return mu[:3 * WIDTH], mu_s


def _pad_rows(w, rows):
    return jnp.concatenate([w, jnp.zeros((rows - w.shape[0], w.shape[1]), w.dtype)], axis=0)


def _mixer_inputs(h2d, bsz, norm_g, w_pack, tm, tn):
    proj = _rms_matmul(h2d, norm_g, w_pack, tm=tm, tn=tn)
    return proj, proj.reshape(bsz, h2d.shape[0] // bsz, N_PACK)


def kernel(x, meta_tokens, norm1_g, w_in, rwkv_mu, rwkv_w0, rwkv_w2, rwkv_a0, rwkv_a2, rwkv_g2,
           rwkv_k_k, rwkv_k_a, rwkv_r_k, rwkv_gn_w, rwkv_gn_b, fox_q_norm_g, fox_k_norm_g,
           fox_f_bias, w_branch_a, w_branch_b, w_o, norm2_g, w_gate_up, w_down):
    bsz, seq, d = x.shape
    assert d == D_MODEL and norm1_g.shape[0] == 1 and seq % (ATTN_SUBTILES * 256) == 0
    m = bsz * seq
    x2d = x.reshape(m, d)
    meta2d = jnp.concatenate(
        [jnp.zeros((META_ROWS - N_META, d), x.dtype), meta_tokens.astype(x.dtype)], axis=0)

    w_pack = _pack_w_in(w_in[0].astype(BF16))
    mu_z, mu_s = _pack_mu(rwkv_mu[0])
    w2p = _pad_rows(rwkv_w2[0], LANES).astype(BF16)
    a2p = _pad_rows(rwkv_a2[0], LANES).astype(BF16)
    g2 = rwkv_g2[0].astype(BF16)
    r_k = rwkv_r_k[0].reshape(WIDTH)

    tm = _pick(m, (1024, 512, 256))
    tt = _pick(seq, (256,))

    _, pm3 = _mixer_inputs(meta2d, 1, norm1_g[0], w_pack, META_ROWS, tn=512)
    zero_z = jnp.zeros((8, 3 * WIDTH), F32)
    zero_s = jnp.zeros((8, S_COLS), F32)
    prep = functools.partial(_rwkv_prep, mu_z=mu_z, mu_s=mu_s, w0=rwkv_w0[0], w2p=w2p,
                             a0=rwkv_a0[0], a2p=a2p, g2=g2, k_k=rwkv_k_k[0], k_a=rwkv_k_a[0])
    wkv = functools.partial(_wkv, gn_w=rwkv_gn_w[0], gn_b=rwkv_gn_b[0], r_k=r_k)
    fox = functools.partial(_fox_prep, q_g=fox_q_norm_g[0], k_g=fox_k_norm_g[0],
                            f_bias=fox_f_bias[0])
    mr = prep(pm3, zero_z, zero_s, tt=META_ROWS)
    _, s_meta = wkv(*mr, jnp.zeros((1, PAIRS, LANES, LANES), F32))
    _, kp_meta, vt_meta, c_meta = fox(pm3, jnp.zeros((1, LANES), F32), tt=META_ROWS,
                                      n_pad=META_ROWS - N_META)

    proj2d, p3 = _mixer_inputs(x2d, bsz, norm1_g[0], w_pack, tm, tn=1536)
    rr = prep(p3, pm3[0, META_ROWS - 8:, 0:3 * WIDTH], pm3[0, META_ROWS - 8:, C_S:], tt=tt)
    ya, _ = wkv(*rr, s_meta)
    qt, kp, vt, _ = fox(p3, c_meta[0, 7:8, :], tt=tt, n_pad=0)
    vt_all = jnp.concatenate([jnp.broadcast_to(vt_meta, (bsz,) + vt_meta.shape[1:]), vt], axis=2)
    yb = _attn(qt, kp, vt_all, kp_meta, tk=tt)

    merged = _merge(ya.reshape(m, WIDTH), yb.reshape(m, WIDTH), w_branch_a[0].astype(BF16),
                    w_branch_b[0].astype(BF16), proj2d, tm=tm // 2, tn=D_MODEL)
    h1 = _matmul_res(merged, w_o[0].astype(BF16), x2d, tm=tm, tn=1024)
    act = _rms_matmul_glu(h1, norm2_g[0], w_gate_up[0].astype(BF16), tm=tm, tn=512)
    out = _matmul_res(act, w_down[0].astype(BF16), h1, tm=tm, tn=512)
    return out.reshape(bsz, seq, d)
```

```python
import functools

import jax
import jax.numpy as jnp
from jax import lax
from jax.experimental import pallas as pl
from jax.experimental.pallas import tpu as pltpu

F32 = jnp.float32
BF16 = jnp.bfloat16

D_MODEL = 2048
N_META = 16
HEAD_DIM = 64
HEADS = 16
WIDTH = HEADS * HEAD_DIM
PAIRS = HEADS // 2
DECAY_LORA = 96
AAA_LORA = 96
GATE_LORA = 256
D_FF = 5632
RMS_EPS = 1e-6
GN_EPS = 64e-5
ATTN_SCALE = HEAD_DIM ** -0.5
LOG2E = 1.4426950408889634

LANES = 128
META_ROWS = 256
CHUNK = 64
MASKED_KEY = -1e30

C_R, C_K, C_V = 0, 1024, 2048
C_FQ, C_FK, C_FV = 3072, 4096, 5120
C_GA, C_GB = 6144, 8192
C_S = 10240
S_COLS = 512
FL_LANE = 96
N_PACK = C_S + S_COLS

VMEM_LIMIT = 56 * 1024 * 1024


def _sigmoid(x):
    return 0.5 * jnp.tanh(0.5 * x) + 0.5


def _softplus(x):
    return jnp.maximum(x, 0.0) + jnp.log(1.0 + jnp.exp(-jnp.abs(x)))


def _dot(a, b):
    return jnp.dot(a, b, preferred_element_type=F32)


def _dot_nt(a, b):
    return lax.dot_general(a, b, (((1,), (1,)), ((), ())), preferred_element_type=F32)


def _dot_tn(a, b):
    return lax.dot_general(a, b, (((0,), (0,)), ((), ())), preferred_element_type=F32)


def _split3(x):
    hi = x.astype(BF16)
    r1 = x - hi.astype(F32)
    mid = r1.astype(BF16)
    lo = (r1 - mid.astype(F32)).astype(BF16)
    return hi, mid, lo


_NN = (((1,), (0,)), ((), ()))
_NT = (((1,), (1,)), ((), ()))
_TN = (((0,), (0,)), ((), ()))


def _head_ones():
    r = lax.broadcasted_iota(jnp.int32, (LANES, LANES), 0) // HEAD_DIM
    c = lax.broadcasted_iota(jnp.int32, (LANES, LANES), 1) // HEAD_DIM
    return jnp.where(r == c, 1.0, 0.0).astype(BF16)


def _head_sum(x, ones_bd):
    hi = x.astype(BF16)
    lo = (x - hi.astype(F32)).astype(BF16)
    return _dot(jnp.concatenate([hi, lo], axis=1), jnp.concatenate([ones_bd, ones_bd], axis=0))


def _rms_mm_kernel(x_ref, g_ref, w_ref, o_ref, xn_ref):
    @pl.when(pl.program_id(1) == 0)
    def _():
        x = x_ref[...]
        ms = jnp.mean(x * x, axis=-1, keepdims=True)
        xn_ref[...] = ((x * lax.rsqrt(ms + RMS_EPS)) * g_ref[...]).astype(BF16)

    o_ref[...] = _dot(xn_ref[...], w_ref[...]).astype(o_ref.dtype)


def _rms_glu_kernel(x_ref, g_ref, wg_ref, wu_ref, o_ref, xn_ref):
    @pl.when(pl.program_id(1) == 0)
    def _():
        x = x_ref[...]
        ms = jnp.mean(x * x, axis=-1, keepdims=True)
        xn_ref[...] = ((x * lax.rsqrt(ms + RMS_EPS)) * g_ref[...]).astype(BF16)

    xn = xn_ref[...]
    gate = _dot(xn, wg_ref[...])
    up = _dot(xn, wu_ref[...])
    o_ref[...] = (gate * _sigmoid(gate) * up).astype(o_ref.dtype)


def _rms_matmul(x2d, g, w_bf16, *, tm, tn, out_dtype=F32):
    m, d = x2d.shape
    n = w_bf16.shape[1]
    return pl.pallas_call(
        _rms_mm_kernel,
        grid=(m // tm, n // tn),
        in_specs=[pl.BlockSpec((tm, d), lambda i, j: (i, 0)),
                  pl.BlockSpec((1, d), lambda i, j: (0, 0)),
                  pl.BlockSpec((d, tn), lambda i, j: (0, j))],
        out_specs=pl.BlockSpec((tm, tn), lambda i, j: (i, j)),
        out_shape=jax.ShapeDtypeStruct((m, n), out_dtype),
        scratch_shapes=[pltpu.VMEM((tm, d), BF16)],
        compiler_params=pltpu.CompilerParams(
            dimension_semantics=("parallel", "arbitrary"), vmem_limit_bytes=VMEM_LIMIT),
        name="rms_matmul",
    )(x2d, g.reshape(1, d), w_bf16)


def _rms_matmul_glu(x2d, g, w_bf16, *, tm, tn):
    m, d = x2d.shape
    n = w_bf16.shape[1] // 2
    up_off = n // tn
    return pl.pallas_call(
        _rms_glu_kernel,
        grid=(m // tm, n // tn),
        in_specs=[pl.BlockSpec((tm, d), lambda i, j: (i, 0)),
                  pl.BlockSpec((1, d), lambda i, j: (0, 0)),
                  pl.BlockSpec((d, tn), lambda i, j: (0, j)),
                  pl.BlockSpec((d, tn), lambda i, j: (0, j + up_off))],
        out_specs=pl.BlockSpec((tm, tn), lambda i, j: (i, j)),
        out_shape=jax.ShapeDtypeStruct((m, n), BF16),
        scratch_shapes=[pltpu.VMEM((tm, d), BF16)],
        compiler_params=pltpu.CompilerParams(
            dimension_semantics=("parallel", "arbitrary"), vmem_limit_bytes=VMEM_LIMIT),
        name="rms_matmul_glu",
    )(x2d, g.reshape(1, d), w_bf16, w_bf16)


def _mm_res_kernel(a_ref, b_ref, r_ref, o_ref):
    o_ref[...] = _dot(a_ref[...], b_ref[...]) + r_ref[...]


def _matmul_res(a, b, res, *, tm, tn):
    m, kk = a.shape
    n = b.shape[1]
    return pl.pallas_call(
        _mm_res_kernel,
        grid=(m // tm, n // tn),
        in_specs=[pl.BlockSpec((tm, kk), lambda i, j: (i, 0)),
                  pl.BlockSpec((kk, tn), lambda i, j: (0, j)),
                  pl.BlockSpec((tm, tn), lambda i, j: (i, j))],
        out_specs=pl.BlockSpec((tm, tn), lambda i, j: (i, j)),
        out_shape=jax.ShapeDtypeStruct((m, n), F32),
        compiler_params=pltpu.CompilerParams(
            dimension_semantics=("parallel", "arbitrary"), vmem_limit_bytes=VMEM_LIMIT),
        name="matmul_res",
    )(a, b, res)


def _merge_kernel(ya_ref, yb_ref, wa_ref, wb_ref, za_ref, zb_ref, o_ref):
    ta = _dot(ya_ref[...], wa_ref[...])
    tb = _dot(yb_ref[...], wb_ref[...])
    o_ref[...] = (_sigmoid(za_ref[...]) * ta + _sigmoid(zb_ref[...]) * tb).astype(o_ref.dtype)


def _merge(ya, yb, wa, wb, proj2d, *, tm, tn):
    m, kk = ya.shape
    n = wa.shape[1]
    oa, ob = C_GA // tn, C_GB // tn
    return pl.pallas_call(
        _merge_kernel,
        grid=(m // tm, n // tn),
        in_specs=[pl.BlockSpec((tm, kk), lambda i, j: (i, 0)),
                  pl.BlockSpec((tm, kk), lambda i, j: (i, 0)),
                  pl.BlockSpec((kk, tn), lambda i, j: (0, j)),
                  pl.BlockSpec((kk, tn), lambda i, j: (0, j)),
                  pl.BlockSpec((tm, tn), lambda i, j: (i, j + oa)),
                  pl.BlockSpec((tm, tn), lambda i, j: (i, j + ob))],
        out_specs=pl.BlockSpec((tm, tn), lambda i, j: (i, j)),
        out_shape=jax.ShapeDtypeStruct((m, n), BF16),
        compiler_params=pltpu.CompilerParams(
            dimension_semantics=("parallel", "arbitrary"), vmem_limit_bytes=VMEM_LIMIT),
        name="merge",
    )(ya, yb, wa, wb, proj2d, proj2d)


def _rwkv_prep_kernel(z_ref, zs_ref, pz_ref, ps_ref, p0z_ref, p0s_ref, mu_ref, mus_ref,
                      w0_ref, w2_ref, a0_ref, a2_ref, g2_ref, kk_ref, ka_ref,
                      r_out, lw_out, k_out, v_out, a_out, b_out, g_out, *, tt):
    first = pl.program_id(1) == 0
    z = z_ref[0]
    zs = zs_ref[0]
    pz = jnp.where(first, p0z_ref[7:8, :], pz_ref[0, 7:8, :])
    ps = jnp.where(first, p0s_ref[7:8, :], ps_ref[0, 7:8, :])
    row = lax.broadcasted_iota(jnp.int32, (tt, 1), 0)

    def lerp(cur, prev_last, mu):
        prev = jnp.where(row == 0, prev_last, pltpu.roll(cur, 1, axis=0))
        return cur + (prev - cur) * mu

    zm = lerp(z, pz, mu_ref[...])
    zsm = lerp(zs, ps, mus_ref[...])
    r = zm[:, C_R:C_R + WIDTH]
    k = zm[:, C_K:C_K + WIDTH]
    v = zm[:, C_V:C_V + WIDTH]
    wd = zsm[:, 0:LANES]
    ad = zsm[:, LANES:2 * LANES]
    gd = zsm[:, 2 * LANES:4 * LANES]

    w_log = -_softplus(-(w0_ref[...] + _dot(jnp.tanh(wd).astype(BF16), w2_ref[...]))) - 0.5
    lw_out[0] = -jnp.exp(w_log)
    a_sig = _sigmoid(a0_ref[...] + _dot(ad.astype(BF16), a2_ref[...]))
    g_out[0] = _dot(_sigmoid(gd).astype(BF16), g2_ref[...]).astype(g_out.dtype)

    ones_bd = _head_ones()
    kk = k * kk_ref[...]
    kk2 = kk * kk
    nrm = jnp.concatenate(
        [_head_sum(kk2[:, p * LANES:(p + 1) * LANES], ones_bd) for p in range(PAIRS)], axis=1)
    kk = kk * lax.rsqrt(jnp.maximum(nrm, 1e-24))
    r_out[0] = r.astype(r_out.dtype)
    k_out[0] = (k * (1.0 + (a_sig - 1.0) * ka_ref[...])).astype(k_out.dtype)
    v_out[0] = v.astype(v_out.dtype)
    a_out[0] = (-kk).astype(a_out.dtype)
    b_out[0] = (kk * a_sig).astype(b_out.dtype)


def _rwkv_prep(proj3d, prev0_z, prev0_s, mu_z, mu_s, w0, w2p, a0, a2p, g2, k_k, k_a, *, tt):
    b, l, _ = proj3d.shape
    s_blk = C_S // S_COLS
    zw = 3 * WIDTH
    row = lambda a: a.reshape(1, -1)
    vec = lambda n: pl.BlockSpec((1, n), lambda bi, i: (0, 0))
    full = lambda r, c: pl.BlockSpec((r, c), lambda bi, i: (0, 0))
    out_spec = pl.BlockSpec((1, tt, WIDTH), lambda bi, i: (bi, i, 0))
    sds = lambda dt: jax.ShapeDtypeStruct((b, l, WIDTH), dt)
    return pl.pallas_call(
        functools.partial(_rwkv_prep_kernel, tt=tt),
        grid=(b, l // tt),
        in_specs=[pl.BlockSpec((1, tt, zw), lambda bi, i: (bi, i, 0)),
                  pl.BlockSpec((1, tt, S_COLS), lambda bi, i: (bi, i, s_blk)),
                  pl.BlockSpec((1, 8, zw), lambda bi, i: (bi, jnp.maximum(i * (tt // 8) - 1, 0), 0)),
                  pl.BlockSpec((1, 8, S_COLS),
                               lambda bi, i: (bi, jnp.maximum(i * (tt // 8) - 1, 0), s_blk)),
                  full(8, zw), full(8, S_COLS),
                  vec(zw), vec(S_COLS),
                  vec(WIDTH), full(LANES, WIDTH), vec(WIDTH), full(LANES, WIDTH),
                  full(GATE_LORA, WIDTH), vec(WIDTH), vec(WIDTH)],
        out_specs=[out_spec] * 7,
        out_shape=[sds(BF16), sds(F32)] + [sds(BF16)] * 5,
        compiler_params=pltpu.CompilerParams(
            dimension_semantics=("parallel", "arbitrary"), vmem_limit_bytes=VMEM_LIMIT),
        name="rwkv_prep",
    )(proj3d, proj3d, proj3d, proj3d, prev0_z, prev0_s, row(mu_z), row(mu_s),
      row(w0), w2p, row(a0), a2p, g2, row(k_k), row(k_a))


def _bf(x):
    return x.astype(BF16)


WKV_CHUNKS = 4


def _wkv_group(a_t, r_t, b_t, k_t, b_h, k_h, v, s, w_total, masks):
    head0, strict, lower, eye = masks
    dg = functools.partial(lax.dot_general, preferred_element_type=F32)
    n = range(len(s))

    def bd(x):
        return jnp.concatenate([jnp.where(head0, x, 0.0), jnp.where(head0, 0.0, x)], axis=0)

    lhs = [_bf(jnp.concatenate([bd(a_t[i]), bd(r_t[i])], axis=0)) for i in n]
    rhs = [_bf(jnp.concatenate([bd(b_t[i]), bd(k_t[i])], axis=0)) for i in n]
    v_bd = [_bf(bd(v[i])) for i in n]
    gram = [dg(lhs[i], rhs[i], _NT) for i in n]
    a_s = [dg(lhs[i], _bf(s[i]), _NT) for i in n]
    n_ab = [jnp.where(strict, g[0:LANES, 0:LANES], 0.0) for g in gram]
    a_ak = [jnp.where(strict, g[0:LANES, LANES:2 * LANES], 0.0) for g in gram]
    a_r = [_bf(jnp.concatenate([jnp.where(lower, g[LANES:2 * LANES, 0:LANES], 0.0),
                                jnp.where(lower, g[LANES:2 * LANES, LANES:2 * LANES], 0.0)], axis=1))
           for g in gram]
    rhs_u = [a_s[i][0:LANES] + dg(_bf(a_ak[i]), v_bd[i], _NN) for i in n]

    p_inv = [eye + x for x in n_ab]
    q = [_bf(x) for x in n_ab]
    for _ in range(5):
        q = [_bf(dg(x, x, _NN)) for x in q]
        p_inv = [p_inv[i] + dg(q[i], _bf(p_inv[i]), _NN) for i in n]
    u = [dg(_bf(p_inv[i]), _bf(rhs_u[i]), _NN) for i in n]

    uv = [jnp.concatenate([_bf(u[i]), v_bd[i]], axis=0) for i in n]
    y_bd = [a_s[i][LANES:2 * LANES] + dg(a_r[i], uv[i], _NN) for i in n]
    y = [x[0:CHUNK] + x[CHUNK:2 * CHUNK] for x in y_bd]
    s_new = [w_total[i] * s[i]
             + dg(uv[i], _bf(jnp.concatenate([bd(b_h[i]), bd(k_h[i])], axis=0)), _TN) for i in n]
    return y, s_new


def _wkv_kernel(r_ref, lw_ref, k_ref, v_ref, a_ref, b_ref, g_ref, s0_ref,
                gnw_ref, gnb_ref, rk_ref, y_ref, sfin_ref, s_scr, *, nc):
    c = pl.program_id(1)

    @pl.when(c == 0)
    def _():
        s_scr[...] = s0_ref[0]

    rows_all = WKV_CHUNKS * CHUNK
    r = r_ref[0].astype(F32)
    lw = lw_ref[0]
    k = k_ref[0].astype(F32)
    v = v_ref[0].astype(F32)
    a = a_ref[0].astype(F32)
    b = b_ref[0].astype(F32)

    ti = lax.broadcasted_iota(jnp.int32, (rows_all, rows_all), 0)
    tj = lax.broadcasted_iota(jnp.int32, (rows_all, rows_all), 1)
    ltri = jnp.where((tj <= ti) & (ti // CHUNK == tj // CHUNK), 1.0, 0.0).astype(BF16)
    hi, mid, lo = _split3(lw)
    cum = _dot(jnp.concatenate([ltri, ltri, ltri], axis=1),
               jnp.concatenate([hi, mid, lo], axis=0))

    lane = lax.broadcasted_iota(jnp.int32, (1, LANES), 1)
    bi = lax.broadcasted_iota(jnp.int32, (LANES, LANES), 0)
    bj = lax.broadcasted_iota(jnp.int32, (LANES, LANES), 1)
    same = (bi // CHUNK) == (bj // CHUNK)
    masks = (lane < HEAD_DIM,
             same & ((bj % CHUNK) < (bi % CHUNK)),
             same & ((bj % CHUNK) <= (bi % CHUNK)),
             jnp.where(bi == bj, 1.0, 0.0))

    ones_bd = _head_ones()
    inv_n = 1.0 / HEAD_DIM
    sls = [slice(p * LANES, (p + 1) * LANES) for p in range(PAIRS)]
    cut = lambda x: [x[:, sl] for sl in sls]
    state = [s_scr[p] for p in range(PAIRS)]
    for sc in range(WKV_CHUNKS):
        rows = slice(sc * CHUNK, (sc + 1) * CHUNK)
        cum_c, lw_c = cum[rows], lw[rows]
        total = cum_c[CHUNK - 1:CHUNK, :]
        e_inv = jnp.exp(-cum_c)
        e_end = jnp.exp(total - cum_c)
        ys, state = _wkv_group(
            cut(a[rows] * jnp.exp(cum_c - lw_c)), cut(r[rows] * jnp.exp(cum_c)),
            cut(b[rows] * e_inv), cut(k[rows] * e_inv), cut(b[rows] * e_end), cut(k[rows] * e_end),
            cut(v[rows]), state, cut(jnp.exp(total)), masks)

        y_all = jnp.concatenate(ys, axis=0)
        rk_all = jnp.concatenate([r[rows, sl] * k[rows, sl] * rk_ref[:, sl] for sl in sls], axis=0)
        bonus_all = _head_sum(rk_all, ones_bd)
        d_all = y_all - _head_sum(y_all, ones_bd) * inv_n
        rstd_all = lax.rsqrt(_head_sum(d_all * d_all, ones_bd) * inv_n + GN_EPS)
        for i, sl in enumerate(sls):
            prow = slice(i * CHUNK, (i + 1) * CHUNK)
            yn = d_all[prow] * rstd_all[prow] * gnw_ref[:, sl] + gnb_ref[:, sl]
            y_ref[0, rows, sl] = ((yn + bonus_all[prow] * v[rows, sl])
                                  * g_ref[0, rows, sl].astype(F32)).astype(y_ref.dtype)

    for p in range(PAIRS):
        s_scr[p] = state[p]

        @pl.when(c == nc - 1)
        def _(p=p):
            sfin_ref[0, p] = state[p]


def _wkv(r, lw, k, v, a, b, g, s0, gn_w, gn_b, r_k):
    bsz, l, _ = r.shape
    nc = l // (WKV_CHUNKS * CHUNK)
    tok = pl.BlockSpec((1, WKV_CHUNKS * CHUNK, WIDTH), lambda bi, c: (bi, c, 0))
    vec = pl.BlockSpec((1, WIDTH), lambda bi, c: (0, 0))
    st_in = pl.BlockSpec((1, PAIRS, LANES, LANES), lambda bi, c: (0, 0, 0, 0))
    st_out = pl.BlockSpec((1, PAIRS, LANES, LANES), lambda bi, c: (bi, 0, 0, 0))
    row = lambda x: x.reshape(1, WIDTH)
    return pl.pallas_call(
        functools.partial(_wkv_kernel, nc=nc),
        grid=(bsz, nc),
        in_specs=[tok] * 7 + [st_in, vec, vec, vec],
        out_specs=[tok, st_out],
        out_shape=[jax.ShapeDtypeStruct((bsz, l, WIDTH), BF16),
                   jax.ShapeDtypeStruct((bsz, PAIRS, LANES, LANES), F32)],
        scratch_shapes=[pltpu.VMEM((PAIRS, LANES, LANES), F32)],
        compiler_params=pltpu.CompilerParams(
            dimension_semantics=("parallel", "arbitrary"), vmem_limit_bytes=VMEM_LIMIT),
        name="wkv",
    )(r, lw, k, v, a, b, g, s0, row(gn_w), row(gn_b), row(r_k))


def _fox_prep_kernel(z_ref, zs_ref, c0_ref, qg_ref, kg_ref, fb_ref,
                     qt_out, k_out, vt_out, c_out, carry, *, tt, n_pad):
    i = pl.program_id(1)

    @pl.when(i == 0)
    def _():
        carry[...] = c0_ref[...]

    z = z_ref[0]
    fl = zs_ref[0][:, 0:LANES]
    log_f = -_softplus(-(fl + fb_ref[...]))
    ti = lax.broadcasted_iota(jnp.int32, (tt, tt), 0)
    tj = lax.broadcasted_iota(jnp.int32, (tt, tt), 1)
    ltri = jnp.where(tj <= ti, 1.0, 0.0).astype(BF16)
    hi, mid, lo = _split3(log_f)
    cum = carry[...] + _dot(jnp.concatenate([ltri, ltri, ltri], axis=1),
                            jnp.concatenate([hi, mid, lo], axis=0))
    carry[...] = cum[tt - 1:tt, :]
    c_out[0] = cum[tt - 8:tt, :]
    c_hi, c_mid, c_lo = (t.astype(F32) for t in _split3(cum * LOG2E))
    neg_hi, neg_mid, neg_lo = -c_hi, -c_mid, -c_lo
    if n_pad:
        grow = i * tt + lax.broadcasted_iota(jnp.int32, (tt, 1), 0)
        neg_hi = jnp.where(grow < n_pad, MASKED_KEY, neg_hi)

    for p in range(PAIRS):
        vt_out[0, p, 0] = z[:, 2 * WIDTH + p * LANES:2 * WIDTH + (p + 1) * LANES].T.astype(BF16)
    ones_bd = _head_ones()
    lane = lax.broadcasted_iota(jnp.int32, (1, LANES), 1)
    inv_n = 1.0 / HEAD_DIM
    for p in range(PAIRS):
        zq = z[:, p * LANES:(p + 1) * LANES]
        zk = z[:, WIDTH + p * LANES:WIDTH + (p + 1) * LANES]
        qn = zq * lax.rsqrt(_head_sum(zq * zq, ones_bd) * inv_n + RMS_EPS) * (qg_ref[...] * (ATTN_SCALE * LOG2E))
        kn = zk * lax.rsqrt(_head_sum(zk * zk, ones_bd) * inv_n + RMS_EPS) * kg_ref[...]
        for j in range(2):
            h = 2 * p + j
            own = (lane < HEAD_DIM) if j == 0 else (lane >= HEAD_DIM)
            e = lane - (HEAD_DIM if j == 0 else 0)
            col = lambda t: t[:, FL_LANE + h:FL_LANE + h + 1]
            q_extra = jnp.where((e >= 0) & (e < 3), 1.0, 0.0)
            k_extra = jnp.where(e == 0, col(neg_hi), jnp.where(e == 1, col(neg_mid), jnp.where(
                e == 2, col(neg_lo), 0.0)))
            qt_out[0, h, 0] = jnp.where(own, qn, q_extra).T.astype(BF16)
            k_out[0, h] = jnp.where(own, kn, k_extra).astype(BF16)


def _fox_prep(proj3d, c0, q_g, k_g, f_bias, *, tt, n_pad):
    b, l, _ = proj3d.shape
    zw = 3 * WIDTH
    tile2 = lambda g: jnp.concatenate([g, g]).reshape(1, LANES)
    fb = jnp.concatenate([jnp.zeros((FL_LANE,), F32), f_bias,
                          jnp.zeros((LANES - FL_LANE - HEADS,), F32)]).reshape(1, LANES)
    hd = pl.BlockSpec((1, HEADS, tt, LANES), lambda bi, i: (bi, 0, i, 0))
    vec = pl.BlockSpec((1, LANES), lambda bi, i: (0, 0))
    return pl.pallas_call(
        functools.partial(_fox_prep_kernel, tt=tt, n_pad=n_pad),
        grid=(b, l // tt),
        in_specs=[pl.BlockSpec((1, tt, zw), lambda bi, i: (bi, i, C_FQ // zw)),
                  pl.BlockSpec((1, tt, S_COLS), lambda bi, i: (bi, i, C_S // S_COLS)),
                  vec, vec, vec, vec],
        out_specs=[pl.BlockSpec((1, HEADS, 1, LANES, tt), lambda bi, i: (bi, 0, i, 0, 0)),
                   hd,
                   pl.BlockSpec((1, PAIRS, 1, LANES, tt), lambda bi, i: (bi, 0, i, 0, 0)),
                   pl.BlockSpec((1, 8, LANES), lambda bi, i: (bi, i, 0))],
        out_shape=[jax.ShapeDtypeStruct((b, HEADS, l // tt, LANES, tt), BF16),
                   jax.ShapeDtypeStruct((b, HEADS, l, LANES), BF16),
                   jax.ShapeDtypeStruct((b, PAIRS, l // tt, LANES, tt), BF16),
                   jax.ShapeDtypeStruct((b, (l // tt) * 8, LANES), F32)],
        scratch_shapes=[pltpu.VMEM((1, LANES), F32)],
        compiler_params=pltpu.CompilerParams(
            dimension_semantics=("parallel", "arbitrary"), vmem_limit_bytes=VMEM_LIMIT),
        name="fox_prep",
    )(proj3d, proj3d, c0, tile2(q_g), tile2(k_g), fb)


ATTN_SUBTILES = 4


def _attn_kernel(qt_ref, k_ref, vt_ref, km_ref, vmt_ref, o_ref, s0_buf, s1_buf, p0_buf, p1_buf,
                 acc_buf, *, tk):
    hq = ATTN_SUBTILES
    s_buf = (s0_buf, s1_buf)
    p_buf = (p0_buf, p1_buf)
    qi = pl.program_id(2)
    key_i = lax.broadcasted_iota(jnp.int32, (tk, tk), 0)
    qry_i = lax.broadcasted_iota(jnp.int32, (tk, tk), 1)
    causal = key_i <= qry_i
    every = tuple(range(2 * hq))
    from_sub = lambda u0: tuple(n for n in every if n % hq >= u0)
    qts = [qt_ref[0, n // hq, n % hq] for n in every]

    def real_keys(r):
        off = pl.multiple_of(r * tk, tk)
        return (k_ref[0, 0, pl.ds(off, tk), :], k_ref[0, 1, pl.ds(off, tk), :])

    def stage_a(kblks, slot, which):
        for n in which:
            s_buf[slot][n] = _dot(kblks[n // hq], qts[n])

    def stage_b(slot, stats, alpha, masks):
        which = tuple(masks)
        s = {n: s_buf[slot][n] for n in which}
        for n in which:
            if masks[n] is not None:
                s[n] = jnp.where(masks[n], s[n], MASKED_KEY)
        m_new = {n: jnp.maximum(stats[n][0], jnp.max(s[n], axis=0, keepdims=True)) for n in which}
        a_new = {n: jnp.exp2(stats[n][0] - m_new[n]) for n in which}
        p = {n: jnp.exp2(s[n] - m_new[n]) for n in which}
        for n in which:
            p_buf[slot][n] = p[n].astype(BF16)
        l_new = {n: a_new[n] * stats[n][1] + jnp.sum(p[n], axis=0, keepdims=True) for n in which}
        stats = tuple((m_new[n], l_new[n]) if n in which else stats[n] for n in every)
        alpha = tuple(a_new[n] if n in which else alpha[n] for n in every)
        return stats, alpha

    def stage_c(slot, e, alpha, which):
        vt = jnp.where(e == 0, vmt_ref[0, 0, 0], vt_ref[0, 0, jnp.maximum(e - 1, 0)])
        pv = {n: _dot(vt[(n // hq) * HEAD_DIM:(n // hq + 1) * HEAD_DIM, :], p_buf[slot][n])
              for n in which}
        for n in which:
            acc_buf[n] = alpha[n] * acc_buf[n] + pv[n]

    unmasked = {n: None for n in every}
    stats = ((jnp.full((1, tk), MASKED_KEY, F32), jnp.zeros((1, tk), F32)),) * len(every)
    alpha = (jnp.ones((1, tk), F32),) * len(every)
    acc_buf[...] = jnp.zeros_like(acc_buf)
    p1_buf[...] = jnp.zeros_like(p1_buf)
    stage_a((km_ref[0, 0], km_ref[0, 1]), 0, every)

    def body(i, state):
        stats, alpha_prev = state
        stats, alpha_a = stage_b(0, stats, alpha_prev, unmasked)
        stage_a(real_keys(2 * i), 1, every)
        stage_c(1, jnp.maximum(2 * i - 1, 0), alpha_prev, every)
        stage_a(real_keys(2 * i + 1), 0, every)
        stats, alpha_b = stage_b(1, stats, alpha_a, unmasked)
        stage_c(0, 2 * i, alpha_a, every)
        return stats, alpha_b

    stats, alpha_prev = lax.fori_loop(0, (hq // 2) * qi, body, (stats, alpha))
    e0 = hq * qi
    alphas = {-1: alpha_prev}
    for r in range(1, hq + 3):
        if r <= hq:
            stage_a(real_keys(e0 + r - 1), r % 2, from_sub(r - 1))
        j = r - 1
        if j <= hq:
            masks = unmasked if j == 0 else {n: (causal if n % hq == j - 1 else None)
                                             for n in from_sub(j - 1)}
            stats, alphas[j] = stage_b(j % 2, stats, alphas[j - 1], masks)
        j = r - 2
        if j == -1:
            stage_c(1, jnp.maximum(e0 - 1, 0), alphas[-1], every)
        else:
            stage_c(j % 2, e0 + j, alphas[j], every if j == 0 else from_sub(j - 1))
    for u in range(hq):
        out_t = jnp.concatenate([acc_buf[u] / stats[u][1], acc_buf[hq + u] / stats[hq + u][1]], axis=0)
        o_ref[0, u * tk:(u + 1) * tk, :] = out_t.T.astype(o_ref.dtype)


def _attn(qt, kp, vt, kp_meta, vt_meta, *, tk):
    b, _, l, _ = kp.shape
    nk = vt.shape[2]
    hq = ATTN_SUBTILES
    assert vt.shape[4] == tk and qt.shape[4] == tk and kp_meta.shape[2] == tk and l % (hq * tk) == 0
    return pl.pallas_call(
        functools.partial(_attn_kernel, tk=tk),
        grid=(b, PAIRS, l // (hq * tk)),
        in_specs=[pl.BlockSpec((1, 2, hq, LANES, tk), lambda bi, p, qi: (bi, p, qi, 0, 0)),
                  pl.BlockSpec((1, 2, l, LANES), lambda bi, p, qi: (bi, p, 0, 0)),
                  pl.BlockSpec((1, 1, nk, LANES, tk), lambda bi, p, qi: (bi, p, 0, 0, 0)),
                  pl.BlockSpec((1, 2, tk, LANES), lambda bi, p, qi: (0, p, 0, 0)),
                  pl.BlockSpec((1, 1, 1, LANES, tk), lambda bi, p, qi: (0, p, 0, 0, 0))],
        out_specs=pl.BlockSpec((1, hq * tk, LANES), lambda bi, p, qi: (bi, qi, p)),
        out_shape=jax.ShapeDtypeStruct((b, l, WIDTH), BF16),
        scratch_shapes=[pltpu.VMEM((2 * hq, tk, tk), F32), pltpu.VMEM((2 * hq, tk, tk), F32),
                        pltpu.VMEM((2 * hq, tk, tk), BF16), pltpu.VMEM((2 * hq, tk, tk), BF16),
                        pltpu.VMEM((2 * hq, HEAD_DIM, tk), F32)],
        compiler_params=pltpu.CompilerParams(
            dimension_semantics=("parallel", "parallel", "arbitrary"), vmem_limit_bytes=VMEM_LIMIT),
        name="attn",
    )(qt, kp, vt, kp_meta, vt_meta)


def _pick(n, prefs):
    for t in prefs:
        if n % t == 0:
            return t
    raise ValueError(f"no tile for {n}")


def _pack_w_kernel(w_ref, o_ref):
    w = w_ref[...].astype(BF16)
    rw, fx = 3 * WIDTH + DECAY_LORA + AAA_LORA + GATE_LORA, 3 * WIDTH + HEADS
    wd0, ad0, gd0 = 3 * WIDTH, 3 * WIDTH + DECAY_LORA, 3 * WIDTH + DECAY_LORA + AAA_LORA
    fl0 = rw + 3 * WIDTH
    lane = lax.broadcasted_iota(jnp.int32, (1, LANES), 1)
    blk_wd_fl = jnp.where(lane < DECAY_LORA, w[:, wd0:wd0 + LANES],
                          jnp.where(lane < DECAY_LORA + HEADS,
                                    w[:, fl0 - FL_LANE:fl0 - FL_LANE + LANES], 0.0))
    blk_ad = jnp.where(lane < AAA_LORA, w[:, ad0:ad0 + LANES], 0.0)
    o_ref[...] = jnp.concatenate([
        w[:, 0:3 * WIDTH], w[:, rw:rw + 3 * WIDTH], w[:, rw + fx:],
        blk_wd_fl, blk_ad, w[:, gd0:rw]], axis=1).astype(BF16)


def _pack_w_in(w_in, *, rows=128):
    d, n = w_in.shape
    return pl.pallas_call(
        _pack_w_kernel,
        grid=(d // rows,),
        in_specs=[pl.BlockSpec((rows, n), lambda i: (i, 0))],
        out_specs=pl.BlockSpec((rows, N_PACK), lambda i: (i, 0)),
        out_shape=jax.ShapeDtypeStruct((d, N_PACK), BF16),
        compiler_params=pltpu.CompilerParams(
            dimension_semantics=("parallel",), vmem_limit_bytes=VMEM_LIMIT),
        name="pack_w_in",
    )(w_in)


def _pack_mu(mu):
    wd0, ad0, gd0 = 3 * WIDTH, 3 * WIDTH + DECAY_LORA, 3 * WIDTH + DECAY_LORA + AAA_LORA
    z = lambda n: jnp.zeros((n,), mu.dtype)
    mu_s = jnp.concatenate([mu[wd0:ad0], z(LANES - DECAY_LORA), mu[ad0:gd0], z(LANES - AAA_LORA),
                            mu[gd0:]])
    return mu[:3 * WIDTH], mu_s


def _pad_rows(w, rows):
    return jnp.concatenate([w, jnp.zeros((rows - w.shape[0], w.shape[1]), w.dtype)], axis=0)


def _mixer_inputs(h2d, bsz, norm_g, w_pack, tm, tn):
    proj = _rms_matmul(h2d, norm_g, w_pack, tm=tm, tn=tn)
    return proj, proj.reshape(bsz, h2d.shape[0] // bsz, N_PACK)


def kernel(x, meta_tokens, norm1_g, w_in, rwkv_mu, rwkv_w0, rwkv_w2, rwkv_a0, rwkv_a2, rwkv_g2,
           rwkv_k_k, rwkv_k_a, rwkv_r_k, rwkv_gn_w, rwkv_gn_b, fox_q_norm_g, fox_k_norm_g,
           fox_f_bias, w_branch_a, w_branch_b, w_o, norm2_g, w_gate_up, w_down):
    bsz, seq, d = x.shape
    assert d == D_MODEL and norm1_g.shape[0] == 1 and seq % (ATTN_SUBTILES * 256) == 0
    m = bsz * seq
    x2d = x.reshape(m, d)
    meta2d = jnp.concatenate(
        [jnp.zeros((META_ROWS - N_META, d), x.dtype), meta_tokens.astype(x.dtype)], axis=0)

    w_pack = _pack_w_in(w_in[0])
    mu_z, mu_s = _pack_mu(rwkv_mu[0])
    w2p = _pad_rows(rwkv_w2[0], LANES).astype(BF16)
    a2p = _pad_rows(rwkv_a2[0], LANES).astype(BF16)
    g2 = rwkv_g2[0].astype(BF16)
    r_k = rwkv_r_k[0].reshape(WIDTH)

    tm = _pick(m, (1024, 512, 256))
    tt = _pick(seq, (256,))

    _, pm3 = _mixer_inputs(meta2d, 1, norm1_g[0], w_pack, META_ROWS, tn=1536)
    zero_z = jnp.zeros((8, 3 * WIDTH), F32)
    zero_s = jnp.zeros((8, S_COLS), F32)
    prep = functools.partial(_rwkv_prep, mu_z=mu_z, mu_s=mu_s, w0=rwkv_w0[0], w2p=w2p,
                             a0=rwkv_a0[0], a2p=a2p, g2=g2, k_k=rwkv_k_k[0], k_a=rwkv_k_a[0])
    wkv = functools.partial(_wkv, gn_w=rwkv_gn_w[0], gn_b=rwkv_gn_b[0], r_k=r_k)
    fox = functools.partial(_fox_prep, q_g=fox_q_norm_g[0], k_g=fox_k_norm_g[0],
                            f_bias=fox_f_bias[0])
    mr = prep(pm3, zero_z, zero_s, tt=META_ROWS)
    _, s_meta = wkv(*mr, jnp.zeros((1, PAIRS, LANES, LANES), F32))
    _, kp_meta, vt_meta, c_meta = fox(pm3, jnp.zeros((1, LANES), F32), tt=META_ROWS,
                                      n_pad=META_ROWS - N_META)

    proj2d, p3 = _mixer_inputs(x2d, bsz, norm1_g[0], w_pack, tm, tn=1536)
    rr = prep(p3, pm3[0, META_ROWS - 8:, 0:3 * WIDTH], pm3[0, META_ROWS - 8:, C_S:], tt=tt)
    ya, _ = wkv(*rr, s_meta)
    qt, kp, vt, _ = fox(p3, c_meta[0, 7:8, :], tt=tt, n_pad=0)
    yb = _attn(qt, kp, vt, kp_meta, vt_meta, tk=tt)

    merged = _merge(ya.reshape(m, WIDTH), yb.reshape(m, WIDTH), w_branch_a[0].astype(BF16),
                    w_branch_b[0].astype(BF16), proj2d, tm=tm // 2, tn=D_MODEL)
    h1 = _matmul_res(merged, w_o[0].astype(BF16), x2d, tm=tm, tn=1024)
    act = _rms_matmul_glu(h1, norm2_g[0], w_gate_up[0].astype(BF16), tm=tm, tn=512)
    out = _matmul_res(act, w_down[0].astype(BF16), h1, tm=tm, tn=512)
    return out.reshape(bsz, seq, d)
```

```python
import functools

import jax
import jax.numpy as jnp
from jax import lax
from jax.experimental import pallas as pl
from jax.experimental.pallas import tpu as pltpu

F32 = jnp.float32
BF16 = jnp.bfloat16

D_MODEL = 2048
N_META = 16
HEAD_DIM = 64
HEADS = 16
WIDTH = HEADS * HEAD_DIM
PAIRS = HEADS // 2
DECAY_LORA = 96
AAA_LORA = 96
GATE_LORA = 256
D_FF = 5632
RMS_EPS = 1e-6
GN_EPS = 64e-5
ATTN_SCALE = HEAD_DIM ** -0.5
LOG2E = 1.4426950408889634

LANES = 128
META_ROWS = 256
CHUNK = 64
MASKED_KEY = -1e30

C_R, C_K, C_V = 0, 1024, 2048
C_FQ, C_FK, C_FV = 3072, 4096, 5120
C_GA, C_GB = 6144, 8192
C_S = 10240
S_COLS = 512
FL_LANE = 96
N_PACK = C_S + S_COLS

VMEM_LIMIT = 56 * 1024 * 1024


def _sigmoid(x):
    return 0.5 * jnp.tanh(0.5 * x) + 0.5


def _softplus(x):
    return jnp.maximum(x, 0.0) + jnp.log(1.0 + jnp.exp(-jnp.abs(x)))


def _dot(a, b):
    return jnp.dot(a, b, preferred_element_type=F32)


def _dot_nt(a, b):
    return lax.dot_general(a, b, (((1,), (1,)), ((), ())), preferred_element_type=F32)


def _dot_tn(a, b):
    return lax.dot_general(a, b, (((0,), (0,)), ((), ())), preferred_element_type=F32)


def _split3(x):
    hi = x.astype(BF16)
    r1 = x - hi.astype(F32)
    mid = r1.astype(BF16)
    lo = (r1 - mid.astype(F32)).astype(BF16)
    return hi, mid, lo


_NN = (((1,), (0,)), ((), ()))
_NT = (((1,), (1,)), ((), ()))
_TN = (((0,), (0,)), ((), ()))


def _head_ones():
    r = lax.broadcasted_iota(jnp.int32, (LANES, LANES), 0) // HEAD_DIM
    c = lax.broadcasted_iota(jnp.int32, (LANES, LANES), 1) // HEAD_DIM
    return jnp.where(r == c, 1.0, 0.0).astype(BF16)


def _head_sum(x, ones_bd):
    hi = x.astype(BF16)
    lo = (x - hi.astype(F32)).astype(BF16)
    return _dot(jnp.concatenate([hi, lo], axis=1), jnp.concatenate([ones_bd, ones_bd], axis=0))


def _rms_mm_kernel(x_ref, g_ref, wt_ref, o_ref, xn_ref):
    @pl.when(pl.program_id(1) == 0)
    def _():
        x = x_ref[...]
        ms = jnp.mean(x * x, axis=-1, keepdims=True)
        xn_ref[...] = ((x * lax.rsqrt(ms + RMS_EPS)) * g_ref[...]).astype(BF16)

    o_ref[...] = _dot_nt(xn_ref[...], wt_ref[...]).astype(o_ref.dtype)


def _rms_glu_kernel(x_ref, g_ref, wg_ref, wu_ref, o_ref, xn_ref):
    @pl.when(pl.program_id(1) == 0)
    def _():
        x = x_ref[...]
        ms = jnp.mean(x * x, axis=-1, keepdims=True)
        xn_ref[...] = ((x * lax.rsqrt(ms + RMS_EPS)) * g_ref[...]).astype(BF16)

    xn = xn_ref[...]
    gate = _dot(xn, wg_ref[...])
    up = _dot(xn, wu_ref[...])
    o_ref[...] = (gate * _sigmoid(gate) * up).astype(o_ref.dtype)


def _rms_matmul(x2d, g, wt_bf16, *, tm, tn, out_dtype=F32):
    m, d = x2d.shape
    n = wt_bf16.shape[0]
    return pl.pallas_call(
        _rms_mm_kernel,
        grid=(m // tm, n // tn),
        in_specs=[pl.BlockSpec((tm, d), lambda i, j: (i, 0)),
                  pl.BlockSpec((1, d), lambda i, j: (0, 0)),
                  pl.BlockSpec((tn, d), lambda i, j: (j, 0))],
        out_specs=pl.BlockSpec((tm, tn), lambda i, j: (i, j)),
        out_shape=jax.ShapeDtypeStruct((m, n), out_dtype),
        scratch_shapes=[pltpu.VMEM((tm, d), BF16)],
        compiler_params=pltpu.CompilerParams(
            dimension_semantics=("parallel", "arbitrary"), vmem_limit_bytes=VMEM_LIMIT),
        name="rms_matmul",
    )(x2d, g.reshape(1, d), wt_bf16)


def _rms_matmul_glu(x2d, g, w_bf16, *, tm, tn):
    m, d = x2d.shape
    n = w_bf16.shape[1] // 2
    up_off = n // tn
    return pl.pallas_call(
        _rms_glu_kernel,
        grid=(m // tm, n // tn),
        in_specs=[pl.BlockSpec((tm, d), lambda i, j: (i, 0)),
                  pl.BlockSpec((1, d), lambda i, j: (0, 0)),
                  pl.BlockSpec((d, tn), lambda i, j: (0, j)),
                  pl.BlockSpec((d, tn), lambda i, j: (0, j + up_off))],
        out_specs=pl.BlockSpec((tm, tn), lambda i, j: (i, j)),
        out_shape=jax.ShapeDtypeStruct((m, n), BF16),
        scratch_shapes=[pltpu.VMEM((tm, d), BF16)],
        compiler_params=pltpu.CompilerParams(
            dimension_semantics=("parallel", "arbitrary"), vmem_limit_bytes=VMEM_LIMIT),
        name="rms_matmul_glu",
    )(x2d, g.reshape(1, d), w_bf16, w_bf16)


def _mm_res_kernel(a_ref, b_ref, r_ref, o_ref):
    o_ref[...] = _dot(a_ref[...], b_ref[...]) + r_ref[...]


def _matmul_res(a, b, res, *, tm, tn):
    m, kk = a.shape
    n = b.shape[1]
    return pl.pallas_call(
        _mm_res_kernel,
        grid=(m // tm, n // tn),
        in_specs=[pl.BlockSpec((tm, kk), lambda i, j: (i, 0)),
                  pl.BlockSpec((kk, tn), lambda i, j: (0, j)),
                  pl.BlockSpec((tm, tn), lambda i, j: (i, j))],
        out_specs=pl.BlockSpec((tm, tn), lambda i, j: (i, j)),
        out_shape=jax.ShapeDtypeStruct((m, n), F32),
        compiler_params=pltpu.CompilerParams(
            dimension_semantics=("parallel", "arbitrary"), vmem_limit_bytes=VMEM_LIMIT),
        name="matmul_res",
    )(a, b, res)


def _merge_kernel(ya_ref, yb_ref, wa_ref, wb_ref, za_ref, zb_ref, o_ref):
    ta = _dot(ya_ref[...], wa_ref[...])
    tb = _dot(yb_ref[...], wb_ref[...])
    o_ref[...] = (_sigmoid(za_ref[...]) * ta + _sigmoid(zb_ref[...]) * tb).astype(o_ref.dtype)


def _merge(ya, yb, wa, wb, proj2d, *, tm, tn):
    m, kk = ya.shape
    n = wa.shape[1]
    oa, ob = C_GA // tn, C_GB // tn
    return pl.pallas_call(
        _merge_kernel,
        grid=(m // tm, n // tn),
        in_specs=[pl.BlockSpec((tm, kk), lambda i, j: (i, 0)),
                  pl.BlockSpec((tm, kk), lambda i, j: (i, 0)),
                  pl.BlockSpec((kk, tn), lambda i, j: (0, j)),
                  pl.BlockSpec((kk, tn), lambda i, j: (0, j)),
                  pl.BlockSpec((tm, tn), lambda i, j: (i, j + oa)),
                  pl.BlockSpec((tm, tn), lambda i, j: (i, j + ob))],
        out_specs=pl.BlockSpec((tm, tn), lambda i, j: (i, j)),
        out_shape=jax.ShapeDtypeStruct((m, n), BF16),
        compiler_params=pltpu.CompilerParams(
            dimension_semantics=("parallel", "arbitrary"), vmem_limit_bytes=VMEM_LIMIT),
        name="merge",
    )(ya, yb, wa, wb, proj2d, proj2d)


def _rwkv_prep_kernel(z_ref, zs_ref, pz_ref, ps_ref, p0z_ref, p0s_ref, mu_ref, mus_ref,
                      w0_ref, w2_ref, a0_ref, a2_ref, g2_ref, kk_ref, ka_ref,
                      r_out, lw_out, k_out, v_out, a_out, b_out, g_out, *, tt):
    first = pl.program_id(1) == 0
    z = z_ref[0]
    zs = zs_ref[0]
    pz = jnp.where(first, p0z_ref[7:8, :], pz_ref[0, 7:8, :])
    ps = jnp.where(first, p0s_ref[7:8, :], ps_ref[0, 7:8, :])
    row = lax.broadcasted_iota(jnp.int32, (tt, 1), 0)

    def lerp(cur, prev_last, mu):
        prev = jnp.where(row == 0, prev_last, pltpu.roll(cur, 1, axis=0))
        return cur + (prev - cur) * mu

    zm = lerp(z, pz, mu_ref[...])
    zsm = lerp(zs, ps, mus_ref[...])
    r = zm[:, C_R:C_R + WIDTH]
    k = zm[:, C_K:C_K + WIDTH]
    v = zm[:, C_V:C_V + WIDTH]
    wd = zsm[:, 0:LANES]
    ad = zsm[:, LANES:2 * LANES]
    gd = zsm[:, 2 * LANES:4 * LANES]

    w_log = -_softplus(-(w0_ref[...] + _dot(jnp.tanh(wd).astype(BF16), w2_ref[...]))) - 0.5
    lw_out[0] = -jnp.exp(w_log)
    a_sig = _sigmoid(a0_ref[...] + _dot(ad.astype(BF16), a2_ref[...]))
    g_out[0] = _dot(_sigmoid(gd).astype(BF16), g2_ref[...]).astype(g_out.dtype)

    ones_bd = _head_ones()
    kk = k * kk_ref[...]
    kk2 = kk * kk
    nrm = jnp.concatenate(
        [_head_sum(kk2[:, p * LANES:(p + 1) * LANES], ones_bd) for p in range(PAIRS)], axis=1)
    kk = kk * lax.rsqrt(jnp.maximum(nrm, 1e-24))
    r_out[0] = r.astype(r_out.dtype)
    k_out[0] = (k * (1.0 + (a_sig - 1.0) * ka_ref[...])).astype(k_out.dtype)
    v_out[0] = v.astype(v_out.dtype)
    a_out[0] = (-kk).astype(a_out.dtype)
    b_out[0] = (kk * a_sig).astype(b_out.dtype)


def _rwkv_prep(proj3d, prev0_z, prev0_s, mu_z, mu_s, w0, w2p, a0, a2p, g2, k_k, k_a, *, tt):
    b, l, _ = proj3d.shape
    s_blk = C_S // S_COLS
    zw = 3 * WIDTH
    row = lambda a: a.reshape(1, -1)
    vec = lambda n: pl.BlockSpec((1, n), lambda bi, i: (0, 0))
    full = lambda r, c: pl.BlockSpec((r, c), lambda bi, i: (0, 0))
    out_spec = pl.BlockSpec((1, tt, WIDTH), lambda bi, i: (bi, i, 0))
    sds = lambda dt: jax.ShapeDtypeStruct((b, l, WIDTH), dt)
    return pl.pallas_call(
        functools.partial(_rwkv_prep_kernel, tt=tt),
        grid=(b, l // tt),
        in_specs=[pl.BlockSpec((1, tt, zw), lambda bi, i: (bi, i, 0)),
                  pl.BlockSpec((1, tt, S_COLS), lambda bi, i: (bi, i, s_blk)),
                  pl.BlockSpec((1, 8, zw), lambda bi, i: (bi, jnp.maximum(i * (tt // 8) - 1, 0), 0)),
                  pl.BlockSpec((1, 8, S_COLS),
                               lambda bi, i: (bi, jnp.maximum(i * (tt // 8) - 1, 0), s_blk)),
                  full(8, zw), full(8, S_COLS),
                  vec(zw), vec(S_COLS),
                  vec(WIDTH), full(LANES, WIDTH), vec(WIDTH), full(LANES, WIDTH),
                  full(GATE_LORA, WIDTH), vec(WIDTH), vec(WIDTH)],
        out_specs=[out_spec] * 7,
        out_shape=[sds(BF16), sds(F32)] + [sds(BF16)] * 5,
        compiler_params=pltpu.CompilerParams(
            dimension_semantics=("parallel", "arbitrary"), vmem_limit_bytes=VMEM_LIMIT),
        name="rwkv_prep",
    )(proj3d, proj3d, proj3d, proj3d, prev0_z, prev0_s, row(mu_z), row(mu_s),
      row(w0), w2p, row(a0), a2p, g2, row(k_k), row(k_a))


def _bf(x):
    return x.astype(BF16)


WKV_CHUNKS = 4


def _wkv_group(a_t, r_t, b_t, k_t, b_h, k_h, v, s, w_total, masks):
    head0, strict, lower, eye = masks
    dg = functools.partial(lax.dot_general, preferred_element_type=F32)
    n = range(len(s))

    def bd(x):
        return jnp.concatenate([jnp.where(head0, x, 0.0), jnp.where(head0, 0.0, x)], axis=0)

    lhs = [_bf(jnp.concatenate([bd(a_t[i]), bd(r_t[i])], axis=0)) for i in n]
    rhs = [_bf(jnp.concatenate([bd(b_t[i]), bd(k_t[i])], axis=0)) for i in n]
    v_bd = [_bf(bd(v[i])) for i in n]
    gram = [dg(lhs[i], rhs[i], _NT) for i in n]
    a_s = [dg(lhs[i], _bf(s[i]), _NT) for i in n]
    n_ab = [jnp.where(strict, g[0:LANES, 0:LANES], 0.0) for g in gram]
    a_ak = [jnp.where(strict, g[0:LANES, LANES:2 * LANES], 0.0) for g in gram]
    a_r = [_bf(jnp.concatenate([jnp.where(lower, g[LANES:2 * LANES, 0:LANES], 0.0),
                                jnp.where(lower, g[LANES:2 * LANES, LANES:2 * LANES], 0.0)], axis=1))
           for g in gram]
    rhs_u = [a_s[i][0:LANES] + dg(_bf(a_ak[i]), v_bd[i], _NN) for i in n]

    p_inv = [eye + x for x in n_ab]
    q = [_bf(x) for x in n_ab]
    for _ in range(5):
        q = [_bf(dg(x, x, _NN)) for x in q]
        p_inv = [p_inv[i] + dg(q[i], _bf(p_inv[i]), _NN) for i in n]
    u = [dg(_bf(p_inv[i]), _bf(rhs_u[i]), _NN) for i in n]

    uv = [jnp.concatenate([_bf(u[i]), v_bd[i]], axis=0) for i in n]
    y_bd = [a_s[i][LANES:2 * LANES] + dg(a_r[i], uv[i], _NN) for i in n]
    y = [x[0:CHUNK] + x[CHUNK:2 * CHUNK] for x in y_bd]
    s_new = [w_total[i] * s[i]
             + dg(uv[i], _bf(jnp.concatenate([bd(b_h[i]), bd(k_h[i])], axis=0)), _TN) for i in n]
    return y, s_new


def _wkv_kernel(r_ref, lw_ref, k_ref, v_ref, a_ref, b_ref, g_ref, s0_ref,
                gnw_ref, gnb_ref, rk_ref, y_ref, sfin_ref, s_scr, *, nc):
    c = pl.program_id(1)

    @pl.when(c == 0)
    def _():
        s_scr[...] = s0_ref[0]

    rows_all = WKV_CHUNKS * CHUNK
    r = r_ref[0].astype(F32)
    lw = lw_ref[0]
    k = k_ref[0].astype(F32)
    v = v_ref[0].astype(F32)
    a = a_ref[0].astype(F32)
    b = b_ref[0].astype(F32)

    ti = lax.broadcasted_iota(jnp.int32, (rows_all, rows_all), 0)
    tj = lax.broadcasted_iota(jnp.int32, (rows_all, rows_all), 1)
    ltri = jnp.where((tj <= ti) & (ti // CHUNK == tj // CHUNK), 1.0, 0.0).astype(BF16)
    hi, mid, lo = _split3(lw)
    cum = _dot(jnp.concatenate([ltri, ltri, ltri], axis=1),
               jnp.concatenate([hi, mid, lo], axis=0))

    lane = lax.broadcasted_iota(jnp.int32, (1, LANES), 1)
    bi = lax.broadcasted_iota(jnp.int32, (LANES, LANES), 0)
    bj = lax.broadcasted_iota(jnp.int32, (LANES, LANES), 1)
    same = (bi // CHUNK) == (bj // CHUNK)
    masks = (lane < HEAD_DIM,
             same & ((bj % CHUNK) < (bi % CHUNK)),
             same & ((bj % CHUNK) <= (bi % CHUNK)),
             jnp.where(bi == bj, 1.0, 0.0))

    ones_bd = _head_ones()
    inv_n = 1.0 / HEAD_DIM
    sls = [slice(p * LANES, (p + 1) * LANES) for p in range(PAIRS)]
    cut = lambda x: [x[:, sl] for sl in sls]
    state = [s_scr[p] for p in range(PAIRS)]
    for sc in range(WKV_CHUNKS):
        rows = slice(sc * CHUNK, (sc + 1) * CHUNK)
        cum_c, lw_c = cum[rows], lw[rows]
        total = cum_c[CHUNK - 1:CHUNK, :]
        e_inv = jnp.exp(-cum_c)
        e_end = jnp.exp(total - cum_c)
        ys, state = _wkv_group(
            cut(a[rows] * jnp.exp(cum_c - lw_c)), cut(r[rows] * jnp.exp(cum_c)),
            cut(b[rows] * e_inv), cut(k[rows] * e_inv), cut(b[rows] * e_end), cut(k[rows] * e_end),
            cut(v[rows]), state, cut(jnp.exp(total)), masks)

        y_all = jnp.concatenate(ys, axis=0)
        rk_all = jnp.concatenate([r[rows, sl] * k[rows, sl] * rk_ref[:, sl] for sl in sls], axis=0)
        bonus_all = _head_sum(rk_all, ones_bd)
        d_all = y_all - _head_sum(y_all, ones_bd) * inv_n
        rstd_all = lax.rsqrt(_head_sum(d_all * d_all, ones_bd) * inv_n + GN_EPS)
        for i, sl in enumerate(sls):
            prow = slice(i * CHUNK, (i + 1) * CHUNK)
            yn = d_all[prow] * rstd_all[prow] * gnw_ref[:, sl] + gnb_ref[:, sl]
            y_ref[0, rows, sl] = ((yn + bonus_all[prow] * v[rows, sl])
                                  * g_ref[0, rows, sl].astype(F32)).astype(y_ref.dtype)

    for p in range(PAIRS):
        s_scr[p] = state[p]

        @pl.when(c == nc - 1)
        def _(p=p):
            sfin_ref[0, p] = state[p]


def _wkv(r, lw, k, v, a, b, g, s0, gn_w, gn_b, r_k):
    bsz, l, _ = r.shape
    nc = l // (WKV_CHUNKS * CHUNK)
    tok = pl.BlockSpec((1, WKV_CHUNKS * CHUNK, WIDTH), lambda bi, c: (bi, c, 0))
    vec = pl.BlockSpec((1, WIDTH), lambda bi, c: (0, 0))
    st_in = pl.BlockSpec((1, PAIRS, LANES, LANES), lambda bi, c: (0, 0, 0, 0))
    st_out = pl.BlockSpec((1, PAIRS, LANES, LANES), lambda bi, c: (bi, 0, 0, 0))
    row = lambda x: x.reshape(1, WIDTH)
    return pl.pallas_call(
        functools.partial(_wkv_kernel, nc=nc),
        grid=(bsz, nc),
        in_specs=[tok] * 7 + [st_in, vec, vec, vec],
        out_specs=[tok, st_out],
        out_shape=[jax.ShapeDtypeStruct((bsz, l, WIDTH), BF16),
                   jax.ShapeDtypeStruct((bsz, PAIRS, LANES, LANES), F32)],
        scratch_shapes=[pltpu.VMEM((PAIRS, LANES, LANES), F32)],
        compiler_params=pltpu.CompilerParams(
            dimension_semantics=("parallel", "arbitrary"), vmem_limit_bytes=VMEM_LIMIT),
        name="wkv",
    )(r, lw, k, v, a, b, g, s0, row(gn_w), row(gn_b), row(r_k))


def _fox_prep_kernel(z_ref, zs_ref, c0_ref, qg_ref, kg_ref, fb_ref,
                     qt_out, k_out, vt_out, c_out, carry, *, tt, n_pad):
    i = pl.program_id(1)

    @pl.when(i == 0)
    def _():
        carry[...] = c0_ref[...]

    z = z_ref[0]
    fl = zs_ref[0][:, 0:LANES]
    log_f = -_softplus(-(fl + fb_ref[...]))
    ti = lax.broadcasted_iota(jnp.int32, (tt, tt), 0)
    tj = lax.broadcasted_iota(jnp.int32, (tt, tt), 1)
    ltri = jnp.where(tj <= ti, 1.0, 0.0).astype(BF16)
    hi, mid, lo = _split3(log_f)
    cum = carry[...] + _dot(jnp.concatenate([ltri, ltri, ltri], axis=1),
                            jnp.concatenate([hi, mid, lo], axis=0))
    carry[...] = cum[tt - 1:tt, :]
    c_out[0] = cum[tt - 8:tt, :]
    c_hi, c_mid, c_lo = (t.astype(F32) for t in _split3(cum * LOG2E))
    neg_hi, neg_mid, neg_lo = -c_hi, -c_mid, -c_lo
    if n_pad:
        grow = i * tt + lax.broadcasted_iota(jnp.int32, (tt, 1), 0)
        neg_hi = jnp.where(grow < n_pad, MASKED_KEY, neg_hi)

    for p in range(PAIRS):
        vt_out[0, p, 0] = z[:, 2 * WIDTH + p * LANES:2 * WIDTH + (p + 1) * LANES].T.astype(BF16)
    ones_bd = _head_ones()
    lane = lax.broadcasted_iota(jnp.int32, (1, LANES), 1)
    inv_n = 1.0 / HEAD_DIM
    for p in range(PAIRS):
        zq = z[:, p * LANES:(p + 1) * LANES]
        zk = z[:, WIDTH + p * LANES:WIDTH + (p + 1) * LANES]
        qn = zq * lax.rsqrt(_head_sum(zq * zq, ones_bd) * inv_n + RMS_EPS) * (qg_ref[...] * (ATTN_SCALE * LOG2E))
        kn = zk * lax.rsqrt(_head_sum(zk * zk, ones_bd) * inv_n + RMS_EPS) * kg_ref[...]
        for j in range(2):
            h = 2 * p + j
            own = (lane < HEAD_DIM) if j == 0 else (lane >= HEAD_DIM)
            e = lane - (HEAD_DIM if j == 0 else 0)
            col = lambda t: t[:, FL_LANE + h:FL_LANE + h + 1]
            q_extra = jnp.where((e >= 0) & (e < 3), 1.0, 0.0)
            k_extra = jnp.where(e == 0, col(neg_hi), jnp.where(e == 1, col(neg_mid), jnp.where(
                e == 2, col(neg_lo), 0.0)))
            qt_out[0, h, 0] = jnp.where(own, qn, q_extra).T.astype(BF16)
            k_out[0, h] = jnp.where(own, kn, k_extra).astype(BF16)


def _fox_prep(proj3d, c0, q_g, k_g, f_bias, *, tt, n_pad):
    b, l, _ = proj3d.shape
    zw = 3 * WIDTH
    tile2 = lambda g: jnp.concatenate([g, g]).reshape(1, LANES)
    fb = jnp.concatenate([jnp.zeros((FL_LANE,), F32), f_bias,
                          jnp.zeros((LANES - FL_LANE - HEADS,), F32)]).reshape(1, LANES)
    hd = pl.BlockSpec((1, HEADS, tt, LANES), lambda bi, i: (bi, 0, i, 0))
    vec = pl.BlockSpec((1, LANES), lambda bi, i: (0, 0))
    return pl.pallas_call(
        functools.partial(_fox_prep_kernel, tt=tt, n_pad=n_pad),
        grid=(b, l // tt),
        in_specs=[pl.BlockSpec((1, tt, zw), lambda bi, i: (bi, i, C_FQ // zw)),
                  pl.BlockSpec((1, tt, S_COLS), lambda bi, i: (bi, i, C_S // S_COLS)),
                  vec, vec, vec, vec],
        out_specs=[pl.BlockSpec((1, HEADS, 1, LANES, tt), lambda bi, i: (bi, 0, i, 0, 0)),
                   hd,
                   pl.BlockSpec((1, PAIRS, 1, LANES, tt), lambda bi, i: (bi, 0, i, 0, 0)),
                   pl.BlockSpec((1, 8, LANES), lambda bi, i: (bi, i, 0))],
        out_shape=[jax.ShapeDtypeStruct((b, HEADS, l // tt, LANES, tt), BF16),
                   jax.ShapeDtypeStruct((b, HEADS, l, LANES), BF16),
                   jax.ShapeDtypeStruct((b, PAIRS, l // tt, LANES, tt), BF16),
                   jax.ShapeDtypeStruct((b, (l // tt) * 8, LANES), F32)],
        scratch_shapes=[pltpu.VMEM((1, LANES), F32)],
        compiler_params=pltpu.CompilerParams(
            dimension_semantics=("parallel", "arbitrary"), vmem_limit_bytes=VMEM_LIMIT),
        name="fox_prep",
    )(proj3d, proj3d, c0, tile2(q_g), tile2(k_g), fb)


ATTN_SUBTILES = 4


def _attn_kernel(qt_ref, k_ref, vt_ref, km_ref, vmt_ref, o_ref, s0_buf, s1_buf, p0_buf, p1_buf,
                 acc_buf, *, tk):
    hq = ATTN_SUBTILES
    s_buf = (s0_buf, s1_buf)
    p_buf = (p0_buf, p1_buf)
    qi = pl.program_id(2)
    key_i = lax.broadcasted_iota(jnp.int32, (tk, tk), 0)
    qry_i = lax.broadcasted_iota(jnp.int32, (tk, tk), 1)
    causal = key_i <= qry_i
    every = tuple(range(2 * hq))
    from_sub = lambda u0: tuple(n for n in every if n % hq >= u0)
    qts = [qt_ref[0, n // hq, n % hq] for n in every]

    def real_keys(r):
        off = pl.multiple_of(r * tk, tk)
        return (k_ref[0, 0, pl.ds(off, tk), :], k_ref[0, 1, pl.ds(off, tk), :])

    def stage_a(kblks, slot, which):
        for n in which:
            s_buf[slot][n] = _dot(kblks[n // hq], qts[n])

    def stage_b(slot, stats, alpha, masks):
        which = tuple(masks)
        s = {n: s_buf[slot][n] for n in which}
        for n in which:
            if masks[n] is not None:
                s[n] = jnp.where(masks[n], s[n], MASKED_KEY)
        m_new = {n: jnp.maximum(stats[n][0], jnp.max(s[n], axis=0, keepdims=True)) for n in which}
        a_new = {n: jnp.exp2(stats[n][0] - m_new[n]) for n in which}
        p = {n: jnp.exp2(s[n] - m_new[n]) for n in which}
        for n in which:
            p_buf[slot][n] = p[n].astype(BF16)
        l_new = {n: a_new[n] * stats[n][1] + jnp.sum(p[n], axis=0, keepdims=True) for n in which}
        stats = tuple((m_new[n], l_new[n]) if n in which else stats[n] for n in every)
        alpha = tuple(a_new[n] if n in which else alpha[n] for n in every)
        return stats, alpha

    def stage_c(slot, e, alpha, which):
        vt = jnp.where(e == 0, vmt_ref[0, 0, 0], vt_ref[0, 0, jnp.maximum(e - 1, 0)])
        pv = {n: _dot(vt[(n // hq) * HEAD_DIM:(n // hq + 1) * HEAD_DIM, :], p_buf[slot][n])
              for n in which}
        for n in which:
            acc_buf[n] = alpha[n] * acc_buf[n] + pv[n]

    unmasked = {n: None for n in every}
    stats = ((jnp.full((1, tk), MASKED_KEY, F32), jnp.zeros((1, tk), F32)),) * len(every)
    alpha = (jnp.ones((1, tk), F32),) * len(every)
    acc_buf[...] = jnp.zeros_like(acc_buf)
    p1_buf[...] = jnp.zeros_like(p1_buf)
    stage_a((km_ref[0, 0], km_ref[0, 1]), 0, every)

    def body(i, state):
        stats, alpha_prev = state
        stats, alpha_a = stage_b(0, stats, alpha_prev, unmasked)
        stage_a(real_keys(2 * i), 1, every)
        stage_c(1, jnp.maximum(2 * i - 1, 0), alpha_prev, every)
        stage_a(real_keys(2 * i + 1), 0, every)
        stats, alpha_b = stage_b(1, stats, alpha_a, unmasked)
        stage_c(0, 2 * i, alpha_a, every)
        return stats, alpha_b

    stats, alpha_prev = lax.fori_loop(0, (hq // 2) * qi, body, (stats, alpha))
    e0 = hq * qi
    alphas = {-1: alpha_prev}
    for r in range(1, hq + 3):
        if r <= hq:
            stage_a(real_keys(e0 + r - 1), r % 2, from_sub(r - 1))
        j = r - 1
        if j <= hq:
            masks = unmasked if j == 0 else {n: (causal if n % hq == j - 1 else None)
                                             for n in from_sub(j - 1)}
            stats, alphas[j] = stage_b(j % 2, stats, alphas[j - 1], masks)
        j = r - 2
        if j == -1:
            stage_c(1, jnp.maximum(e0 - 1, 0), alphas[-1], every)
        else:
            stage_c(j % 2, e0 + j, alphas[j], every if j == 0 else from_sub(j - 1))
    for u in range(hq):
        out_t = jnp.concatenate([acc_buf[u] / stats[u][1], acc_buf[hq + u] / stats[hq + u][1]], axis=0)
        o_ref[0, u * tk:(u + 1) * tk, :] = out_t.T.astype(o_ref.dtype)


def _attn(qt, kp, vt, kp_meta, vt_meta, *, tk):
    b, _, l, _ = kp.shape
    nk = vt.shape[2]
    hq = ATTN_SUBTILES
    assert vt.shape[4] == tk and qt.shape[4] == tk and kp_meta.shape[2] == tk and l % (hq * tk) == 0
    return pl.pallas_call(
        functools.partial(_attn_kernel, tk=tk),
        grid=(b, PAIRS, l // (hq * tk)),
        in_specs=[pl.BlockSpec((1, 2, hq, LANES, tk), lambda bi, p, qi: (bi, p, qi, 0, 0)),
                  pl.BlockSpec((1, 2, l, LANES), lambda bi, p, qi: (bi, p, 0, 0)),
                  pl.BlockSpec((1, 1, nk, LANES, tk), lambda bi, p, qi: (bi, p, 0, 0, 0)),
                  pl.BlockSpec((1, 2, tk, LANES), lambda bi, p, qi: (0, p, 0, 0)),
                  pl.BlockSpec((1, 1, 1, LANES, tk), lambda bi, p, qi: (0, p, 0, 0, 0))],
        out_specs=pl.BlockSpec((1, hq * tk, LANES), lambda bi, p, qi: (bi, qi, p)),
        out_shape=jax.ShapeDtypeStruct((b, l, WIDTH), BF16),
        scratch_shapes=[pltpu.VMEM((2 * hq, tk, tk), F32), pltpu.VMEM((2 * hq, tk, tk), F32),
                        pltpu.VMEM((2 * hq, tk, tk), BF16), pltpu.VMEM((2 * hq, tk, tk), BF16),
                        pltpu.VMEM((2 * hq, HEAD_DIM, tk), F32)],
        compiler_params=pltpu.CompilerParams(
            dimension_semantics=("parallel", "parallel", "arbitrary"), vmem_limit_bytes=VMEM_LIMIT),
        name="attn",
    )(qt, kp, vt, kp_meta, vt_meta)


def _pick(n, prefs):
    for t in prefs:
        if n % t == 0:
            return t
    raise ValueError(f"no tile for {n}")


def _pack_wt_kernel(wt_ref, o_ref):
    rw, fx = 3 * WIDTH + DECAY_LORA + AAA_LORA + GATE_LORA, 3 * WIDTH + HEADS
    wd0, ad0, gd0 = 3 * WIDTH, 3 * WIDTH + DECAY_LORA, 3 * WIDTH + DECAY_LORA + AAA_LORA
    fl0 = rw + 3 * WIDTH

    def put(dst, src, n):
        o_ref[dst:dst + n, :] = wt_ref[src:src + n, :].astype(BF16)

    def zero(dst, n):
        o_ref[dst:dst + n, :] = jnp.zeros((n, o_ref.shape[1]), BF16)

    put(C_R, 0, 3 * WIDTH)
    put(C_FQ, rw, 3 * WIDTH)
    put(C_GA, rw + fx, 2 * D_MODEL)
    put(C_S, wd0, DECAY_LORA)
    put(C_S + FL_LANE, fl0, HEADS)
    zero(C_S + FL_LANE + HEADS, LANES - FL_LANE - HEADS)
    put(C_S + LANES, ad0, AAA_LORA)
    zero(C_S + LANES + AAA_LORA, LANES - AAA_LORA)
    put(C_S + 2 * LANES, gd0, GATE_LORA)


def _pack_w_in_t(wt, *, cols=256):
    n, d = wt.shape
    return pl.pallas_call(
        _pack_wt_kernel,
        grid=(d // cols,),
        in_specs=[pl.BlockSpec((n, cols), lambda i: (0, i))],
        out_specs=pl.BlockSpec((N_PACK, cols), lambda i: (0, i)),
        out_shape=jax.ShapeDtypeStruct((N_PACK, d), BF16),
        compiler_params=pltpu.CompilerParams(
            dimension_semantics=("parallel",), vmem_limit_bytes=VMEM_LIMIT),
        name="pack_w_in",
    )(wt)


def _pack_mu(mu):
    wd0, ad0, gd0 = 3 * WIDTH, 3 * WIDTH + DECAY_LORA, 3 * WIDTH + DECAY_LORA + AAA_LORA
    z = lambda n: jnp.zeros((n,), mu.dtype)
    mu_s = jnp.concatenate([mu[wd0:ad0], z(LANES - DECAY_LORA), mu[ad0:gd0], z(LANES - AAA_LORA),
                            mu[gd0:]])
    return mu[:3 * WIDTH], mu_s


def _pad_rows(w, rows):
    return jnp.concatenate([w, jnp.zeros((rows - w.shape[0], w.shape[1]), w.dtype)], axis=0)


def _mixer_inputs(h2d, bsz, norm_g, w_pack, tm, tn):
    proj = _rms_matmul(h2d, norm_g, w_pack, tm=tm, tn=tn)
    return proj, proj.reshape(bsz, h2d.shape[0] // bsz, N_PACK)


def kernel(x, meta_tokens, norm1_g, w_in, rwkv_mu, rwkv_w0, rwkv_w2, rwkv_a0, rwkv_a2, rwkv_g2,
           rwkv_k_k, rwkv_k_a, rwkv_r_k, rwkv_gn_w, rwkv_gn_b, fox_q_norm_g, fox_k_norm_g,
           fox_f_bias, w_branch_a, w_branch_b, w_o, norm2_g, w_gate_up, w_down):
    bsz, seq, d = x.shape
    assert d == D_MODEL and norm1_g.shape[0] == 1 and seq % (ATTN_SUBTILES * 256) == 0
    m = bsz * seq
    x2d = x.reshape(m, d)
    meta2d = jnp.concatenate(
        [jnp.zeros((META_ROWS - N_META, d), x.dtype), meta_tokens.astype(x.dtype)], axis=0)

    w_pack = _pack_w_in_t(jnp.swapaxes(w_in[0], 0, 1))
    mu_z, mu_s = _pack_mu(rwkv_mu[0])
    w2p = _pad_rows(rwkv_w2[0], LANES).astype(BF16)
    a2p = _pad_rows(rwkv_a2[0], LANES).astype(BF16)
    g2 = rwkv_g2[0].astype(BF16)
    r_k = rwkv_r_k[0].reshape(WIDTH)

    tm = _pick(m, (1024, 512, 256))
    tt = _pick(seq, (256,))

    _, pm3 = _mixer_inputs(meta2d, 1, norm1_g[0], w_pack, META_ROWS, tn=1536)
    zero_z = jnp.zeros((8, 3 * WIDTH), F32)
    zero_s = jnp.zeros((8, S_COLS), F32)
    prep = functools.partial(_rwkv_prep, mu_z=mu_z, mu_s=mu_s, w0=rwkv_w0[0], w2p=w2p,
                             a0=rwkv_a0[0], a2p=a2p, g2=g2, k_k=rwkv_k_k[0], k_a=rwkv_k_a[0])
    wkv = functools.partial(_wkv, gn_w=rwkv_gn_w[0], gn_b=rwkv_gn_b[0], r_k=r_k)
    fox = functools.partial(_fox_prep, q_g=fox_q_norm_g[0], k_g=fox_k_norm_g[0],
                            f_bias=fox_f_bias[0])
    mr = prep(pm3, zero_z, zero_s, tt=META_ROWS)
    _, s_meta = wkv(*mr, jnp.zeros((1, PAIRS, LANES, LANES), F32))
    _, kp_meta, vt_meta, c_meta = fox(pm3, jnp.zeros((1, LANES), F32), tt=META_ROWS,
                                      n_pad=META_ROWS - N_META)

    proj2d, p3 = _mixer_inputs(x2d, bsz, norm1_g[0], w_pack, tm, tn=1536)
    rr = prep(p3, pm3[0, META_ROWS - 8:, 0:3 * WIDTH], pm3[0, META_ROWS - 8:, C_S:], tt=tt)
    ya, _ = wkv(*rr, s_meta)
    qt, kp, vt, _ = fox(p3, c_meta[0, 7:8, :], tt=tt, n_pad=0)
    yb = _attn(qt, kp, vt, kp_meta, vt_meta, tk=tt)

    merged = _merge(ya.reshape(m, WIDTH), yb.reshape(m, WIDTH), w_branch_a[0].astype(BF16),
                    w_branch_b[0].astype(BF16), proj2d, tm=tm // 2, tn=D_MODEL)
    h1 = _matmul_res(merged, w_o[0].astype(BF16), x2d, tm=tm, tn=1024)
    act = _rms_matmul_glu(h1, norm2_g[0], w_gate_up[0].astype(BF16), tm=tm, tn=512)
    out = _matmul_res(act, w_down[0].astype(BF16), h1, tm=tm, tn=512)
    return out.reshape(bsz, seq, d)
```

```python
import functools

import jax
import jax.numpy as jnp
from jax import lax
from jax.experimental import pallas as pl
from jax.experimental.pallas import tpu as pltpu

F32 = jnp.float32
BF16 = jnp.bfloat16

D_MODEL = 2048
N_META = 16
HEAD_DIM = 64
HEADS = 16
WIDTH = HEADS * HEAD_DIM
PAIRS = HEADS // 2
DECAY_LORA = 96
AAA_LORA = 96
GATE_LORA = 256
D_FF = 5632
RMS_EPS = 1e-6
GN_EPS = 64e-5
ATTN_SCALE = HEAD_DIM ** -0.5
LOG2E = 1.4426950408889634

LANES = 128
META_ROWS = 256
CHUNK = 64
MASKED_KEY = -1e30

C_R, C_K, C_V = 0, 1024, 2048
C_FQ, C_FK, C_FV = 3072, 4096, 5120
C_GA, C_GB = 6144, 8192
C_S = 10240
S_COLS = 512
FL_LANE = 96
N_PACK = C_S + S_COLS

VMEM_LIMIT = 56 * 1024 * 1024


def _sigmoid(x):
    return 0.5 * jnp.tanh(0.5 * x) + 0.5


def _softplus(x):
    return jnp.maximum(x, 0.0) + jnp.log(1.0 + jnp.exp(-jnp.abs(x)))


def _dot(a, b):
    return jnp.dot(a, b, preferred_element_type=F32)


def _dot_nt(a, b):
    return lax.dot_general(a, b, (((1,), (1,)), ((), ())), preferred_element_type=F32)


def _split3(x):
    hi = x.astype(BF16)
    r1 = x - hi.astype(F32)
    mid = r1.astype(BF16)
    lo = (r1 - mid.astype(F32)).astype(BF16)
    return hi, mid, lo


_NN = (((1,), (0,)), ((), ()))
_NT = (((1,), (1,)), ((), ()))
_TN = (((0,), (0,)), ((), ()))


def _head_ones():
    r = lax.broadcasted_iota(jnp.int32, (LANES, LANES), 0) // HEAD_DIM
    c = lax.broadcasted_iota(jnp.int32, (LANES, LANES), 1) // HEAD_DIM
    return jnp.where(r == c, 1.0, 0.0).astype(BF16)


def _head_sum(x, ones_bd):
    hi = x.astype(BF16)
    lo = (x - hi.astype(F32)).astype(BF16)
    return _dot(jnp.concatenate([hi, lo], axis=1), jnp.concatenate([ones_bd, ones_bd], axis=0))


def _rms_mm_kernel(x_ref, g_ref, wt_ref, o_ref, xn_ref):
    @pl.when(pl.program_id(1) == 0)
    def _():
        x = x_ref[...]
        ms = jnp.mean(x * x, axis=-1, keepdims=True)
        xn_ref[...] = ((x * lax.rsqrt(ms + RMS_EPS)) * g_ref[...]).astype(BF16)

    o_ref[...] = _dot_nt(xn_ref[...], wt_ref[...]).astype(o_ref.dtype)


def _rms_glu_kernel(x_ref, g_ref, wg_ref, wu_ref, o_ref, xn_ref):
    @pl.when(pl.program_id(1) == 0)
    def _():
        x = x_ref[...]
        ms = jnp.mean(x * x, axis=-1, keepdims=True)
        xn_ref[...] = ((x * lax.rsqrt(ms + RMS_EPS)) * g_ref[...]).astype(BF16)

    xn = xn_ref[...]
    gate = _dot(xn, wg_ref[...])
    up = _dot(xn, wu_ref[...])
    o_ref[...] = (gate * _sigmoid(gate) * up).astype(o_ref.dtype)


def _rms_matmul(x2d, g, wt_bf16, *, tm, tn, out_dtype=F32):
    m, d = x2d.shape
    n = wt_bf16.shape[0]
    return pl.pallas_call(
        _rms_mm_kernel,
        grid=(m // tm, n // tn),
        in_specs=[pl.BlockSpec((tm, d), lambda i, j: (i, 0)),
                  pl.BlockSpec((1, d), lambda i, j: (0, 0)),
                  pl.BlockSpec((tn, d), lambda i, j: (j, 0))],
        out_specs=pl.BlockSpec((tm, tn), lambda i, j: (i, j)),
        out_shape=jax.ShapeDtypeStruct((m, n), out_dtype),
        scratch_shapes=[pltpu.VMEM((tm, d), BF16)],
        compiler_params=pltpu.CompilerParams(
            dimension_semantics=("parallel", "arbitrary"), vmem_limit_bytes=VMEM_LIMIT),
        name="rms_matmul",
    )(x2d, g.reshape(1, d), wt_bf16)


def _rms_matmul_glu(x2d, g, w_bf16, *, tm, tn):
    m, d = x2d.shape
    n = w_bf16.shape[1] // 2
    up_off = n // tn
    return pl.pallas_call(
        _rms_glu_kernel,
        grid=(m // tm, n // tn),
        in_specs=[pl.BlockSpec((tm, d), lambda i, j: (i, 0)),
                  pl.BlockSpec((1, d), lambda i, j: (0, 0)),
                  pl.BlockSpec((d, tn), lambda i, j: (0, j)),
                  pl.BlockSpec((d, tn), lambda i, j: (0, j + up_off))],
        out_specs=pl.BlockSpec((tm, tn), lambda i, j: (i, j)),
        out_shape=jax.ShapeDtypeStruct((m, n), BF16),
        scratch_shapes=[pltpu.VMEM((tm, d), BF16)],
        compiler_params=pltpu.CompilerParams(
            dimension_semantics=("parallel", "arbitrary"), vmem_limit_bytes=VMEM_LIMIT),
        name="rms_matmul_glu",
    )(x2d, g.reshape(1, d), w_bf16, w_bf16)


def _mm_res_kernel(a_ref, b_ref, r_ref, o_ref):
    o_ref[...] = _dot(a_ref[...], b_ref[...]) + r_ref[...]


def _matmul_res(a, b, res, *, tm, tn):
    m, kk = a.shape
    n = b.shape[1]
    return pl.pallas_call(
        _mm_res_kernel,
        grid=(m // tm, n // tn),
        in_specs=[pl.BlockSpec((tm, kk), lambda i, j: (i, 0)),
                  pl.BlockSpec((kk, tn), lambda i, j: (0, j)),
                  pl.BlockSpec((tm, tn), lambda i, j: (i, j))],
        out_specs=pl.BlockSpec((tm, tn), lambda i, j: (i, j)),
        out_shape=jax.ShapeDtypeStruct((m, n), F32),
        compiler_params=pltpu.CompilerParams(
            dimension_semantics=("parallel", "arbitrary"), vmem_limit_bytes=VMEM_LIMIT),
        name="matmul_res",
    )(a, b, res)


def _merge_kernel(ya_ref, yb_ref, wa_ref, wb_ref, za_ref, zb_ref, o_ref):
    ta = _dot(ya_ref[...], wa_ref[...])
    tb = _dot(yb_ref[...], wb_ref[...])
    o_ref[...] = (_sigmoid(za_ref[...]) * ta + _sigmoid(zb_ref[...]) * tb).astype(o_ref.dtype)


def _merge(ya, yb, wa, wb, proj2d, *, tm, tn):
    m, kk = ya.shape
    n = wa.shape[1]
    oa, ob = C_GA // tn, C_GB // tn
    return pl.pallas_call(
        _merge_kernel,
        grid=(m // tm, n // tn),
        in_specs=[pl.BlockSpec((tm, kk), lambda i, j: (i, 0)),
                  pl.BlockSpec((tm, kk), lambda i, j: (i, 0)),
                  pl.BlockSpec((kk, tn), lambda i, j: (0, j)),
                  pl.BlockSpec((kk, tn), lambda i, j: (0, j)),
                  pl.BlockSpec((tm, tn), lambda i, j: (i, j + oa)),
                  pl.BlockSpec((tm, tn), lambda i, j: (i, j + ob))],
        out_specs=pl.BlockSpec((tm, tn), lambda i, j: (i, j)),
        out_shape=jax.ShapeDtypeStruct((m, n), BF16),
        compiler_params=pltpu.CompilerParams(
            dimension_semantics=("parallel", "arbitrary"), vmem_limit_bytes=VMEM_LIMIT),
        name="merge",
    )(ya, yb, wa, wb, proj2d, proj2d)


def _rwkv_prep_kernel(z_ref, zs_ref, pz_ref, ps_ref, p0z_ref, p0s_ref, mu_ref, mus_ref,
                      w0_ref, w2_ref, a0_ref, a2_ref, g2_ref, kk_ref, ka_ref,
                      r_out, lw_out, k_out, v_out, a_out, b_out, g_out, *, tt):
    first = pl.program_id(1) == 0
    z = z_ref[0]
    zs = zs_ref[0]
    pz = jnp.where(first, p0z_ref[7:8, :], pz_ref[0, 7:8, :])
    ps = jnp.where(first, p0s_ref[7:8, :], ps_ref[0, 7:8, :])
    row = lax.broadcasted_iota(jnp.int32, (tt, 1), 0)

    def lerp(cur, prev_last, mu):
        prev = jnp.where(row == 0, prev_last, pltpu.roll(cur, 1, axis=0))
        return cur + (prev - cur) * mu

    zm = lerp(z, pz, mu_ref[...])
    zsm = lerp(zs, ps, mus_ref[...])
    r = zm[:, C_R:C_R + WIDTH]
    k = zm[:, C_K:C_K + WIDTH]
    v = zm[:, C_V:C_V + WIDTH]
    wd = zsm[:, 0:LANES]
    ad = zsm[:, LANES:2 * LANES]
    gd = zsm[:, 2 * LANES:4 * LANES]

    w_log = -_softplus(-(w0_ref[...] + _dot(jnp.tanh(wd).astype(BF16), w2_ref[...]))) - 0.5
    lw_out[0] = -jnp.exp(w_log)
    a_sig = _sigmoid(a0_ref[...] + _dot(ad.astype(BF16), a2_ref[...]))
    g_out[0] = _dot(_sigmoid(gd).astype(BF16), g2_ref[...]).astype(g_out.dtype)

    ones_bd = _head_ones()
    kk = k * kk_ref[...]
    kk2 = kk * kk
    nrm = jnp.concatenate(
        [_head_sum(kk2[:, p * LANES:(p + 1) * LANES], ones_bd) for p in range(PAIRS)], axis=1)
    kk = kk * lax.rsqrt(jnp.maximum(nrm, 1e-24))
    r_out[0] = r.astype(r_out.dtype)
    k_out[0] = (k * (1.0 + (a_sig - 1.0) * ka_ref[...])).astype(k_out.dtype)
    v_out[0] = v.astype(v_out.dtype)
    a_out[0] = (-kk).astype(a_out.dtype)
    b_out[0] = (kk * a_sig).astype(b_out.dtype)


def _rwkv_prep(proj3d, prev0_z, prev0_s, mu_z, mu_s, w0, w2p, a0, a2p, g2, k_k, k_a, *, tt):
    b, l, _ = proj3d.shape
    s_blk = C_S // S_COLS
    zw = 3 * WIDTH
    row = lambda a: a.reshape(1, -1)
    vec = lambda n: pl.BlockSpec((1, n), lambda bi, i: (0, 0))
    full = lambda r, c: pl.BlockSpec((r, c), lambda bi, i: (0, 0))
    out_spec = pl.BlockSpec((1, tt, WIDTH), lambda bi, i: (bi, i, 0))
    sds = lambda dt: jax.ShapeDtypeStruct((b, l, WIDTH), dt)
    return pl.pallas_call(
        functools.partial(_rwkv_prep_kernel, tt=tt),
        grid=(b, l // tt),
        in_specs=[pl.BlockSpec((1, tt, zw), lambda bi, i: (bi, i, 0)),
                  pl.BlockSpec((1, tt, S_COLS), lambda bi, i: (bi, i, s_blk)),
                  pl.BlockSpec((1, 8, zw), lambda bi, i: (bi, jnp.maximum(i * (tt // 8) - 1, 0), 0)),
                  pl.BlockSpec((1, 8, S_COLS),
                               lambda bi, i: (bi, jnp.maximum(i * (tt // 8) - 1, 0), s_blk)),
                  full(8, zw), full(8, S_COLS),
                  vec(zw), vec(S_COLS),
                  vec(WIDTH), full(LANES, WIDTH), vec(WIDTH), full(LANES, WIDTH),
                  full(GATE_LORA, WIDTH), vec(WIDTH), vec(WIDTH)],
        out_specs=[out_spec] * 7,
        out_shape=[sds(BF16), sds(F32)] + [sds(BF16)] * 5,
        compiler_params=pltpu.CompilerParams(
            dimension_semantics=("parallel", "arbitrary"), vmem_limit_bytes=VMEM_LIMIT),
        name="rwkv_prep",
    )(proj3d, proj3d, proj3d, proj3d, prev0_z, prev0_s, row(mu_z), row(mu_s),
      row(w0), w2p, row(a0), a2p, g2, row(k_k), row(k_a))


def _bf(x):
    return x.astype(BF16)


WKV_CHUNKS = 4


def _wkv_group(a_t, r_t, b_t, k_t, b_h, k_h, v, s, w_total, masks):
    head0, strict, lower, eye = masks
    dg = functools.partial(lax.dot_general, preferred_element_type=F32)
    n = range(len(s))

    def bd(x):
        return jnp.concatenate([jnp.where(head0, x, 0.0), jnp.where(head0, 0.0, x)], axis=0)

    lhs = [_bf(jnp.concatenate([bd(a_t[i]), bd(r_t[i])], axis=0)) for i in n]
    rhs = [_bf(jnp.concatenate([bd(b_t[i]), bd(k_t[i])], axis=0)) for i in n]
    v_bd = [_bf(bd(v[i])) for i in n]
    gram = [dg(lhs[i], rhs[i], _NT) for i in n]
    a_s = [dg(lhs[i], _bf(s[i]), _NT) for i in n]
    n_ab = [jnp.where(strict, g[0:LANES, 0:LANES], 0.0) for g in gram]
    a_ak = [jnp.where(strict, g[0:LANES, LANES:2 * LANES], 0.0) for g in gram]
    a_r = [_bf(jnp.concatenate([jnp.where(lower, g[LANES:2 * LANES, 0:LANES], 0.0),
                                jnp.where(lower, g[LANES:2 * LANES, LANES:2 * LANES], 0.0)], axis=1))
           for g in gram]
    rhs_u = [a_s[i][0:LANES] + dg(_bf(a_ak[i]), v_bd[i], _NN) for i in n]

    p_inv = [eye + x for x in n_ab]
    q = [_bf(x) for x in n_ab]
    for _ in range(5):
        q = [_bf(dg(x, x, _NN)) for x in q]
        p_inv = [p_inv[i] + dg(q[i], _bf(p_inv[i]), _NN) for i in n]
    u = [dg(_bf(p_inv[i]), _bf(rhs_u[i]), _NN) for i in n]

    uv = [jnp.concatenate([_bf(u[i]), v_bd[i]], axis=0) for i in n]
    y_bd = [a_s[i][LANES:2 * LANES] + dg(a_r[i], uv[i], _NN) for i in n]
    y = [x[0:CHUNK] + x[CHUNK:2 * CHUNK] for x in y_bd]
    s_new = [w_total[i] * s[i]
             + dg(uv[i], _bf(jnp.concatenate([bd(b_h[i]), bd(k_h[i])], axis=0)), _TN) for i in n]
    return y, s_new


def _wkv_kernel(r_ref, lw_ref, k_ref, v_ref, a_ref, b_ref, g_ref, s0_ref,
                gnw_ref, gnb_ref, rk_ref, y_ref, sfin_ref, s_scr, *, nc, chunks):
    c = pl.program_id(1)

    @pl.when(c == 0)
    def _():
        s_scr[...] = s0_ref[0]

    rows_all = chunks * CHUNK
    r = r_ref[0].astype(F32)
    lw = lw_ref[0]
    k = k_ref[0].astype(F32)
    v = v_ref[0].astype(F32)
    a = a_ref[0].astype(F32)
    b = b_ref[0].astype(F32)

    ti = lax.broadcasted_iota(jnp.int32, (rows_all, rows_all), 0)
    tj = lax.broadcasted_iota(jnp.int32, (rows_all, rows_all), 1)
    ltri = jnp.where((tj <= ti) & (ti // CHUNK == tj // CHUNK), 1.0, 0.0).astype(BF16)
    hi, mid, lo = _split3(lw)
    cum = _dot(jnp.concatenate([ltri, ltri, ltri], axis=1),
               jnp.concatenate([hi, mid, lo], axis=0))

    lane = lax.broadcasted_iota(jnp.int32, (1, LANES), 1)
    bi = lax.broadcasted_iota(jnp.int32, (LANES, LANES), 0)
    bj = lax.broadcasted_iota(jnp.int32, (LANES, LANES), 1)
    same = (bi // CHUNK) == (bj // CHUNK)
    masks = (lane < HEAD_DIM,
             same & ((bj % CHUNK) < (bi % CHUNK)),
             same & ((bj % CHUNK) <= (bi % CHUNK)),
             jnp.where(bi == bj, 1.0, 0.0))

    ones_bd = _head_ones()
    inv_n = 1.0 / HEAD_DIM
    sls = [slice(p * LANES, (p + 1) * LANES) for p in range(PAIRS)]
    cut = lambda x: [x[:, sl] for sl in sls]
    state = [s_scr[p] for p in range(PAIRS)]
    for sc in range(chunks):
        rows = slice(sc * CHUNK, (sc + 1) * CHUNK)
        cum_c, lw_c = cum[rows], lw[rows]
        total = cum_c[CHUNK - 1:CHUNK, :]
        e_inv = jnp.exp(-cum_c)
        e_end = jnp.exp(total - cum_c)
        ys, state = _wkv_group(
            cut(a[rows] * jnp.exp(cum_c - lw_c)), cut(r[rows] * jnp.exp(cum_c)),
            cut(b[rows] * e_inv), cut(k[rows] * e_inv), cut(b[rows] * e_end), cut(k[rows] * e_end),
            cut(v[rows]), state, cut(jnp.exp(total)), masks)

        y_all = jnp.concatenate(ys, axis=0)
        rk_all = jnp.concatenate([r[rows, sl] * k[rows, sl] * rk_ref[:, sl] for sl in sls], axis=0)
        bonus_all = _head_sum(rk_all, ones_bd)
        d_all = y_all - _head_sum(y_all, ones_bd) * inv_n
        rstd_all = lax.rsqrt(_head_sum(d_all * d_all, ones_bd) * inv_n + GN_EPS)
        for i, sl in enumerate(sls):
            prow = slice(i * CHUNK, (i + 1) * CHUNK)
            yn = d_all[prow] * rstd_all[prow] * gnw_ref[:, sl] + gnb_ref[:, sl]
            y_ref[0, rows, sl] = ((yn + bonus_all[prow] * v[rows, sl])
                                  * g_ref[0, rows, sl].astype(F32)).astype(y_ref.dtype)

    for p in range(PAIRS):
        s_scr[p] = state[p]

        @pl.when(c == nc - 1)
        def _(p=p):
            sfin_ref[0, p] = state[p]


def _wkv(r, lw, k, v, a, b, g, s0, gn_w, gn_b, r_k):
    bsz, l, _ = r.shape
    chunks = min(WKV_CHUNKS, l // CHUNK)
    nc = l // (chunks * CHUNK)
    tok = pl.BlockSpec((1, chunks * CHUNK, WIDTH), lambda bi, c: (bi, c, 0))
    vec = pl.BlockSpec((1, WIDTH), lambda bi, c: (0, 0))
    st_in = pl.BlockSpec((1, PAIRS, LANES, LANES), lambda bi, c: (0, 0, 0, 0))
    st_out = pl.BlockSpec((1, PAIRS, LANES, LANES), lambda bi, c: (bi, 0, 0, 0))
    row = lambda x: x.reshape(1, WIDTH)
    return pl.pallas_call(
        functools.partial(_wkv_kernel, nc=nc, chunks=chunks),
        grid=(bsz, nc),
        in_specs=[tok] * 7 + [st_in, vec, vec, vec],
        out_specs=[tok, st_out],
        out_shape=[jax.ShapeDtypeStruct((bsz, l, WIDTH), BF16),
                   jax.ShapeDtypeStruct((bsz, PAIRS, LANES, LANES), F32)],
        scratch_shapes=[pltpu.VMEM((PAIRS, LANES, LANES), F32)],
        compiler_params=pltpu.CompilerParams(
            dimension_semantics=("parallel", "arbitrary"), vmem_limit_bytes=VMEM_LIMIT),
        name="wkv",
    )(r, lw, k, v, a, b, g, s0, row(gn_w), row(gn_b), row(r_k))


def _fox_prep_kernel(z_ref, zs_ref, c0_ref, qg_ref, kg_ref, fb_ref,
                     qt_out, k_out, vt_out, c_out, carry, *, tt, n_pad):
    i = pl.program_id(1)

    @pl.when(i == 0)
    def _():
        carry[...] = c0_ref[...]

    z = z_ref[0]
    fl = zs_ref[0][:, 0:LANES]
    log_f = -_softplus(-(fl + fb_ref[...]))
    ti = lax.broadcasted_iota(jnp.int32, (tt, tt), 0)
    tj = lax.broadcasted_iota(jnp.int32, (tt, tt), 1)
    ltri = jnp.where(tj <= ti, 1.0, 0.0).astype(BF16)
    hi, mid, lo = _split3(log_f)
    cum = carry[...] + _dot(jnp.concatenate([ltri, ltri, ltri], axis=1),
                            jnp.concatenate([hi, mid, lo], axis=0))
    carry[...] = cum[tt - 1:tt, :]
    c_out[0] = cum[tt - 8:tt, :]
    c_hi, c_mid, c_lo = (t.astype(F32) for t in _split3(cum * LOG2E))
    neg_hi, neg_mid, neg_lo = -c_hi, -c_mid, -c_lo
    if n_pad:
        grow = i * tt + lax.broadcasted_iota(jnp.int32, (tt, 1), 0)
        neg_hi = jnp.where(grow < n_pad, MASKED_KEY, neg_hi)

    for p in range(PAIRS):
        vt_out[0, p, 0] = z[:, 2 * WIDTH + p * LANES:2 * WIDTH + (p + 1) * LANES].T.astype(BF16)
    ones_bd = _head_ones()
    lane = lax.broadcasted_iota(jnp.int32, (1, LANES), 1)
    inv_n = 1.0 / HEAD_DIM
    for p in range(PAIRS):
        zq = z[:, p * LANES:(p + 1) * LANES]
        zk = z[:, WIDTH + p * LANES:WIDTH + (p + 1) * LANES]
        qn = zq * lax.rsqrt(_head_sum(zq * zq, ones_bd) * inv_n + RMS_EPS) * (qg_ref[...] * (ATTN_SCALE * LOG2E))
        kn = zk * lax.rsqrt(_head_sum(zk * zk, ones_bd) * inv_n + RMS_EPS) * kg_ref[...]
        for j in range(2):
            h = 2 * p + j
            own = (lane < HEAD_DIM) if j == 0 else (lane >= HEAD_DIM)
            e = lane - (HEAD_DIM if j == 0 else 0)
            col = lambda t: t[:, FL_LANE + h:FL_LANE + h + 1]
            q_extra = jnp.where((e >= 0) & (e < 3), 1.0, 0.0)
            k_extra = jnp.where(e == 0, col(neg_hi), jnp.where(e == 1, col(neg_mid), jnp.where(
                e == 2, col(neg_lo), 0.0)))
            qt_out[0, h, 0] = jnp.where(own, qn, q_extra).T.astype(BF16)
            k_out[0, h] = jnp.where(own, kn, k_extra).astype(BF16)


def _fox_prep(proj3d, c0, q_g, k_g, f_bias, *, tt, n_pad):
    b, l, _ = proj3d.shape
    zw = 3 * WIDTH
    tile2 = lambda g: jnp.concatenate([g, g]).reshape(1, LANES)
    fb = jnp.concatenate([jnp.zeros((FL_LANE,), F32), f_bias,
                          jnp.zeros((LANES - FL_LANE - HEADS,), F32)]).reshape(1, LANES)
    hd = pl.BlockSpec((1, HEADS, tt, LANES), lambda bi, i: (bi, 0, i, 0))
    vec = pl.BlockSpec((1, LANES), lambda bi, i: (0, 0))
    return pl.pallas_call(
        functools.partial(_fox_prep_kernel, tt=tt, n_pad=n_pad),
        grid=(b, l // tt),
        in_specs=[pl.BlockSpec((1, tt, zw), lambda bi, i: (bi, i, C_FQ // zw)),
                  pl.BlockSpec((1, tt, S_COLS), lambda bi, i: (bi, i, C_S // S_COLS)),
                  vec, vec, vec, vec],
        out_specs=[pl.BlockSpec((1, HEADS, 1, LANES, tt), lambda bi, i: (bi, 0, i, 0, 0)),
                   hd,
                   pl.BlockSpec((1, PAIRS, 1, LANES, tt), lambda bi, i: (bi, 0, i, 0, 0)),
                   pl.BlockSpec((1, 8, LANES), lambda bi, i: (bi, i, 0))],
        out_shape=[jax.ShapeDtypeStruct((b, HEADS, l // tt, LANES, tt), BF16),
                   jax.ShapeDtypeStruct((b, HEADS, l, LANES), BF16),
                   jax.ShapeDtypeStruct((b, PAIRS, l // tt, LANES, tt), BF16),
                   jax.ShapeDtypeStruct((b, (l // tt) * 8, LANES), F32)],
        scratch_shapes=[pltpu.VMEM((1, LANES), F32)],
        compiler_params=pltpu.CompilerParams(
            dimension_semantics=("parallel", "arbitrary"), vmem_limit_bytes=VMEM_LIMIT),
        name="fox_prep",
    )(proj3d, proj3d, c0, tile2(q_g), tile2(k_g), fb)


ATTN_SUBTILES = 4


def _attn_kernel(qt_ref, k_ref, vt_ref, km_ref, vmt_ref, o_ref, s0_buf, s1_buf, p0_buf, p1_buf,
                 acc_buf, *, tk):
    hq = ATTN_SUBTILES
    s_buf = (s0_buf, s1_buf)
    p_buf = (p0_buf, p1_buf)
    qi = pl.program_id(2)
    key_i = lax.broadcasted_iota(jnp.int32, (tk, tk), 0)
    qry_i = lax.broadcasted_iota(jnp.int32, (tk, tk), 1)
    causal = key_i <= qry_i
    every = tuple(range(2 * hq))
    from_sub = lambda u0: tuple(n for n in every if n % hq >= u0)
    qts = [qt_ref[0, n // hq, n % hq] for n in every]

    def real_keys(r):
        off = pl.multiple_of(r * tk, tk)
        return (k_ref[0, 0, pl.ds(off, tk), :], k_ref[0, 1, pl.ds(off, tk), :])

    def stage_a(kblks, slot, which):
        for n in which:
            s_buf[slot][n] = _dot(kblks[n // hq], qts[n])

    def stage_b(slot, stats, alpha, masks):
        which = tuple(masks)
        s = {n: s_buf[slot][n] for n in which}
        for n in which:
            if masks[n] is not None:
                s[n] = jnp.where(masks[n], s[n], MASKED_KEY)
        m_new = {n: jnp.maximum(stats[n][0], jnp.max(s[n], axis=0, keepdims=True)) for n in which}
        a_new = {n: jnp.exp2(stats[n][0] - m_new[n]) for n in which}
        p = {n: jnp.exp2(s[n] - m_new[n]) for n in which}
        for n in which:
            p_buf[slot][n] = p[n].astype(BF16)
        l_new = {n: a_new[n] * stats[n][1] + jnp.sum(p[n], axis=0, keepdims=True) for n in which}
        stats = tuple((m_new[n], l_new[n]) if n in which else stats[n] for n in every)
        alpha = tuple(a_new[n] if n in which else alpha[n] for n in every)
        return stats, alpha

    def stage_c(slot, e, alpha, which):
        vt = jnp.where(e == 0, vmt_ref[0, 0, 0], vt_ref[0, 0, jnp.maximum(e - 1, 0)])
        pv = {n: _dot(vt[(n // hq) * HEAD_DIM:(n // hq + 1) * HEAD_DIM, :], p_buf[slot][n])
              for n in which}
        for n in which:
            acc_buf[n] = alpha[n] * acc_buf[n] + pv[n]

    unmasked = {n: None for n in every}
    stats = ((jnp.full((1, tk), MASKED_KEY, F32), jnp.zeros((1, tk), F32)),) * len(every)
    alpha = (jnp.ones((1, tk), F32),) * len(every)
    acc_buf[...] = jnp.zeros_like(acc_buf)
    p1_buf[...] = jnp.zeros_like(p1_buf)
    stage_a((km_ref[0, 0], km_ref[0, 1]), 0, every)

    def body(i, state):
        stats, alpha_prev = state
        stats, alpha_a = stage_b(0, stats, alpha_prev, unmasked)
        stage_a(real_keys(2 * i), 1, every)
        stage_c(1, jnp.maximum(2 * i - 1, 0), alpha_prev, every)
        stage_a(real_keys(2 * i + 1), 0, every)
        stats, alpha_b = stage_b(1, stats, alpha_a, unmasked)
        stage_c(0, 2 * i, alpha_a, every)
        return stats, alpha_b

    stats, alpha_prev = lax.fori_loop(0, (hq // 2) * qi, body, (stats, alpha))
    e0 = hq * qi
    alphas = {-1: alpha_prev}
    for r in range(1, hq + 3):
        if r <= hq:
            stage_a(real_keys(e0 + r - 1), r % 2, from_sub(r - 1))
        j = r - 1
        if j <= hq:
            masks = unmasked if j == 0 else {n: (causal if n % hq == j - 1 else None)
                                             for n in from_sub(j - 1)}
            stats, alphas[j] = stage_b(j % 2, stats, alphas[j - 1], masks)
        j = r - 2
        if j == -1:
            stage_c(1, jnp.maximum(e0 - 1, 0), alphas[-1], every)
        else:
            stage_c(j % 2, e0 + j, alphas[j], every if j == 0 else from_sub(j - 1))
    for u in range(hq):
        out_t = jnp.concatenate([acc_buf[u] / stats[u][1], acc_buf[hq + u] / stats[hq + u][1]], axis=0)
        o_ref[0, u * tk:(u + 1) * tk, :] = out_t.T.astype(o_ref.dtype)


def _attn(qt, kp, vt, kp_meta, vt_meta, *, tk):
    b, _, l, _ = kp.shape
    nk = vt.shape[2]
    hq = ATTN_SUBTILES
    assert vt.shape[4] == tk and qt.shape[4] == tk and kp_meta.shape[2] == tk and l % (hq * tk) == 0
    return pl.pallas_call(
        functools.partial(_attn_kernel, tk=tk),
        grid=(b, PAIRS, l // (hq * tk)),
        in_specs=[pl.BlockSpec((1, 2, hq, LANES, tk), lambda bi, p, qi: (bi, p, qi, 0, 0)),
                  pl.BlockSpec((1, 2, l, LANES), lambda bi, p, qi: (bi, p, 0, 0)),
                  pl.BlockSpec((1, 1, nk, LANES, tk), lambda bi, p, qi: (bi, p, 0, 0, 0)),
                  pl.BlockSpec((1, 2, tk, LANES), lambda bi, p, qi: (0, p, 0, 0)),
                  pl.BlockSpec((1, 1, 1, LANES, tk), lambda bi, p, qi: (0, p, 0, 0, 0))],
        out_specs=pl.BlockSpec((1, hq * tk, LANES), lambda bi, p, qi: (bi, qi, p)),
        out_shape=jax.ShapeDtypeStruct((b, l, WIDTH), BF16),
        scratch_shapes=[pltpu.VMEM((2 * hq, tk, tk), F32), pltpu.VMEM((2 * hq, tk, tk), F32),
                        pltpu.VMEM((2 * hq, tk, tk), BF16), pltpu.VMEM((2 * hq, tk, tk), BF16),
                        pltpu.VMEM((2 * hq, HEAD_DIM, tk), F32)],
        compiler_params=pltpu.CompilerParams(
            dimension_semantics=("parallel", "parallel", "arbitrary"), vmem_limit_bytes=VMEM_LIMIT),
        name="attn",
    )(qt, kp, vt, kp_meta, vt_meta)


def _pick(n, prefs):
    for t in prefs:
        if n % t == 0:
            return t
    raise ValueError(f"no tile for {n}")


def _pack_wt_kernel(wt_ref, o_ref):
    rw, fx = 3 * WIDTH + DECAY_LORA + AAA_LORA + GATE_LORA, 3 * WIDTH + HEADS
    wd0, ad0, gd0 = 3 * WIDTH, 3 * WIDTH + DECAY_LORA, 3 * WIDTH + DECAY_LORA + AAA_LORA
    fl0 = rw + 3 * WIDTH

    def put(dst, src, n):
        o_ref[dst:dst + n, :] = wt_ref[src:src + n, :].astype(BF16)

    def zero(dst, n):
        o_ref[dst:dst + n, :] = jnp.zeros((n, o_ref.shape[1]), BF16)

    put(C_R, 0, 3 * WIDTH)
    put(C_FQ, rw, 3 * WIDTH)
    put(C_GA, rw + fx, 2 * D_MODEL)
    put(C_S, wd0, DECAY_LORA)
    put(C_S + FL_LANE, fl0, HEADS)
    zero(C_S + FL_LANE + HEADS, LANES - FL_LANE - HEADS)
    put(C_S + LANES, ad0, AAA_LORA)
    zero(C_S + LANES + AAA_LORA, LANES - AAA_LORA)
    put(C_S + 2 * LANES, gd0, GATE_LORA)


def _pack_w_in_t(wt, *, cols=256):
    n, d = wt.shape
    return pl.pallas_call(
        _pack_wt_kernel,
        grid=(d // cols,),
        in_specs=[pl.BlockSpec((n, cols), lambda i: (0, i))],
        out_specs=pl.BlockSpec((N_PACK, cols), lambda i: (0, i)),
        out_shape=jax.ShapeDtypeStruct((N_PACK, d), BF16),
        compiler_params=pltpu.CompilerParams(
            dimension_semantics=("parallel",), vmem_limit_bytes=VMEM_LIMIT),
        name="pack_w_in",
    )(wt)


def _pack_mu(mu):
    wd0, ad0, gd0 = 3 * WIDTH, 3 * WIDTH + DECAY_LORA, 3 * WIDTH + DECAY_LORA + AAA_LORA
    z = lambda n: jnp.zeros((n,), mu.dtype)
    mu_s = jnp.concatenate([mu[wd0:ad0], z(LANES - DECAY_LORA), mu[ad0:gd0], z(LANES - AAA_LORA),
                            mu[gd0:]])
    return mu[:3 * WIDTH], mu_s


def _pad_rows(w, rows):
    return jnp.concatenate([w, jnp.zeros((rows - w.shape[0], w.shape[1]), w.dtype)], axis=0)


def _mixer_inputs(h2d, bsz, norm_g, w_pack, tm, tn):
    proj = _rms_matmul(h2d, norm_g, w_pack, tm=tm, tn=tn)
    return proj, proj.reshape(bsz, h2d.shape[0] // bsz, N_PACK)


def kernel(x, meta_tokens, norm1_g, w_in, rwkv_mu, rwkv_w0, rwkv_w2, rwkv_a0, rwkv_a2, rwkv_g2,
           rwkv_k_k, rwkv_k_a, rwkv_r_k, rwkv_gn_w, rwkv_gn_b, fox_q_norm_g, fox_k_norm_g,
           fox_f_bias, w_branch_a, w_branch_b, w_o, norm2_g, w_gate_up, w_down):
    bsz, seq, d = x.shape
    assert d == D_MODEL and norm1_g.shape[0] == 1 and seq % (ATTN_SUBTILES * 256) == 0
    m = bsz * seq
    x2d = x.reshape(m, d)
    meta2d = jnp.concatenate(
        [jnp.zeros((META_ROWS - N_META, d), x.dtype), meta_tokens.astype(x.dtype)], axis=0)

    w_pack = _pack_w_in_t(jnp.swapaxes(w_in[0], 0, 1))
    mu_z, mu_s = _pack_mu(rwkv_mu[0])
    w2p = _pad_rows(rwkv_w2[0], LANES).astype(BF16)
    a2p = _pad_rows(rwkv_a2[0], LANES).astype(BF16)
    g2 = rwkv_g2[0].astype(BF16)
    r_k = rwkv_r_k[0].reshape(WIDTH)

    tm = _pick(m, (1024, 512, 256))
    tt = _pick(seq, (256,))
    tn_proj, tn_wo, tn_ffn = 1536, D_MODEL, 512

    _, pm3 = _mixer_inputs(meta2d, 1, norm1_g[0], w_pack, META_ROWS, tn=tn_proj)
    zero_z = jnp.zeros((8, 3 * WIDTH), F32)
    zero_s = jnp.zeros((8, S_COLS), F32)
    prep = functools.partial(_rwkv_prep, mu_z=mu_z, mu_s=mu_s, w0=rwkv_w0[0], w2p=w2p,
                             a0=rwkv_a0[0], a2p=a2p, g2=g2, k_k=rwkv_k_k[0], k_a=rwkv_k_a[0])
    wkv = functools.partial(_wkv, gn_w=rwkv_gn_w[0], gn_b=rwkv_gn_b[0], r_k=r_k)
    fox = functools.partial(_fox_prep, q_g=fox_q_norm_g[0], k_g=fox_k_norm_g[0],
                            f_bias=fox_f_bias[0])
    mr = prep(pm3, zero_z, zero_s, tt=META_ROWS)
    _, s_meta = wkv(*mr, jnp.zeros((1, PAIRS, LANES, LANES), F32))
    _, kp_meta, vt_meta, c_meta = fox(pm3, jnp.zeros((1, LANES), F32), tt=META_ROWS,
                                      n_pad=META_ROWS - N_META)

    proj2d, p3 = _mixer_inputs(x2d, bsz, norm1_g[0], w_pack, tm, tn=tn_proj)
    rr = prep(p3, pm3[0, META_ROWS - 8:, 0:3 * WIDTH], pm3[0, META_ROWS - 8:, C_S:], tt=tt)
    ya, _ = wkv(*rr, s_meta)
    qt, kp, vt, _ = fox(p3, c_meta[0, 7:8, :], tt=tt, n_pad=0)
    yb = _attn(qt, kp, vt, kp_meta, vt_meta, tk=tt)

    merged = _merge(ya.reshape(m, WIDTH), yb.reshape(m, WIDTH), w_branch_a[0].astype(BF16),
                    w_branch_b[0].astype(BF16), proj2d, tm=tm // 2, tn=D_MODEL)
    h1 = _matmul_res(merged, w_o[0].astype(BF16), x2d, tm=tm // 2, tn=tn_wo)
    act = _rms_matmul_glu(h1, norm2_g[0], w_gate_up[0].astype(BF16), tm=tm, tn=tn_ffn)
    out = _matmul_res(act, w_down[0].astype(BF16), h1, tm=tm, tn=tn_ffn)
    return out.reshape(bsz, seq, d)
```

```python
import functools

import jax
import jax.numpy as jnp
from jax import lax
from jax.experimental import pallas as pl
from jax.experimental.pallas import tpu as pltpu

F32 = jnp.float32
BF16 = jnp.bfloat16

D_MODEL = 2048
N_META = 16
HEAD_DIM = 64
HEADS = 16
WIDTH = HEADS * HEAD_DIM
PAIRS = HEADS // 2
DECAY_LORA = 96
AAA_LORA = 96
GATE_LORA = 256
D_FF = 5632
RMS_EPS = 1e-6
GN_EPS = 64e-5
ATTN_SCALE = HEAD_DIM ** -0.5
LOG2E = 1.4426950408889634

LANES = 128
META_ROWS = 256
CHUNK = 64
MASKED_KEY = -1e30

C_R, C_K, C_V = 0, 1024, 2048
C_FQ, C_FK, C_FV = 3072, 4096, 5120
C_GA, C_GB = 6144, 8192
C_S = 10240
S_COLS = 512
FL_LANE = 96
N_PACK = C_S + S_COLS

VMEM_LIMIT = 56 * 1024 * 1024


def _sigmoid(x):
    return 0.5 * jnp.tanh(0.5 * x) + 0.5


def _softplus(x):
    return jnp.maximum(x, 0.0) + jnp.log(1.0 + jnp.exp(-jnp.abs(x)))


def _dot(a, b):
    return jnp.dot(a, b, preferred_element_type=F32)


def _dot_nt(a, b):
    return lax.dot_general(a, b, (((1,), (1,)), ((), ())), preferred_element_type=F32)


def _split3(x):
    hi = x.astype(BF16)
    r1 = x - hi.astype(F32)
    mid = r1.astype(BF16)
    lo = (r1 - mid.astype(F32)).astype(BF16)
    return hi, mid, lo


_NN = (((1,), (0,)), ((), ()))
_NT = (((1,), (1,)), ((), ()))
_TN = (((0,), (0,)), ((), ()))


def _head_ones():
    r = lax.broadcasted_iota(jnp.int32, (LANES, LANES), 0) // HEAD_DIM
    c = lax.broadcasted_iota(jnp.int32, (LANES, LANES), 1) // HEAD_DIM
    return jnp.where(r == c, 1.0, 0.0).astype(BF16)


def _head_sum(x, ones_bd):
    hi = x.astype(BF16)
    lo = (x - hi.astype(F32)).astype(BF16)
    return _dot(jnp.concatenate([hi, lo], axis=1), jnp.concatenate([ones_bd, ones_bd], axis=0))


def _rms_mm_kernel(x_ref, g_ref, wt_ref, o_ref, xn_ref):
    @pl.when(pl.program_id(1) == 0)
    def _():
        x = x_ref[...]
        ms = jnp.mean(x * x, axis=-1, keepdims=True)
        xn_ref[...] = ((x * lax.rsqrt(ms + RMS_EPS)) * g_ref[...]).astype(BF16)

    o_ref[...] = _dot_nt(xn_ref[...], wt_ref[...]).astype(o_ref.dtype)


def _rms_glu_kernel(x_ref, g_ref, wg_ref, wu_ref, o_ref, xn_ref):
    @pl.when(pl.program_id(1) == 0)
    def _():
        x = x_ref[...]
        ms = jnp.mean(x * x, axis=-1, keepdims=True)
        xn_ref[...] = ((x * lax.rsqrt(ms + RMS_EPS)) * g_ref[...]).astype(BF16)

    xn = xn_ref[...]
    gate = _dot(xn, wg_ref[...])
    up = _dot(xn, wu_ref[...])
    o_ref[...] = (gate * _sigmoid(gate) * up).astype(o_ref.dtype)


def _rms_matmul(x2d, g, wt_bf16, *, tm, tn, out_dtype=F32):
    m, d = x2d.shape
    n = wt_bf16.shape[0]
    return pl.pallas_call(
        _rms_mm_kernel,
        grid=(m // tm, n // tn),
        in_specs=[pl.BlockSpec((tm, d), lambda i, j: (i, 0)),
                  pl.BlockSpec((1, d), lambda i, j: (0, 0)),
                  pl.BlockSpec((tn, d), lambda i, j: (j, 0))],
        out_specs=pl.BlockSpec((tm, tn), lambda i, j: (i, j)),
        out_shape=jax.ShapeDtypeStruct((m, n), out_dtype),
        scratch_shapes=[pltpu.VMEM((tm, d), BF16)],
        compiler_params=pltpu.CompilerParams(
            dimension_semantics=("parallel", "arbitrary"), vmem_limit_bytes=VMEM_LIMIT),
        name="rms_matmul",
    )(x2d, g.reshape(1, d), wt_bf16)


def _rms_matmul_glu(x2d, g, w_bf16, *, tm, tn):
    m, d = x2d.shape
    n = w_bf16.shape[1] // 2
    up_off = n // tn
    return pl.pallas_call(
        _rms_glu_kernel,
        grid=(m // tm, n // tn),
        in_specs=[pl.BlockSpec((tm, d), lambda i, j: (i, 0)),
                  pl.BlockSpec((1, d), lambda i, j: (0, 0)),
                  pl.BlockSpec((d, tn), lambda i, j: (0, j)),
                  pl.BlockSpec((d, tn), lambda i, j: (0, j + up_off))],
        out_specs=pl.BlockSpec((tm, tn), lambda i, j: (i, j)),
        out_shape=jax.ShapeDtypeStruct((m, n), BF16),
        scratch_shapes=[pltpu.VMEM((tm, d), BF16)],
        compiler_params=pltpu.CompilerParams(
            dimension_semantics=("parallel", "arbitrary"), vmem_limit_bytes=VMEM_LIMIT),
        name="rms_matmul_glu",
    )(x2d, g.reshape(1, d), w_bf16, w_bf16)


def _mm_res_kernel(a_ref, b_ref, r_ref, o_ref):
    o_ref[...] = _dot(a_ref[...], b_ref[...]) + r_ref[...]


def _matmul_res(a, b, res, *, tm, tn):
    m, kk = a.shape
    n = b.shape[1]
    return pl.pallas_call(
        _mm_res_kernel,
        grid=(m // tm, n // tn),
        in_specs=[pl.BlockSpec((tm, kk), lambda i, j: (i, 0)),
                  pl.BlockSpec((kk, tn), lambda i, j: (0, j)),
                  pl.BlockSpec((tm, tn), lambda i, j: (i, j))],
        out_specs=pl.BlockSpec((tm, tn), lambda i, j: (i, j)),
        out_shape=jax.ShapeDtypeStruct((m, n), F32),
        compiler_params=pltpu.CompilerParams(
            dimension_semantics=("parallel", "arbitrary"), vmem_limit_bytes=VMEM_LIMIT),
        name="matmul_res",
    )(a, b, res)


def _merge_kernel(ya_ref, yb_ref, wa_ref, wb_ref, za_ref, zb_ref, o_ref):
    ta = _dot(ya_ref[...], wa_ref[...])
    tb = _dot(yb_ref[...], wb_ref[...])
    o_ref[...] = (_sigmoid(za_ref[...]) * ta + _sigmoid(zb_ref[...]) * tb).astype(o_ref.dtype)


def _merge(ya, yb, wa, wb, proj2d, *, tm, tn):
    m, kk = ya.shape
    n = wa.shape[1]
    oa, ob = C_GA // tn, C_GB // tn
    return pl.pallas_call(
        _merge_kernel,
        grid=(m // tm, n // tn),
        in_specs=[pl.BlockSpec((tm, kk), lambda i, j: (i, 0)),
                  pl.BlockSpec((tm, kk), lambda i, j: (i, 0)),
                  pl.BlockSpec((kk, tn), lambda i, j: (0, j)),
                  pl.BlockSpec((kk, tn), lambda i, j: (0, j)),
                  pl.BlockSpec((tm, tn), lambda i, j: (i, j + oa)),
                  pl.BlockSpec((tm, tn), lambda i, j: (i, j + ob))],
        out_specs=pl.BlockSpec((tm, tn), lambda i, j: (i, j)),
        out_shape=jax.ShapeDtypeStruct((m, n), BF16),
        compiler_params=pltpu.CompilerParams(
            dimension_semantics=("parallel", "arbitrary"), vmem_limit_bytes=VMEM_LIMIT),
        name="merge",
    )(ya, yb, wa, wb, proj2d, proj2d)


def _rwkv_prep_kernel(z_ref, zs_ref, pz_ref, ps_ref, p0z_ref, p0s_ref, mu_ref, mus_ref,
                      w0_ref, w2_ref, a0_ref, a2_ref, g2_ref, kk_ref, ka_ref,
                      r_out, lw_out, k_out, v_out, a_out, b_out, g_out, *, tt):
    first = pl.program_id(1) == 0
    z = z_ref[0]
    zs = zs_ref[0]
    pz = jnp.where(first, p0z_ref[7:8, :], pz_ref[0, 7:8, :])
    ps = jnp.where(first, p0s_ref[7:8, :], ps_ref[0, 7:8, :])
    row = lax.broadcasted_iota(jnp.int32, (tt, 1), 0)

    def lerp(cur, prev_last, mu):
        prev = jnp.where(row == 0, prev_last, pltpu.roll(cur, 1, axis=0))
        return cur + (prev - cur) * mu

    zm = lerp(z, pz, mu_ref[...])
    zsm = lerp(zs, ps, mus_ref[...])
    r = zm[:, C_R:C_R + WIDTH]
    k = zm[:, C_K:C_K + WIDTH]
    v = zm[:, C_V:C_V + WIDTH]
    wd = zsm[:, 0:LANES]
    ad = zsm[:, LANES:2 * LANES]
    gd = zsm[:, 2 * LANES:4 * LANES]

    w_log = -_softplus(-(w0_ref[...] + _dot(jnp.tanh(wd).astype(BF16), w2_ref[...]))) - 0.5
    lw_out[0] = -jnp.exp(w_log)
    a_sig = _sigmoid(a0_ref[...] + _dot(ad.astype(BF16), a2_ref[...]))
    g_out[0] = _dot(_sigmoid(gd).astype(BF16), g2_ref[...]).astype(g_out.dtype)

    ones_bd = _head_ones()
    kk = k * kk_ref[...]
    kk2 = kk * kk
    nrm = jnp.concatenate(
        [_head_sum(kk2[:, p * LANES:(p + 1) * LANES], ones_bd) for p in range(PAIRS)], axis=1)
    kk = kk * lax.rsqrt(jnp.maximum(nrm, 1e-24))
    r_out[0] = r.astype(r_out.dtype)
    k_out[0] = (k * (1.0 + (a_sig - 1.0) * ka_ref[...])).astype(k_out.dtype)
    v_out[0] = v.astype(v_out.dtype)
    a_out[0] = (-kk).astype(a_out.dtype)
    b_out[0] = (kk * a_sig).astype(b_out.dtype)


def _rwkv_prep(proj3d, prev0_z, prev0_s, mu_z, mu_s, w0, w2p, a0, a2p, g2, k_k, k_a, *, tt):
    b, l, _ = proj3d.shape
    s_blk = C_S // S_COLS
    zw = 3 * WIDTH
    row = lambda a: a.reshape(1, -1)
    vec = lambda n: pl.BlockSpec((1, n), lambda bi, i: (0, 0))
    full = lambda r, c: pl.BlockSpec((r, c), lambda bi, i: (0, 0))
    out_spec = pl.BlockSpec((1, tt, WIDTH), lambda bi, i: (bi, i, 0))
    sds = lambda dt: jax.ShapeDtypeStruct((b, l, WIDTH), dt)
    return pl.pallas_call(
        functools.partial(_rwkv_prep_kernel, tt=tt),
        grid=(b, l // tt),
        in_specs=[pl.BlockSpec((1, tt, zw), lambda bi, i: (bi, i, 0)),
                  pl.BlockSpec((1, tt, S_COLS), lambda bi, i: (bi, i, s_blk)),
                  pl.BlockSpec((1, 8, zw), lambda bi, i: (bi, jnp.maximum(i * (tt // 8) - 1, 0), 0)),
                  pl.BlockSpec((1, 8, S_COLS),
                               lambda bi, i: (bi, jnp.maximum(i * (tt // 8) - 1, 0), s_blk)),
                  full(8, zw), full(8, S_COLS),
                  vec(zw), vec(S_COLS),
                  vec(WIDTH), full(LANES, WIDTH), vec(WIDTH), full(LANES, WIDTH),
                  full(GATE_LORA, WIDTH), vec(WIDTH), vec(WIDTH)],
        out_specs=[out_spec] * 7,
        out_shape=[sds(BF16), sds(F32)] + [sds(BF16)] * 5,
        compiler_params=pltpu.CompilerParams(
            dimension_semantics=("parallel", "arbitrary"), vmem_limit_bytes=VMEM_LIMIT),
        name="rwkv_prep",
    )(proj3d, proj3d, proj3d, proj3d, prev0_z, prev0_s, row(mu_z), row(mu_s),
      row(w0), w2p, row(a0), a2p, g2, row(k_k), row(k_a))


def _bf(x):
    return x.astype(BF16)


WKV_CHUNKS = 4


def _wkv_group(a_t, r_t, b_t, k_t, b_h, k_h, v, s, w_total, masks):
    head0, strict, lower, eye = masks
    dg = functools.partial(lax.dot_general, preferred_element_type=F32)
    n = range(len(s))

    def bd(x):
        return jnp.concatenate([jnp.where(head0, x, 0.0), jnp.where(head0, 0.0, x)], axis=0)

    lhs = [_bf(jnp.concatenate([bd(a_t[i]), bd(r_t[i])], axis=0)) for i in n]
    rhs = [_bf(jnp.concatenate([bd(b_t[i]), bd(k_t[i])], axis=0)) for i in n]
    v_bd = [_bf(bd(v[i])) for i in n]
    gram = [dg(lhs[i], rhs[i], _NT) for i in n]
    a_s = [dg(lhs[i], _bf(s[i]), _NT) for i in n]
    n_ab = [jnp.where(strict, g[0:LANES, 0:LANES], 0.0) for g in gram]
    a_ak = [jnp.where(strict, g[0:LANES, LANES:2 * LANES], 0.0) for g in gram]
    a_r = [_bf(jnp.concatenate([jnp.where(lower, g[LANES:2 * LANES, 0:LANES], 0.0),
                                jnp.where(lower, g[LANES:2 * LANES, LANES:2 * LANES], 0.0)], axis=1))
           for g in gram]
    rhs_u = [a_s[i][0:LANES] + dg(_bf(a_ak[i]), v_bd[i], _NN) for i in n]

    p_inv = [eye + x for x in n_ab]
    q = [_bf(x) for x in n_ab]
    for _ in range(5):
        q = [_bf(dg(x, x, _NN)) for x in q]
        p_inv = [p_inv[i] + dg(q[i], _bf(p_inv[i]), _NN) for i in n]
    u = [dg(_bf(p_inv[i]), _bf(rhs_u[i]), _NN) for i in n]

    uv = [jnp.concatenate([_bf(u[i]), v_bd[i]], axis=0) for i in n]
    y_bd = [a_s[i][LANES:2 * LANES] + dg(a_r[i], uv[i], _NN) for i in n]
    y = [x[0:CHUNK] + x[CHUNK:2 * CHUNK] for x in y_bd]
    s_new = [w_total[i] * s[i]
             + dg(uv[i], _bf(jnp.concatenate([bd(b_h[i]), bd(k_h[i])], axis=0)), _TN) for i in n]
    return y, s_new


def _wkv_kernel(r_ref, lw_ref, k_ref, v_ref, a_ref, b_ref, g_ref, s0_ref,
                gnw_ref, gnb_ref, rk_ref, y_ref, sfin_ref, s_scr, *, nc, chunks):
    c = pl.program_id(1)

    @pl.when(c == 0)
    def _():
        s_scr[...] = s0_ref[0]

    rows_all = chunks * CHUNK
    r = r_ref[0].astype(F32)
    lw = lw_ref[0]
    k = k_ref[0].astype(F32)
    v = v_ref[0].astype(F32)
    a = a_ref[0].astype(F32)
    b = b_ref[0].astype(F32)

    ti = lax.broadcasted_iota(jnp.int32, (rows_all, rows_all), 0)
    tj = lax.broadcasted_iota(jnp.int32, (rows_all, rows_all), 1)
    ltri = jnp.where((tj <= ti) & (ti // CHUNK == tj // CHUNK), 1.0, 0.0).astype(BF16)
    hi, mid, lo = _split3(lw)
    cum = _dot(jnp.concatenate([ltri, ltri, ltri], axis=1),
               jnp.concatenate([hi, mid, lo], axis=0))

    lane = lax.broadcasted_iota(jnp.int32, (1, LANES), 1)
    bi = lax.broadcasted_iota(jnp.int32, (LANES, LANES), 0)
    bj = lax.broadcasted_iota(jnp.int32, (LANES, LANES), 1)
    same = (bi // CHUNK) == (bj // CHUNK)
    masks = (lane < HEAD_DIM,
             same & ((bj % CHUNK) < (bi % CHUNK)),
             same & ((bj % CHUNK) <= (bi % CHUNK)),
             jnp.where(bi == bj, 1.0, 0.0))

    ones_bd = _head_ones()
    inv_n = 1.0 / HEAD_DIM
    sls = [slice(p * LANES, (p + 1) * LANES) for p in range(PAIRS)]
    cut = lambda x: [x[:, sl] for sl in sls]
    state = [s_scr[p] for p in range(PAIRS)]
    for sc in range(chunks):
        rows = slice(sc * CHUNK, (sc + 1) * CHUNK)
        cum_c, lw_c = cum[rows], lw[rows]
        total = cum_c[CHUNK - 1:CHUNK, :]
        e_inv = jnp.exp(-cum_c)
        e_end = jnp.exp(total - cum_c)
        ys, state = _wkv_group(
            cut(a[rows] * jnp.exp(cum_c - lw_c)), cut(r[rows] * jnp.exp(cum_c)),
            cut(b[rows] * e_inv), cut(k[rows] * e_inv), cut(b[rows] * e_end), cut(k[rows] * e_end),
            cut(v[rows]), state, cut(jnp.exp(total)), masks)

        y_all = jnp.concatenate(ys, axis=0)
        rk_all = jnp.concatenate([r[rows, sl] * k[rows, sl] * rk_ref[:, sl] for sl in sls], axis=0)
        bonus_all = _head_sum(rk_all, ones_bd)
        d_all = y_all - _head_sum(y_all, ones_bd) * inv_n
        rstd_all = lax.rsqrt(_head_sum(d_all * d_all, ones_bd) * inv_n + GN_EPS)
        for i, sl in enumerate(sls):
            prow = slice(i * CHUNK, (i + 1) * CHUNK)
            yn = d_all[prow] * rstd_all[prow] * gnw_ref[:, sl] + gnb_ref[:, sl]
            y_ref[0, rows, sl] = ((yn + bonus_all[prow] * v[rows, sl])
                                  * g_ref[0, rows, sl].astype(F32)).astype(y_ref.dtype)

    for p in range(PAIRS):
        s_scr[p] = state[p]

        @pl.when(c == nc - 1)
        def _(p=p):
            sfin_ref[0, p] = state[p]


def _wkv(r, lw, k, v, a, b, g, s0, gn_w, gn_b, r_k):
    bsz, l, _ = r.shape
    chunks = min(WKV_CHUNKS, l // CHUNK)
    nc = l // (chunks * CHUNK)
    tok = pl.BlockSpec((1, chunks * CHUNK, WIDTH), lambda bi, c: (bi, c, 0))
    vec = pl.BlockSpec((1, WIDTH), lambda bi, c: (0, 0))
    st_in = pl.BlockSpec((1, PAIRS, LANES, LANES), lambda bi, c: (0, 0, 0, 0))
    st_out = pl.BlockSpec((1, PAIRS, LANES, LANES), lambda bi, c: (bi, 0, 0, 0))
    row = lambda x: x.reshape(1, WIDTH)
    return pl.pallas_call(
        functools.partial(_wkv_kernel, nc=nc, chunks=chunks),
        grid=(bsz, nc),
        in_specs=[tok] * 7 + [st_in, vec, vec, vec],
        out_specs=[tok, st_out],
        out_shape=[jax.ShapeDtypeStruct((bsz, l, WIDTH), BF16),
                   jax.ShapeDtypeStruct((bsz, PAIRS, LANES, LANES), F32)],
        scratch_shapes=[pltpu.VMEM((PAIRS, LANES, LANES), F32)],
        compiler_params=pltpu.CompilerParams(
            dimension_semantics=("parallel", "arbitrary"), vmem_limit_bytes=VMEM_LIMIT),
        name="wkv",
    )(r, lw, k, v, a, b, g, s0, row(gn_w), row(gn_b), row(r_k))


def _fox_prep_kernel(z_ref, zs_ref, c0_ref, qg_ref, kg_ref, fb_ref,
                     qt_out, k_out, kb_out, vt_out, c_out, carry, *, tt, n_pad):
    i = pl.program_id(1)

    @pl.when(i == 0)
    def _():
        carry[...] = c0_ref[...]

    z = z_ref[0]
    fl = zs_ref[0][:, 0:LANES]
    log_f = -_softplus(-(fl + fb_ref[...]))
    ti = lax.broadcasted_iota(jnp.int32, (tt, tt), 0)
    tj = lax.broadcasted_iota(jnp.int32, (tt, tt), 1)
    ltri = jnp.where(tj <= ti, 1.0, 0.0).astype(BF16)
    hi, mid, lo = _split3(log_f)
    cum = carry[...] + _dot(jnp.concatenate([ltri, ltri, ltri], axis=1),
                            jnp.concatenate([hi, mid, lo], axis=0))
    carry[...] = cum[tt - 1:tt, :]
    c_out[0] = cum[tt - 8:tt, :]
    c_hi, c_mid, c_lo = (t.astype(F32) for t in _split3(cum * LOG2E))
    neg_hi, neg_mid, neg_lo = -c_hi, -c_mid, -c_lo
    if n_pad:
        grow = i * tt + lax.broadcasted_iota(jnp.int32, (tt, 1), 0)
        neg_hi = jnp.where(grow < n_pad, MASKED_KEY, neg_hi)

    for p in range(PAIRS):
        vt_out[0, p, 0] = z[:, 2 * WIDTH + p * LANES:2 * WIDTH + (p + 1) * LANES].T.astype(BF16)
    ones_bd = _head_ones()
    lane = lax.broadcasted_iota(jnp.int32, (1, LANES), 1)
    low_half = lane < HEAD_DIM
    inv_n = 1.0 / HEAD_DIM
    for p in range(PAIRS):
        zq = z[:, p * LANES:(p + 1) * LANES]
        zk = z[:, WIDTH + p * LANES:WIDTH + (p + 1) * LANES]
        qn = zq * lax.rsqrt(_head_sum(zq * zq, ones_bd) * inv_n + RMS_EPS) * (qg_ref[...] * (ATTN_SCALE * LOG2E))
        kn = zk * lax.rsqrt(_head_sum(zk * zk, ones_bd) * inv_n + RMS_EPS) * kg_ref[...]
        qt_out[0, p, 0] = qn.T.astype(BF16)
        k_out[0, p] = kn.astype(BF16)
        e = jnp.where(low_half, lane, lane - HEAD_DIM)
        col = lambda t, h: t[:, FL_LANE + h:FL_LANE + h + 1]
        bias = lambda h: jnp.where(e == 0, col(neg_hi, h), jnp.where(
            e == 1, col(neg_mid, h), jnp.where(e == 2, col(neg_lo, h), 0.0)))
        kb_out[0, p] = jnp.where(low_half, bias(2 * p + 1), bias(2 * p)).astype(BF16)


def _fox_prep(proj3d, c0, q_g, k_g, f_bias, *, tt, n_pad):
    b, l, _ = proj3d.shape
    zw = 3 * WIDTH
    tile2 = lambda g: jnp.concatenate([g, g]).reshape(1, LANES)
    fb = jnp.concatenate([jnp.zeros((FL_LANE,), F32), f_bias,
                          jnp.zeros((LANES - FL_LANE - HEADS,), F32)]).reshape(1, LANES)
    tok = pl.BlockSpec((1, PAIRS, tt, LANES), lambda bi, i: (bi, 0, i, 0))
    tr = pl.BlockSpec((1, PAIRS, 1, LANES, tt), lambda bi, i: (bi, 0, i, 0, 0))
    vec = pl.BlockSpec((1, LANES), lambda bi, i: (0, 0))
    return pl.pallas_call(
        functools.partial(_fox_prep_kernel, tt=tt, n_pad=n_pad),
        grid=(b, l // tt),
        in_specs=[pl.BlockSpec((1, tt, zw), lambda bi, i: (bi, i, C_FQ // zw)),
                  pl.BlockSpec((1, tt, S_COLS), lambda bi, i: (bi, i, C_S // S_COLS)),
                  vec, vec, vec, vec],
        out_specs=[tr, tok, tok, tr,
                   pl.BlockSpec((1, 8, LANES), lambda bi, i: (bi, i, 0))],
        out_shape=[jax.ShapeDtypeStruct((b, PAIRS, l // tt, LANES, tt), BF16),
                   jax.ShapeDtypeStruct((b, PAIRS, l, LANES), BF16),
                   jax.ShapeDtypeStruct((b, PAIRS, l, LANES), BF16),
                   jax.ShapeDtypeStruct((b, PAIRS, l // tt, LANES, tt), BF16),
                   jax.ShapeDtypeStruct((b, (l // tt) * 8, LANES), F32)],
        scratch_shapes=[pltpu.VMEM((1, LANES), F32)],
        compiler_params=pltpu.CompilerParams(
            dimension_semantics=("parallel", "arbitrary"), vmem_limit_bytes=VMEM_LIMIT),
        name="fox_prep",
    )(proj3d, proj3d, c0, tile2(q_g), tile2(k_g), fb)


ATTN_SUBTILES = 4


def _attn_kernel(qt_ref, k_ref, kb_ref, vt_ref, km_ref, kbm_ref, vmt_ref, o_ref,
                 s0_buf, s1_buf, p0_buf, p1_buf,
                 acc_buf, *, tk):
    hq = ATTN_SUBTILES
    s_buf = (s0_buf, s1_buf)
    p_buf = (p0_buf, p1_buf)
    qi = pl.program_id(2)
    key_i = lax.broadcasted_iota(jnp.int32, (tk, tk), 0)
    qry_i = lax.broadcasted_iota(jnp.int32, (tk, tk), 1)
    causal = key_i <= qry_i
    every = tuple(range(2 * hq))
    from_sub = lambda u0: tuple(n for n in every if n % hq >= u0)
    ones_rows = jnp.where(lax.broadcasted_iota(jnp.int32, (HEAD_DIM, tk), 0) < 3, 1.0, 0.0).astype(BF16)
    qts = []
    for n in every:
        q_pair = qt_ref[0, 0, n % hq]
        qts.append(jnp.concatenate([q_pair[:HEAD_DIM], ones_rows] if n < hq
                                   else [ones_rows, q_pair[HEAD_DIM:]], axis=0))
    low_half = lax.broadcasted_iota(jnp.int32, (1, LANES), 1) < HEAD_DIM

    def per_head(k_pair, bias_pair):
        return jnp.where(low_half, k_pair, bias_pair), jnp.where(low_half, bias_pair, k_pair)

    def real_keys(r):
        off = pl.multiple_of(r * tk, tk)
        return per_head(k_ref[0, 0, pl.ds(off, tk), :], kb_ref[0, 0, pl.ds(off, tk), :])

    def stage_a(kblks, slot, which):
        for n in which:
            s_buf[slot][n] = _dot(kblks[n // hq], qts[n])

    def stage_b(slot, stats, alpha, masks):
        which = tuple(masks)
        s = {n: s_buf[slot][n] for n in which}
        for n in which:
            if masks[n] is not None:
                s[n] = jnp.where(masks[n], s[n], MASKED_KEY)
        m_new = {n: jnp.maximum(stats[n][0], jnp.max(s[n], axis=0, keepdims=True)) for n in which}
        a_new = {n: jnp.exp2(stats[n][0] - m_new[n]) for n in which}
        p = {n: jnp.exp2(s[n] - m_new[n]) for n in which}
        for n in which:
            p_buf[slot][n] = p[n].astype(BF16)
        l_new = {n: a_new[n] * stats[n][1] + jnp.sum(p[n], axis=0, keepdims=True) for n in which}
        stats = tuple((m_new[n], l_new[n]) if n in which else stats[n] for n in every)
        alpha = tuple(a_new[n] if n in which else alpha[n] for n in every)
        return stats, alpha

    def stage_c(slot, e, alpha, which):
        vt = jnp.where(e == 0, vmt_ref[0, 0, 0], vt_ref[0, 0, jnp.maximum(e - 1, 0)])
        pv = {n: _dot(vt[(n // hq) * HEAD_DIM:(n // hq + 1) * HEAD_DIM, :], p_buf[slot][n])
              for n in which}
        for n in which:
            acc_buf[n] = alpha[n] * acc_buf[n] + pv[n]

    unmasked = {n: None for n in every}
    stats = ((jnp.full((1, tk), MASKED_KEY, F32), jnp.zeros((1, tk), F32)),) * len(every)
    alpha = (jnp.ones((1, tk), F32),) * len(every)
    acc_buf[...] = jnp.zeros_like(acc_buf)
    p1_buf[...] = jnp.zeros_like(p1_buf)
    stage_a(per_head(km_ref[0, 0], kbm_ref[0, 0]), 0, every)

    def body(i, state):
        stats, alpha_prev = state
        stats, alpha_a = stage_b(0, stats, alpha_prev, unmasked)
        stage_a(real_keys(2 * i), 1, every)
        stage_c(1, jnp.maximum(2 * i - 1, 0), alpha_prev, every)
        stage_a(real_keys(2 * i + 1), 0, every)
        stats, alpha_b = stage_b(1, stats, alpha_a, unmasked)
        stage_c(0, 2 * i, alpha_a, every)
        return stats, alpha_b

    stats, alpha_prev = lax.fori_loop(0, (hq // 2) * qi, body, (stats, alpha))
    e0 = hq * qi
    alphas = {-1: alpha_prev}
    for r in range(1, hq + 3):
        if r <= hq:
            stage_a(real_keys(e0 + r - 1), r % 2, from_sub(r - 1))
        j = r - 1
        if j <= hq:
            masks = unmasked if j == 0 else {n: (causal if n % hq == j - 1 else None)
                                             for n in from_sub(j - 1)}
            stats, alphas[j] = stage_b(j % 2, stats, alphas[j - 1], masks)
        j = r - 2
        if j == -1:
            stage_c(1, jnp.maximum(e0 - 1, 0), alphas[-1], every)
        else:
            stage_c(j % 2, e0 + j, alphas[j], every if j == 0 else from_sub(j - 1))
    for u in range(hq):
        out_t = jnp.concatenate([acc_buf[u] / stats[u][1], acc_buf[hq + u] / stats[hq + u][1]], axis=0)
        o_ref[0, u * tk:(u + 1) * tk, :] = out_t.T.astype(o_ref.dtype)


def _attn(qt, kp, kb, vt, kp_meta, kb_meta, vt_meta, *, tk):
    b, _, l, _ = kp.shape
    nk = vt.shape[2]
    hq = ATTN_SUBTILES
    assert vt.shape[4] == tk and qt.shape[4] == tk and kp_meta.shape[2] == tk and l % (hq * tk) == 0
    keys = pl.BlockSpec((1, 1, l, LANES), lambda bi, p, qi: (bi, p, 0, 0))
    meta_keys = pl.BlockSpec((1, 1, tk, LANES), lambda bi, p, qi: (0, p, 0, 0))
    return pl.pallas_call(
        functools.partial(_attn_kernel, tk=tk),
        grid=(b, PAIRS, l // (hq * tk)),
        in_specs=[pl.BlockSpec((1, 1, hq, LANES, tk), lambda bi, p, qi: (bi, p, qi, 0, 0)),
                  keys, keys,
                  pl.BlockSpec((1, 1, nk, LANES, tk), lambda bi, p, qi: (bi, p, 0, 0, 0)),
                  meta_keys, meta_keys,
                  pl.BlockSpec((1, 1, 1, LANES, tk), lambda bi, p, qi: (0, p, 0, 0, 0))],
        out_specs=pl.BlockSpec((1, hq * tk, LANES), lambda bi, p, qi: (bi, qi, p)),
        out_shape=jax.ShapeDtypeStruct((b, l, WIDTH), BF16),
        scratch_shapes=[pltpu.VMEM((2 * hq, tk, tk), F32), pltpu.VMEM((2 * hq, tk, tk), F32),
                        pltpu.VMEM((2 * hq, tk, tk), BF16), pltpu.VMEM((2 * hq, tk, tk), BF16),
                        pltpu.VMEM((2 * hq, HEAD_DIM, tk), F32)],
        compiler_params=pltpu.CompilerParams(
            dimension_semantics=("parallel", "parallel", "arbitrary"), vmem_limit_bytes=VMEM_LIMIT),
        name="attn",
    )(qt, kp, kb, vt, kp_meta, kb_meta, vt_meta)


def _pick(n, prefs):
    for t in prefs:
        if n % t == 0:
            return t
    raise ValueError(f"no tile for {n}")


def _pack_wt_kernel(wt_ref, o_ref):
    rw, fx = 3 * WIDTH + DECAY_LORA + AAA_LORA + GATE_LORA, 3 * WIDTH + HEADS
    wd0, ad0, gd0 = 3 * WIDTH, 3 * WIDTH + DECAY_LORA, 3 * WIDTH + DECAY_LORA + AAA_LORA
    fl0 = rw + 3 * WIDTH

    def put(dst, src, n):
        o_ref[dst:dst + n, :] = wt_ref[src:src + n, :].astype(BF16)

    def zero(dst, n):
        o_ref[dst:dst + n, :] = jnp.zeros((n, o_ref.shape[1]), BF16)

    put(C_R, 0, 3 * WIDTH)
    put(C_FQ, rw, 3 * WIDTH)
    put(C_GA, rw + fx, 2 * D_MODEL)
    put(C_S, wd0, DECAY_LORA)
    put(C_S + FL_LANE, fl0, HEADS)
    zero(C_S + FL_LANE + HEADS, LANES - FL_LANE - HEADS)
    put(C_S + LANES, ad0, AAA_LORA)
    zero(C_S + LANES + AAA_LORA, LANES - AAA_LORA)
    put(C_S + 2 * LANES, gd0, GATE_LORA)


def _pack_w_in_t(wt, *, cols=256):
    n, d = wt.shape
    return pl.pallas_call(
        _pack_wt_kernel,
        grid=(d // cols,),
        in_specs=[pl.BlockSpec((n, cols), lambda i: (0, i))],
        out_specs=pl.BlockSpec((N_PACK, cols), lambda i: (0, i)),
        out_shape=jax.ShapeDtypeStruct((N_PACK, d), BF16),
        compiler_params=pltpu.CompilerParams(
            dimension_semantics=("parallel",), vmem_limit_bytes=VMEM_LIMIT),
        name="pack_w_in",
    )(wt)


def _pack_mu(mu):
    wd0, ad0, gd0 = 3 * WIDTH, 3 * WIDTH + DECAY_LORA, 3 * WIDTH + DECAY_LORA + AAA_LORA
    z = lambda n: jnp.zeros((n,), mu.dtype)
    mu_s = jnp.concatenate([mu[wd0:ad0], z(LANES - DECAY_LORA), mu[ad0:gd0], z(LANES - AAA_LORA),
                            mu[gd0:]])
    return mu[:3 * WIDTH], mu_s


def _pad_rows(w, rows):
    return jnp.concatenate([w, jnp.zeros((rows - w.shape[0], w.shape[1]), w.dtype)], axis=0)


def _mixer_inputs(h2d, bsz, norm_g, w_pack, tm, tn):
    proj = _rms_matmul(h2d, norm_g, w_pack, tm=tm, tn=tn)
    return proj, proj.reshape(bsz, h2d.shape[0] // bsz, N_PACK)


def kernel(x, meta_tokens, norm1_g, w_in, rwkv_mu, rwkv_w0, rwkv_w2, rwkv_a0, rwkv_a2, rwkv_g2,
           rwkv_k_k, rwkv_k_a, rwkv_r_k, rwkv_gn_w, rwkv_gn_b, fox_q_norm_g, fox_k_norm_g,
           fox_f_bias, w_branch_a, w_branch_b, w_o, norm2_g, w_gate_up, w_down):
    bsz, seq, d = x.shape
    assert d == D_MODEL and norm1_g.shape[0] == 1 and seq % (ATTN_SUBTILES * 256) == 0
    m = bsz * seq
    x2d = x.reshape(m, d)
    meta2d = jnp.concatenate(
        [jnp.zeros((META_ROWS - N_META, d), x.dtype), meta_tokens.astype(x.dtype)], axis=0)

    w_pack = _pack_w_in_t(jnp.swapaxes(w_in[0], 0, 1))
    mu_z, mu_s = _pack_mu(rwkv_mu[0])
    w2p = _pad_rows(rwkv_w2[0], LANES).astype(BF16)
    a2p = _pad_rows(rwkv_a2[0], LANES).astype(BF16)
    g2 = rwkv_g2[0].astype(BF16)
    r_k = rwkv_r_k[0].reshape(WIDTH)

    tm = _pick(m, (1024, 512, 256))
    tt = _pick(seq, (256,))
    tn_proj, tn_wo, tn_ffn = 1536, D_MODEL, 512

    _, pm3 = _mixer_inputs(meta2d, 1, norm1_g[0], w_pack, META_ROWS, tn=tn_proj)
    zero_z = jnp.zeros((8, 3 * WIDTH), F32)
    zero_s = jnp.zeros((8, S_COLS), F32)
    prep = functools.partial(_rwkv_prep, mu_z=mu_z, mu_s=mu_s, w0=rwkv_w0[0], w2p=w2p,
                             a0=rwkv_a0[0], a2p=a2p, g2=g2, k_k=rwkv_k_k[0], k_a=rwkv_k_a[0])
    wkv = functools.partial(_wkv, gn_w=rwkv_gn_w[0], gn_b=rwkv_gn_b[0], r_k=r_k)
    fox = functools.partial(_fox_prep, q_g=fox_q_norm_g[0], k_g=fox_k_norm_g[0],
                            f_bias=fox_f_bias[0])
    mr = prep(pm3, zero_z, zero_s, tt=META_ROWS)
    _, s_meta = wkv(*mr, jnp.zeros((1, PAIRS, LANES, LANES), F32))
    _, kp_meta, kb_meta, vt_meta, c_meta = fox(pm3, jnp.zeros((1, LANES), F32), tt=META_ROWS,
                                      n_pad=META_ROWS - N_META)

    proj2d, p3 = _mixer_inputs(x2d, bsz, norm1_g[0], w_pack, tm, tn=tn_proj)
    rr = prep(p3, pm3[0, META_ROWS - 8:, 0:3 * WIDTH], pm3[0, META_ROWS - 8:, C_S:], tt=tt)
    ya, _ = wkv(*rr, s_meta)
    qt, kp, kb, vt, _ = fox(p3, c_meta[0, 7:8, :], tt=tt, n_pad=0)
    yb = _attn(qt, kp, kb, vt, kp_meta, kb_meta, vt_meta, tk=tt)

    merged = _merge(ya.reshape(m, WIDTH), yb.reshape(m, WIDTH), w_branch_a[0].astype(BF16),
                    w_branch_b[0].astype(BF16), proj2d, tm=tm // 2, tn=D_MODEL)
    h1 = _matmul_res(merged, w_o[0].astype(BF16), x2d, tm=tm // 2, tn=tn_wo)
    act = _rms_matmul_glu(h1, norm2_g[0], w_gate_up[0].astype(BF16), tm=tm, tn=tn_ffn)
    out = _matmul_res(act, w_down[0].astype(BF16), h1, tm=tm, tn=tn_ffn)
    return out.reshape(bsz, seq, d)
```

```python
import functools

import jax
import jax.numpy as jnp
from jax import lax
from jax.experimental import pallas as pl
from jax.experimental.pallas import tpu as pltpu

F32 = jnp.float32
BF16 = jnp.bfloat16

D_MODEL = 2048
N_META = 16
HEAD_DIM = 64
HEADS = 16
WIDTH = HEADS * HEAD_DIM
PAIRS = HEADS // 2
DECAY_LORA = 96
AAA_LORA = 96
GATE_LORA = 256
D_FF = 5632
RMS_EPS = 1e-6
GN_EPS = 64e-5
ATTN_SCALE = HEAD_DIM ** -0.5
LOG2E = 1.4426950408889634

LANES = 128
META_ROWS = 256
CHUNK = 64
MASKED_KEY = -1e30

C_R, C_K, C_V = 0, 1024, 2048
C_FQ, C_FK, C_FV = 3072, 4096, 5120
C_GA, C_GB = 6144, 8192
C_S = 10240
S_COLS = 512
FL_LANE = 96
N_PACK = C_S + S_COLS

VMEM_LIMIT = 56 * 1024 * 1024


def _sigmoid(x):
    return 0.5 * jnp.tanh(0.5 * x) + 0.5


def _softplus(x):
    return jnp.maximum(x, 0.0) + jnp.log(1.0 + jnp.exp(-jnp.abs(x)))


def _dot(a, b):
    return jnp.dot(a, b, preferred_element_type=F32)


def _dot_nt(a, b):
    return lax.dot_general(a, b, (((1,), (1,)), ((), ())), preferred_element_type=F32)


def _split3(x):
    hi = x.astype(BF16)
    r1 = x - hi.astype(F32)
    mid = r1.astype(BF16)
    lo = (r1 - mid.astype(F32)).astype(BF16)
    return hi, mid, lo


_NN = (((1,), (0,)), ((), ()))
_NT = (((1,), (1,)), ((), ()))
_TN = (((0,), (0,)), ((), ()))


def _head_ones():
    r = lax.broadcasted_iota(jnp.int32, (LANES, LANES), 0) // HEAD_DIM
    c = lax.broadcasted_iota(jnp.int32, (LANES, LANES), 1) // HEAD_DIM
    return jnp.where(r == c, 1.0, 0.0).astype(BF16)


def _head_sum(x, ones_bd):
    hi = x.astype(BF16)
    lo = (x - hi.astype(F32)).astype(BF16)
    return _dot(jnp.concatenate([hi, lo], axis=1), jnp.concatenate([ones_bd, ones_bd], axis=0))


def _rms_mm_kernel(x_ref, g_ref, wt_ref, o_ref, xn_ref):
    @pl.when(pl.program_id(1) == 0)
    def _():
        x = x_ref[...]
        ms = jnp.mean(x * x, axis=-1, keepdims=True)
        xn_ref[...] = ((x * lax.rsqrt(ms + RMS_EPS)) * g_ref[...]).astype(BF16)

    o_ref[...] = _dot_nt(xn_ref[...], wt_ref[...]).astype(o_ref.dtype)


def _rms_glu_kernel(x_ref, g_ref, wg_ref, wu_ref, o_ref, xn_ref):
    @pl.when(pl.program_id(1) == 0)
    def _():
        x = x_ref[...]
        ms = jnp.mean(x * x, axis=-1, keepdims=True)
        xn_ref[...] = ((x * lax.rsqrt(ms + RMS_EPS)) * g_ref[...]).astype(BF16)

    xn = xn_ref[...]
    gate = _dot(xn, wg_ref[...])
    up = _dot(xn, wu_ref[...])
    o_ref[...] = (gate * _sigmoid(gate) * up).astype(o_ref.dtype)


def _rms_matmul(x2d, g, wt_bf16, *, tm, tn, out_dtype=F32):
    m, d = x2d.shape
    n = wt_bf16.shape[0]
    return pl.pallas_call(
        _rms_mm_kernel,
        grid=(m // tm, n // tn),
        in_specs=[pl.BlockSpec((tm, d), lambda i, j: (i, 0)),
                  pl.BlockSpec((1, d), lambda i, j: (0, 0)),
                  pl.BlockSpec((tn, d), lambda i, j: (j, 0))],
        out_specs=pl.BlockSpec((tm, tn), lambda i, j: (i, j)),
        out_shape=jax.ShapeDtypeStruct((m, n), out_dtype),
        scratch_shapes=[pltpu.VMEM((tm, d), BF16)],
        compiler_params=pltpu.CompilerParams(
            dimension_semantics=("parallel", "arbitrary"), vmem_limit_bytes=VMEM_LIMIT),
        name="rms_matmul",
    )(x2d, g.reshape(1, d), wt_bf16)


def _rms_matmul_glu(x2d, g, w_bf16, *, tm, tn):
    m, d = x2d.shape
    n = w_bf16.shape[1] // 2
    up_off = n // tn
    return pl.pallas_call(
        _rms_glu_kernel,
        grid=(m // tm, n // tn),
        in_specs=[pl.BlockSpec((tm, d), lambda i, j: (i, 0)),
                  pl.BlockSpec((1, d), lambda i, j: (0, 0)),
                  pl.BlockSpec((d, tn), lambda i, j: (0, j)),
                  pl.BlockSpec((d, tn), lambda i, j: (0, j + up_off))],
        out_specs=pl.BlockSpec((tm, tn), lambda i, j: (i, j)),
        out_shape=jax.ShapeDtypeStruct((m, n), BF16),
        scratch_shapes=[pltpu.VMEM((tm, d), BF16)],
        compiler_params=pltpu.CompilerParams(
            dimension_semantics=("parallel", "arbitrary"), vmem_limit_bytes=VMEM_LIMIT),
        name="rms_matmul_glu",
    )(x2d, g.reshape(1, d), w_bf16, w_bf16)


def _mm_res_kernel(a_ref, b_ref, r_ref, o_ref):
    o_ref[...] = _dot(a_ref[...], b_ref[...]) + r_ref[...]


def _matmul_res(a, b, res, *, tm, tn):
    m, kk = a.shape
    n = b.shape[1]
    return pl.pallas_call(
        _mm_res_kernel,
        grid=(m // tm, n // tn),
        in_specs=[pl.BlockSpec((tm, kk), lambda i, j: (i, 0)),
                  pl.BlockSpec((kk, tn), lambda i, j: (0, j)),
                  pl.BlockSpec((tm, tn), lambda i, j: (i, j))],
        out_specs=pl.BlockSpec((tm, tn), lambda i, j: (i, j)),
        out_shape=jax.ShapeDtypeStruct((m, n), F32),
        compiler_params=pltpu.CompilerParams(
            dimension_semantics=("parallel", "arbitrary"), vmem_limit_bytes=VMEM_LIMIT),
        name="matmul_res",
    )(a, b, res)


def _merge_kernel(ya_ref, yb_ref, wa_ref, wb_ref, za_ref, zb_ref, o_ref):
    ta = _dot(ya_ref[...], wa_ref[...])
    tb = _dot(yb_ref[...], wb_ref[...])
    o_ref[...] = (_sigmoid(za_ref[...]) * ta + _sigmoid(zb_ref[...]) * tb).astype(o_ref.dtype)


def _merge(ya, yb, wa, wb, proj2d, *, tm, tn):
    m, kk = ya.shape
    n = wa.shape[1]
    oa, ob = C_GA // tn, C_GB // tn
    return pl.pallas_call(
        _merge_kernel,
        grid=(m // tm, n // tn),
        in_specs=[pl.BlockSpec((tm, kk), lambda i, j: (i, 0)),
                  pl.BlockSpec((tm, kk), lambda i, j: (i, 0)),
                  pl.BlockSpec((kk, tn), lambda i, j: (0, j)),
                  pl.BlockSpec((kk, tn), lambda i, j: (0, j)),
                  pl.BlockSpec((tm, tn), lambda i, j: (i, j + oa)),
                  pl.BlockSpec((tm, tn), lambda i, j: (i, j + ob))],
        out_specs=pl.BlockSpec((tm, tn), lambda i, j: (i, j)),
        out_shape=jax.ShapeDtypeStruct((m, n), BF16),
        compiler_params=pltpu.CompilerParams(
            dimension_semantics=("parallel", "arbitrary"), vmem_limit_bytes=VMEM_LIMIT),
        name="merge",
    )(ya, yb, wa, wb, proj2d, proj2d)


def _rwkv_prep_kernel(z_ref, zs_ref, pz_ref, ps_ref, p0z_ref, p0s_ref, mu_ref, mus_ref,
                      w0_ref, w2_ref, a0_ref, a2_ref, g2_ref, kk_ref, ka_ref,
                      r_out, lw_out, k_out, v_out, a_out, b_out, g_out, *, tt):
    first = pl.program_id(1) == 0
    z = z_ref[0]
    zs = zs_ref[0]
    pz = jnp.where(first, p0z_ref[7:8, :], pz_ref[0, 7:8, :])
    ps = jnp.where(first, p0s_ref[7:8, :], ps_ref[0, 7:8, :])
    row = lax.broadcasted_iota(jnp.int32, (tt, 1), 0)

    def lerp(cur, prev_last, mu):
        prev = jnp.where(row == 0, prev_last, pltpu.roll(cur, 1, axis=0))
        return cur + (prev - cur) * mu

    zm = lerp(z, pz, mu_ref[...])
    zsm = lerp(zs, ps, mus_ref[...])
    r = zm[:, C_R:C_R + WIDTH]
    k = zm[:, C_K:C_K + WIDTH]
    v = zm[:, C_V:C_V + WIDTH]
    wd = zsm[:, 0:LANES]
    ad = zsm[:, LANES:2 * LANES]
    gd = zsm[:, 2 * LANES:4 * LANES]

    w_log = -_softplus(-(w0_ref[...] + _dot(jnp.tanh(wd).astype(BF16), w2_ref[...]))) - 0.5
    lw_out[0] = -jnp.exp(w_log)
    a_sig = _sigmoid(a0_ref[...] + _dot(ad.astype(BF16), a2_ref[...]))
    g_out[0] = _dot(_sigmoid(gd).astype(BF16), g2_ref[...]).astype(g_out.dtype)

    ones_bd = _head_ones()
    kk = k * kk_ref[...]
    kk2 = kk * kk
    nrm = jnp.concatenate(
        [_head_sum(kk2[:, p * LANES:(p + 1) * LANES], ones_bd) for p in range(PAIRS)], axis=1)
    kk = kk * lax.rsqrt(jnp.maximum(nrm, 1e-24))
    r_out[0] = r.astype(r_out.dtype)
    k_out[0] = (k * (1.0 + (a_sig - 1.0) * ka_ref[...])).astype(k_out.dtype)
    v_out[0] = v.astype(v_out.dtype)
    a_out[0] = (-kk).astype(a_out.dtype)
    b_out[0] = (kk * a_sig).astype(b_out.dtype)


def _rwkv_prep(proj3d, prev0_z, prev0_s, mu_z, mu_s, w0, w2p, a0, a2p, g2, k_k, k_a, *, tt):
    b, l, _ = proj3d.shape
    s_blk = C_S // S_COLS
    zw = 3 * WIDTH
    row = lambda a: a.reshape(1, -1)
    vec = lambda n: pl.BlockSpec((1, n), lambda bi, i: (0, 0))
    full = lambda r, c: pl.BlockSpec((r, c), lambda bi, i: (0, 0))
    out_spec = pl.BlockSpec((1, tt, WIDTH), lambda bi, i: (bi, i, 0))
    sds = lambda dt: jax.ShapeDtypeStruct((b, l, WIDTH), dt)
    return pl.pallas_call(
        functools.partial(_rwkv_prep_kernel, tt=tt),
        grid=(b, l // tt),
        in_specs=[pl.BlockSpec((1, tt, zw), lambda bi, i: (bi, i, 0)),
                  pl.BlockSpec((1, tt, S_COLS), lambda bi, i: (bi, i, s_blk)),
                  pl.BlockSpec((1, 8, zw), lambda bi, i: (bi, jnp.maximum(i * (tt // 8) - 1, 0), 0)),
                  pl.BlockSpec((1, 8, S_COLS),
                               lambda bi, i: (bi, jnp.maximum(i * (tt // 8) - 1, 0), s_blk)),
                  full(8, zw), full(8, S_COLS),
                  vec(zw), vec(S_COLS),
                  vec(WIDTH), full(LANES, WIDTH), vec(WIDTH), full(LANES, WIDTH),
                  full(GATE_LORA, WIDTH), vec(WIDTH), vec(WIDTH)],
        out_specs=[out_spec] * 7,
        out_shape=[sds(BF16), sds(F32)] + [sds(BF16)] * 5,
        compiler_params=pltpu.CompilerParams(
            dimension_semantics=("parallel", "arbitrary"), vmem_limit_bytes=VMEM_LIMIT),
        name="rwkv_prep",
    )(proj3d, proj3d, proj3d, proj3d, prev0_z, prev0_s, row(mu_z), row(mu_s),
      row(w0), w2p, row(a0), a2p, g2, row(k_k), row(k_a))


def _bf(x):
    return x.astype(BF16)


WKV_CHUNKS = 4


def _wkv_group(a_t, r_t, b_t, k_t, b_h, k_h, v, s, w_total, masks):
    head0, strict, lower, eye = masks
    dg = functools.partial(lax.dot_general, preferred_element_type=F32)
    n = range(len(s))

    def bd(x):
        return jnp.concatenate([jnp.where(head0, x, 0.0), jnp.where(head0, 0.0, x)], axis=0)

    lhs = [_bf(jnp.concatenate([bd(a_t[i]), bd(r_t[i])], axis=0)) for i in n]
    rhs = [_bf(jnp.concatenate([bd(b_t[i]), bd(k_t[i])], axis=0)) for i in n]
    v_bd = [_bf(bd(v[i])) for i in n]
    gram = [dg(lhs[i], rhs[i], _NT) for i in n]
    a_s = [dg(lhs[i], _bf(s[i]), _NT) for i in n]
    n_ab = [jnp.where(strict, g[0:LANES, 0:LANES], 0.0) for g in gram]
    a_ak = [jnp.where(strict, g[0:LANES, LANES:2 * LANES], 0.0) for g in gram]
    a_r = [_bf(jnp.concatenate([jnp.where(lower, g[LANES:2 * LANES, 0:LANES], 0.0),
                                jnp.where(lower, g[LANES:2 * LANES, LANES:2 * LANES], 0.0)], axis=1))
           for g in gram]
    rhs_u = [a_s[i][0:LANES] + dg(_bf(a_ak[i]), v_bd[i], _NN) for i in n]

    p_inv = [eye + x for x in n_ab]
    q = [_bf(x) for x in n_ab]
    for _ in range(5):
        q = [_bf(dg(x, x, _NN)) for x in q]
        p_inv = [p_inv[i] + dg(q[i], _bf(p_inv[i]), _NN) for i in n]
    u = [dg(_bf(p_inv[i]), _bf(rhs_u[i]), _NN) for i in n]

    uv = [jnp.concatenate([_bf(u[i]), v_bd[i]], axis=0) for i in n]
    y_bd = [a_s[i][LANES:2 * LANES] + dg(a_r[i], uv[i], _NN) for i in n]
    y = [x[0:CHUNK] + x[CHUNK:2 * CHUNK] for x in y_bd]
    s_new = [w_total[i] * s[i]
             + dg(uv[i], _bf(jnp.concatenate([bd(b_h[i]), bd(k_h[i])], axis=0)), _TN) for i in n]
    return y, s_new


def _wkv_kernel(r_ref, lw_ref, k_ref, v_ref, a_ref, b_ref, g_ref, s0_ref,
                gnw_ref, gnb_ref, rk_ref, y_ref, sfin_ref, s_scr, *, nc, chunks):
    c = pl.program_id(1)

    @pl.when(c == 0)
    def _():
        s_scr[...] = s0_ref[0]

    rows_all = chunks * CHUNK
    r = r_ref[0].astype(F32)
    lw = lw_ref[0]
    k = k_ref[0].astype(F32)
    v = v_ref[0].astype(F32)
    a = a_ref[0].astype(F32)
    b = b_ref[0].astype(F32)

    ti = lax.broadcasted_iota(jnp.int32, (rows_all, rows_all), 0)
    tj = lax.broadcasted_iota(jnp.int32, (rows_all, rows_all), 1)
    ltri = jnp.where((tj <= ti) & (ti // CHUNK == tj // CHUNK), 1.0, 0.0).astype(BF16)
    hi, mid, lo = _split3(lw)
    cum = _dot(jnp.concatenate([ltri, ltri, ltri], axis=1),
               jnp.concatenate([hi, mid, lo], axis=0))

    lane = lax.broadcasted_iota(jnp.int32, (1, LANES), 1)
    bi = lax.broadcasted_iota(jnp.int32, (LANES, LANES), 0)
    bj = lax.broadcasted_iota(jnp.int32, (LANES, LANES), 1)
    same = (bi // CHUNK) == (bj // CHUNK)
    masks = (lane < HEAD_DIM,
             same & ((bj % CHUNK) < (bi % CHUNK)),
             same & ((bj % CHUNK) <= (bi % CHUNK)),
             jnp.where(bi == bj, 1.0, 0.0))

    ones_bd = _head_ones()
    inv_n = 1.0 / HEAD_DIM
    sls = [slice(p * LANES, (p + 1) * LANES) for p in range(PAIRS)]
    cut = lambda x: [x[:, sl] for sl in sls]
    state = [s_scr[p] for p in range(PAIRS)]
    for sc in range(chunks):
        rows = slice(sc * CHUNK, (sc + 1) * CHUNK)
        cum_c, lw_c = cum[rows], lw[rows]
        total = cum_c[CHUNK - 1:CHUNK, :]
        e_inv = jnp.exp(-cum_c)
        e_end = jnp.exp(total - cum_c)
        ys, state = _wkv_group(
            cut(a[rows] * jnp.exp(cum_c - lw_c)), cut(r[rows] * jnp.exp(cum_c)),
            cut(b[rows] * e_inv), cut(k[rows] * e_inv), cut(b[rows] * e_end), cut(k[rows] * e_end),
            cut(v[rows]), state, cut(jnp.exp(total)), masks)

        y_all = jnp.concatenate(ys, axis=0)
        rk_all = jnp.concatenate([r[rows, sl] * k[rows, sl] * rk_ref[:, sl] for sl in sls], axis=0)
        bonus_all = _head_sum(rk_all, ones_bd)
        d_all = y_all - _head_sum(y_all, ones_bd) * inv_n
        rstd_all = lax.rsqrt(_head_sum(d_all * d_all, ones_bd) * inv_n + GN_EPS)
        for i, sl in enumerate(sls):
            prow = slice(i * CHUNK, (i + 1) * CHUNK)
            yn = d_all[prow] * rstd_all[prow] * gnw_ref[:, sl] + gnb_ref[:, sl]
            y_ref[0, rows, sl] = ((yn + bonus_all[prow] * v[rows, sl])
                                  * g_ref[0, rows, sl].astype(F32)).astype(y_ref.dtype)

    for p in range(PAIRS):
        s_scr[p] = state[p]

        @pl.when(c == nc - 1)
        def _(p=p):
            sfin_ref[0, p] = state[p]


def _wkv(r, lw, k, v, a, b, g, s0, gn_w, gn_b, r_k):
    bsz, l, _ = r.shape
    chunks = min(WKV_CHUNKS, l // CHUNK)
    nc = l // (chunks * CHUNK)
    tok = pl.BlockSpec((1, chunks * CHUNK, WIDTH), lambda bi, c: (bi, c, 0))
    vec = pl.BlockSpec((1, WIDTH), lambda bi, c: (0, 0))
    st_in = pl.BlockSpec((1, PAIRS, LANES, LANES), lambda bi, c: (0, 0, 0, 0))
    st_out = pl.BlockSpec((1, PAIRS, LANES, LANES), lambda bi, c: (bi, 0, 0, 0))
    row = lambda x: x.reshape(1, WIDTH)
    return pl.pallas_call(
        functools.partial(_wkv_kernel, nc=nc, chunks=chunks),
        grid=(bsz, nc),
        in_specs=[tok] * 7 + [st_in, vec, vec, vec],
        out_specs=[tok, st_out],
        out_shape=[jax.ShapeDtypeStruct((bsz, l, WIDTH), BF16),
                   jax.ShapeDtypeStruct((bsz, PAIRS, LANES, LANES), F32)],
        scratch_shapes=[pltpu.VMEM((PAIRS, LANES, LANES), F32)],
        compiler_params=pltpu.CompilerParams(
            dimension_semantics=("parallel", "arbitrary"), vmem_limit_bytes=VMEM_LIMIT),
        name="wkv",
    )(r, lw, k, v, a, b, g, s0, row(gn_w), row(gn_b), row(r_k))


def _fox_prep_kernel(z_ref, zs_ref, c0_ref, qg_ref, kg_ref, fb_ref,
                     qt_out, k_out, vt_out, c_out, carry, *, tt, n_pad):
    i = pl.program_id(1)

    @pl.when(i == 0)
    def _():
        carry[...] = c0_ref[...]

    z = z_ref[0]
    fl = zs_ref[0][:, 0:LANES]
    log_f = -_softplus(-(fl + fb_ref[...]))
    ti = lax.broadcasted_iota(jnp.int32, (tt, tt), 0)
    tj = lax.broadcasted_iota(jnp.int32, (tt, tt), 1)
    ltri = jnp.where(tj <= ti, 1.0, 0.0).astype(BF16)
    hi, mid, lo = _split3(log_f)
    cum = carry[...] + _dot(jnp.concatenate([ltri, ltri, ltri], axis=1),
                            jnp.concatenate([hi, mid, lo], axis=0))
    carry[...] = cum[tt - 1:tt, :]
    c_out[0] = cum[tt - 8:tt, :]
    c_hi, c_mid, c_lo = (t.astype(F32) for t in _split3(cum * LOG2E))
    neg_hi, neg_mid, neg_lo = -c_hi, -c_mid, -c_lo
    if n_pad:
        grow = i * tt + lax.broadcasted_iota(jnp.int32, (tt, 1), 0)
        neg_hi = jnp.where(grow < n_pad, MASKED_KEY, neg_hi)

    for p in range(PAIRS):
        vt_out[0, p, 0] = z[:, 2 * WIDTH + p * LANES:2 * WIDTH + (p + 1) * LANES].T.astype(BF16)
    ones_bd = _head_ones()
    lane = lax.broadcasted_iota(jnp.int32, (1, LANES), 1)
    inv_n = 1.0 / HEAD_DIM
    for p in range(PAIRS):
        zq = z[:, p * LANES:(p + 1) * LANES]
        zk = z[:, WIDTH + p * LANES:WIDTH + (p + 1) * LANES]
        qn = zq * lax.rsqrt(_head_sum(zq * zq, ones_bd) * inv_n + RMS_EPS) * (qg_ref[...] * (ATTN_SCALE * LOG2E))
        kn = zk * lax.rsqrt(_head_sum(zk * zk, ones_bd) * inv_n + RMS_EPS) * kg_ref[...]
        for j in range(2):
            h = 2 * p + j
            own = (lane < HEAD_DIM) if j == 0 else (lane >= HEAD_DIM)
            e = lane - (HEAD_DIM if j == 0 else 0)
            col = lambda t: t[:, FL_LANE + h:FL_LANE + h + 1]
            q_extra = jnp.where((e >= 0) & (e < 3), 1.0, 0.0)
            k_extra = jnp.where(e == 0, col(neg_hi), jnp.where(e == 1, col(neg_mid), jnp.where(
                e == 2, col(neg_lo), 0.0)))
            qt_out[0, h, 0] = jnp.where(own, qn, q_extra).T.astype(BF16)
            k_out[0, h] = jnp.where(own, kn, k_extra).astype(BF16)


def _fox_prep(proj3d, c0, q_g, k_g, f_bias, *, tt, n_pad):
    b, l, _ = proj3d.shape
    zw = 3 * WIDTH
    tile2 = lambda g: jnp.concatenate([g, g]).reshape(1, LANES)
    fb = jnp.concatenate([jnp.zeros((FL_LANE,), F32), f_bias,
                          jnp.zeros((LANES - FL_LANE - HEADS,), F32)]).reshape(1, LANES)
    hd = pl.BlockSpec((1, HEADS, tt, LANES), lambda bi, i: (bi, 0, i, 0))
    vec = pl.BlockSpec((1, LANES), lambda bi, i: (0, 0))
    return pl.pallas_call(
        functools.partial(_fox_prep_kernel, tt=tt, n_pad=n_pad),
        grid=(b, l // tt),
        in_specs=[pl.BlockSpec((1, tt, zw), lambda bi, i: (bi, i, C_FQ // zw)),
                  pl.BlockSpec((1, tt, S_COLS), lambda bi, i: (bi, i, C_S // S_COLS)),
                  vec, vec, vec, vec],
        out_specs=[pl.BlockSpec((1, HEADS, 1, LANES, tt), lambda bi, i: (bi, 0, i, 0, 0)),
                   hd,
                   pl.BlockSpec((1, PAIRS, 1, LANES, tt), lambda bi, i: (bi, 0, i, 0, 0)),
                   pl.BlockSpec((1, 8, LANES), lambda bi, i: (bi, i, 0))],
        out_shape=[jax.ShapeDtypeStruct((b, HEADS, l // tt, LANES, tt), BF16),
                   jax.ShapeDtypeStruct((b, HEADS, l, LANES), BF16),
                   jax.ShapeDtypeStruct((b, PAIRS, l // tt, LANES, tt), BF16),
                   jax.ShapeDtypeStruct((b, (l // tt) * 8, LANES), F32)],
        scratch_shapes=[pltpu.VMEM((1, LANES), F32)],
        compiler_params=pltpu.CompilerParams(
            dimension_semantics=("parallel", "arbitrary"), vmem_limit_bytes=VMEM_LIMIT),
        name="fox_prep",
    )(proj3d, proj3d, c0, tile2(q_g), tile2(k_g), fb)


ATTN_SUBTILES = 8


def _attn_kernel(qt_ref, k_ref, vt_ref, km_ref, vmt_ref, o_ref, s0_buf, s1_buf, p0_buf, p1_buf,
                 acc_buf, *, tk):
    hq = ATTN_SUBTILES
    s_buf = (s0_buf, s1_buf)
    p_buf = (p0_buf, p1_buf)
    qi = pl.program_id(2)
    key_i = lax.broadcasted_iota(jnp.int32, (tk, tk), 0)
    qry_i = lax.broadcasted_iota(jnp.int32, (tk, tk), 1)
    causal = key_i <= qry_i
    every = tuple(range(2 * hq))
    from_sub = lambda u0: tuple(n for n in every if n % hq >= u0)
    qts = [qt_ref[0, n // hq, n % hq] for n in every]

    def real_keys(r):
        off = pl.multiple_of(r * tk, tk)
        return (k_ref[0, 0, pl.ds(off, tk), :], k_ref[0, 1, pl.ds(off, tk), :])

    def stage_a(kblks, slot, which):
        for n in which:
            s_buf[slot][n] = _dot(kblks[n // hq], qts[n])

    def stage_b(slot, stats, alpha, masks):
        which = tuple(masks)
        s = {n: s_buf[slot][n] for n in which}
        for n in which:
            if masks[n] is not None:
                s[n] = jnp.where(masks[n], s[n], MASKED_KEY)
        m_new = {n: jnp.maximum(stats[n][0], jnp.max(s[n], axis=0, keepdims=True)) for n in which}
        a_new = {n: jnp.exp2(stats[n][0] - m_new[n]) for n in which}
        p = {n: jnp.exp2(s[n] - m_new[n]) for n in which}
        for n in which:
            p_buf[slot][n] = p[n].astype(BF16)
        l_new = {n: a_new[n] * stats[n][1] + jnp.sum(p[n], axis=0, keepdims=True) for n in which}
        stats = tuple((m_new[n], l_new[n]) if n in which else stats[n] for n in every)
        alpha = tuple(a_new[n] if n in which else alpha[n] for n in every)
        return stats, alpha

    def stage_c(slot, e, alpha, which):
        vt = jnp.where(e == 0, vmt_ref[0, 0, 0], vt_ref[0, 0, jnp.maximum(e - 1, 0)])
        pv = {n: _dot(vt[(n // hq) * HEAD_DIM:(n // hq + 1) * HEAD_DIM, :], p_buf[slot][n])
              for n in which}
        for n in which:
            acc_buf[n] = alpha[n] * acc_buf[n] + pv[n]

    unmasked = {n: None for n in every}
    stats = ((jnp.full((1, tk), MASKED_KEY, F32), jnp.zeros((1, tk), F32)),) * len(every)
    alpha = (jnp.ones((1, tk), F32),) * len(every)
    acc_buf[...] = jnp.zeros_like(acc_buf)
    p1_buf[...] = jnp.zeros_like(p1_buf)
    stage_a((km_ref[0, 0], km_ref[0, 1]), 0, every)

    def body(i, state):
        stats, alpha_prev = state
        stats, alpha_a = stage_b(0, stats, alpha_prev, unmasked)
        stage_a(real_keys(2 * i), 1, every)
        stage_c(1, jnp.maximum(2 * i - 1, 0), alpha_prev, every)
        stage_a(real_keys(2 * i + 1), 0, every)
        stats, alpha_b = stage_b(1, stats, alpha_a, unmasked)
        stage_c(0, 2 * i, alpha_a, every)
        return stats, alpha_b

    stats, alpha_prev = lax.fori_loop(0, (hq // 2) * qi, body, (stats, alpha))
    e0 = hq * qi
    alphas = {-1: alpha_prev}
    for r in range(1, hq + 3):
        if r <= hq:
            stage_a(real_keys(e0 + r - 1), r % 2, from_sub(r - 1))
        j = r - 1
        if j <= hq:
            masks = unmasked if j == 0 else {n: (causal if n % hq == j - 1 else None)
                                             for n in from_sub(j - 1)}
            stats, alphas[j] = stage_b(j % 2, stats, alphas[j - 1], masks)
        j = r - 2
        if j == -1:
            stage_c(1, jnp.maximum(e0 - 1, 0), alphas[-1], every)
        else:
            stage_c(j % 2, e0 + j, alphas[j], every if j == 0 else from_sub(j - 1))
    for u in range(hq):
        out_t = jnp.concatenate([acc_buf[u] / stats[u][1], acc_buf[hq + u] / stats[hq + u][1]], axis=0)
        o_ref[0, u * tk:(u + 1) * tk, :] = out_t.T.astype(o_ref.dtype)


def _attn(qt, kp, vt, kp_meta, vt_meta, *, tk):
    b, _, l, _ = kp.shape
    nk = vt.shape[2]
    hq = ATTN_SUBTILES
    assert vt.shape[4] == tk and qt.shape[4] == tk and kp_meta.shape[2] == tk and l % (hq * tk) == 0
    return pl.pallas_call(
        functools.partial(_attn_kernel, tk=tk),
        grid=(b, PAIRS, l // (hq * tk)),
        in_specs=[pl.BlockSpec((1, 2, hq, LANES, tk), lambda bi, p, qi: (bi, p, qi, 0, 0)),
                  pl.BlockSpec((1, 2, l, LANES), lambda bi, p, qi: (bi, p, 0, 0)),
                  pl.BlockSpec((1, 1, nk, LANES, tk), lambda bi, p, qi: (bi, p, 0, 0, 0)),
                  pl.BlockSpec((1, 2, tk, LANES), lambda bi, p, qi: (0, p, 0, 0)),
                  pl.BlockSpec((1, 1, 1, LANES, tk), lambda bi, p, qi: (0, p, 0, 0, 0))],
        out_specs=pl.BlockSpec((1, hq * tk, LANES), lambda bi, p, qi: (bi, qi, p)),
        out_shape=jax.ShapeDtypeStruct((b, l, WIDTH), BF16),
        scratch_shapes=[pltpu.VMEM((2 * hq, tk, tk), F32), pltpu.VMEM((2 * hq, tk, tk), F32),
                        pltpu.VMEM((2 * hq, tk, tk), BF16), pltpu.VMEM((2 * hq, tk, tk), BF16),
                        pltpu.VMEM((2 * hq, HEAD_DIM, tk), F32)],
        compiler_params=pltpu.CompilerParams(
            dimension_semantics=("parallel", "parallel", "arbitrary"), vmem_limit_bytes=VMEM_LIMIT),
        name="attn",
    )(qt, kp, vt, kp_meta, vt_meta)


def _pick(n, prefs):
    for t in prefs:
        if n % t == 0:
            return t
    raise ValueError(f"no tile for {n}")


def _pack_wt_kernel(wt_ref, o_ref):
    rw, fx = 3 * WIDTH + DECAY_LORA + AAA_LORA + GATE_LORA, 3 * WIDTH + HEADS
    wd0, ad0, gd0 = 3 * WIDTH, 3 * WIDTH + DECAY_LORA, 3 * WIDTH + DECAY_LORA + AAA_LORA
    fl0 = rw + 3 * WIDTH

    def put(dst, src, n):
        o_ref[dst:dst + n, :] = wt_ref[src:src + n, :].astype(BF16)

    def zero(dst, n):
        o_ref[dst:dst + n, :] = jnp.zeros((n, o_ref.shape[1]), BF16)

    put(C_R, 0, 3 * WIDTH)
    put(C_FQ, rw, 3 * WIDTH)
    put(C_GA, rw + fx, 2 * D_MODEL)
    put(C_S, wd0, DECAY_LORA)
    put(C_S + FL_LANE, fl0, HEADS)
    zero(C_S + FL_LANE + HEADS, LANES - FL_LANE - HEADS)
    put(C_S + LANES, ad0, AAA_LORA)
    zero(C_S + LANES + AAA_LORA, LANES - AAA_LORA)
    put(C_S + 2 * LANES, gd0, GATE_LORA)


def _pack_w_in_t(wt, *, cols=256):
    n, d = wt.shape
    return pl.pallas_call(
        _pack_wt_kernel,
        grid=(d // cols,),
        in_specs=[pl.BlockSpec((n, cols), lambda i: (0, i))],
        out_specs=pl.BlockSpec((N_PACK, cols), lambda i: (0, i)),
        out_shape=jax.ShapeDtypeStruct((N_PACK, d), BF16),
        compiler_params=pltpu.CompilerParams(
            dimension_semantics=("parallel",), vmem_limit_bytes=VMEM_LIMIT),
        name="pack_w_in",
    )(wt)


def _pack_mu(mu):
    wd0, ad0, gd0 = 3 * WIDTH, 3 * WIDTH + DECAY_LORA, 3 * WIDTH + DECAY_LORA + AAA_LORA
    z = lambda n: jnp.zeros((n,), mu.dtype)
    mu_s = jnp.concatenate([mu[wd0:ad0], z(LANES - DECAY_LORA), mu[ad0:gd0], z(LANES - AAA_LORA),
                            mu[gd0:]])
    return mu[:3 * WIDTH], mu_s


def _pad_rows(w, rows):
    return jnp.concatenate([w, jnp.zeros((rows - w.shape[0], w.shape[1]), w.dtype)], axis=0)


def _mixer_inputs(h2d, bsz, norm_g, w_pack, tm, tn):
    proj = _rms_matmul(h2d, norm_g, w_pack, tm=tm, tn=tn)
    return proj, proj.reshape(bsz, h2d.shape[0] // bsz, N_PACK)


def kernel(x, meta_tokens, norm1_g, w_in, rwkv_mu, rwkv_w0, rwkv_w2, rwkv_a0, rwkv_a2, rwkv_g2,
           rwkv_k_k, rwkv_k_a, rwkv_r_k, rwkv_gn_w, rwkv_gn_b, fox_q_norm_g, fox_k_norm_g,
           fox_f_bias, w_branch_a, w_branch_b, w_o, norm2_g, w_gate_up, w_down):
    bsz, seq, d = x.shape
    assert d == D_MODEL and norm1_g.shape[0] == 1 and seq % (ATTN_SUBTILES * 256) == 0
    m = bsz * seq
    x2d = x.reshape(m, d)
    meta2d = jnp.concatenate(
        [jnp.zeros((META_ROWS - N_META, d), x.dtype), meta_tokens.astype(x.dtype)], axis=0)

    w_pack = _pack_w_in_t(jnp.swapaxes(w_in[0], 0, 1))
    mu_z, mu_s = _pack_mu(rwkv_mu[0])
    w2p = _pad_rows(rwkv_w2[0], LANES).astype(BF16)
    a2p = _pad_rows(rwkv_a2[0], LANES).astype(BF16)
    g2 = rwkv_g2[0].astype(BF16)
    r_k = rwkv_r_k[0].reshape(WIDTH)

    tm = _pick(m, (1024, 512, 256))
    tt = _pick(seq, (256,))
    tn_proj, tn_wo, tn_ffn = 1536, D_MODEL, 512

    _, pm3 = _mixer_inputs(meta2d, 1, norm1_g[0], w_pack, META_ROWS, tn=tn_proj)
    zero_z = jnp.zeros((8, 3 * WIDTH), F32)
    zero_s = jnp.zeros((8, S_COLS), F32)
    prep = functools.partial(_rwkv_prep, mu_z=mu_z, mu_s=mu_s, w0=rwkv_w0[0], w2p=w2p,
                             a0=rwkv_a0[0], a2p=a2p, g2=g2, k_k=rwkv_k_k[0], k_a=rwkv_k_a[0])
    wkv = functools.partial(_wkv, gn_w=rwkv_gn_w[0], gn_b=rwkv_gn_b[0], r_k=r_k)
    fox = functools.partial(_fox_prep, q_g=fox_q_norm_g[0], k_g=fox_k_norm_g[0],
                            f_bias=fox_f_bias[0])
    mr = prep(pm3, zero_z, zero_s, tt=META_ROWS)
    _, s_meta = wkv(*mr, jnp.zeros((1, PAIRS, LANES, LANES), F32))
    _, kp_meta, vt_meta, c_meta = fox(pm3, jnp.zeros((1, LANES), F32), tt=META_ROWS,
                                      n_pad=META_ROWS - N_META)

    proj2d, p3 = _mixer_inputs(x2d, bsz, norm1_g[0], w_pack, tm, tn=tn_proj)
    rr = prep(p3, pm3[0, META_ROWS - 8:, 0:3 * WIDTH], pm3[0, META_ROWS - 8:, C_S:], tt=tt)
    ya, _ = wkv(*rr, s_meta)
    qt, kp, vt, _ = fox(p3, c_meta[0, 7:8, :], tt=tt, n_pad=0)
    yb = _attn(qt, kp, vt, kp_meta, vt_meta, tk=tt)

    merged = _merge(ya.reshape(m, WIDTH), yb.reshape(m, WIDTH), w_branch_a[0].astype(BF16),
                    w_branch_b[0].astype(BF16), proj2d, tm=tm // 2, tn=D_MODEL)
    h1 = _matmul_res(merged, w_o[0].astype(BF16), x2d, tm=tm // 2, tn=tn_wo)
    act = _rms_matmul_glu(h1, norm2_g[0], w_gate_up[0].astype(BF16), tm=tm, tn=tn_ffn)
    out = _matmul_res(act, w_down[0].astype(BF16), h1, tm=tm, tn=tn_ffn)
    return out.reshape(bsz, seq, d)
```
